```python
import math
import jax
import jax.numpy as jnp
from jax import lax
import numpy as np

D_MODEL = 2048
BATCH = 4
SEQ = 4096
DEPTH = 2

DIFF_HEADS = 8
DIFF_HEAD_DIM = 64
DIFF_WIDTH = DIFF_HEADS * 2 * DIFF_HEAD_DIM
PARTIAL_ROPE_DIM = DIFF_HEAD_DIM // 4
MLA_HEADS = 8
MLA_Q_RANK = 512
MLA_KV_RANK = 256
MLA_NOPE_DIM = 128
MLA_ROPE_DIM = 64
MLA_V_DIM = 128
MLA_WIDTH = MLA_HEADS * MLA_V_DIM
IN_SIZES = (DIFF_WIDTH, DIFF_WIDTH, DIFF_WIDTH, MLA_Q_RANK, MLA_KV_RANK, MLA_ROPE_DIM, D_MODEL, D_MODEL)
IN_COLS = 3 * DIFF_WIDTH + MLA_Q_RANK + MLA_KV_RANK + MLA_ROPE_DIM + 2 * D_MODEL
D_FF = -((-8 * D_MODEL) // (3 * 256)) * 256
PLE_DIM = 256
ROPE_THETA = 500000.0
Q_BLOCK = 128
NORM_EPS = 1e-6
SUBLN_EPS = 1e-5

kernel_name = 'hybrid_diffattn_mla_gated_encoder'


def rmsnorm(x, g, eps=NORM_EPS):
    xf = x.astype(jnp.float32)
    y = xf * lax.rsqrt(jnp.mean(xf * xf, axis=-1, keepdims=True) + eps)
    return (y * g.astype(jnp.float32)).astype(x.dtype)


def rope_tables(positions, dim):
    inv_freq = ROPE_THETA ** (-jnp.arange(0, dim, 2, dtype=jnp.float32) / dim)
    ang = positions.astype(jnp.float32)[..., None] * inv_freq
    return jnp.cos(ang), jnp.sin(ang)


def apply_rope(x, cos, sin):
    x1, x2 = jnp.split(x.astype(jnp.float32), 2, axis=-1)
    c = cos[:, :, None, :]
    s = sin[:, :, None, :]
    return jnp.concatenate([x1 * c - x2 * s, x2 * c + x1 * s], axis=-1).astype(x.dtype)


def apply_partial_rope(x, cos, sin):
    return jnp.concatenate([apply_rope(x[..., :PARTIAL_ROPE_DIM], cos, sin), x[..., PARTIAL_ROPE_DIM:]], axis=-1)


def to_blocks(t):
    b, s = t.shape[:2]
    return jnp.swapaxes(t.reshape((b, s // Q_BLOCK, Q_BLOCK) + t.shape[2:]), 0, 1)


def from_blocks(t):
    t = jnp.swapaxes(t, 0, 1)
    return t.reshape((t.shape[0], t.shape[1] * t.shape[2]) + t.shape[3:])


def diff_attention(q, k, v, lam):
    b, s = q.shape[:2]
    qh = q.reshape(b, s, DIFF_HEADS, 2, DIFF_HEAD_DIM)
    kh = k.reshape(b, s, DIFF_HEADS, 2, DIFF_HEAD_DIM)
    scale = DIFF_HEAD_DIM ** -0.5

    def one_block(qb):
        scores = jnp.einsum('bqhcd,bkhcd->bhcqk', qb, kh).astype(jnp.float32) * scale
        probs = jax.nn.softmax(scores, axis=-1)
        weights = probs[:, :, 0] - lam * probs[:, :, 1]
        return jnp.einsum('bhqk,bkhe->bqhe', weights.astype(v.dtype), v)

    return from_blocks(lax.map(one_block, to_blocks(qh)))


def latent_attention(q_nope, q_rope, k_nope, k_rope, v):
    scale = (MLA_NOPE_DIM + MLA_ROPE_DIM) ** -0.5

    def one_block(qs):
        qn, qr = qs
        scores = (jnp.einsum('bqhd,bkhd->bhqk', qn, k_nope).astype(jnp.float32)
                  + jnp.einsum('bqhr,bkr->bhqk', qr, k_rope).astype(jnp.float32)) * scale
        probs = jax.nn.softmax(scores, axis=-1)
        return jnp.einsum('bhqk,bkhd->bqhd', probs.astype(v.dtype), v)

    return from_blocks(lax.map(one_block, (to_blocks(q_nope), to_blocks(q_rope))))


def setup_inputs(seed: int = 0) -> dict:
    key = jax.random.key(seed)
    ks = iter(jax.random.split(key, 32))
    f32 = jnp.float32

    def w(shape, fan_in):
        return jax.random.normal(next(ks), shape, f32) * (fan_in ** -0.5)

    def gain(shape):
        return 1.0 + 0.02 * jax.random.normal(next(ks), shape, f32)

    x = jax.random.normal(next(ks), (BATCH, SEQ, D_MODEL), f32)
    p = jax.random.normal(next(ks), (DEPTH, BATCH, SEQ, PLE_DIM), f32)
    offsets = jax.random.randint(next(ks), (BATCH, 1), 0, SEQ, dtype=jnp.int32)
    positions = jnp.arange(SEQ, dtype=jnp.int32)[None, :] + offsets
    return {
        'x': x,
        'p': p,
        'positions': positions,
        'g_mix': gain((DEPTH, D_MODEL)),
        'w_in': w((DEPTH, D_MODEL, IN_COLS), D_MODEL),
        'lambda_q1': 0.1 * jax.random.normal(next(ks), (DEPTH, DIFF_HEAD_DIM), f32),
        'lambda_k1': 0.1 * jax.random.normal(next(ks), (DEPTH, DIFF_HEAD_DIM), f32),
        'lambda_q2': 0.1 * jax.random.normal(next(ks), (DEPTH, DIFF_HEAD_DIM), f32),
        'lambda_k2': 0.1 * jax.random.normal(next(ks), (DEPTH, DIFF_HEAD_DIM), f32),
        'g_subln': gain((DEPTH, 2 * DIFF_HEAD_DIM)),
        'g_q_latent': gain((DEPTH, MLA_Q_RANK)),
        'w_q_up': w((DEPTH, MLA_Q_RANK, MLA_HEADS * (MLA_NOPE_DIM + MLA_ROPE_DIM)), MLA_Q_RANK),
        'g_kv_latent': gain((DEPTH, MLA_KV_RANK)),
        'w_kv_up': w((DEPTH, MLA_KV_RANK, MLA_HEADS * (MLA_NOPE_DIM + MLA_V_DIM)), MLA_KV_RANK),
        'w_branch_diff': w((DEPTH, DIFF_WIDTH, D_MODEL), DIFF_WIDTH),
        'w_branch_mla': w((DEPTH, MLA_WIDTH, D_MODEL), MLA_WIDTH),
        'w_out': w((DEPTH, D_MODEL, D_MODEL), D_MODEL),
        'g_ffn': gain((DEPTH, D_MODEL)),
        'w_gate_up': w((DEPTH, D_MODEL, 2 * D_FF), D_MODEL),
        'w_down': w((DEPTH, D_FF, D_MODEL), D_FF),
        'w_ple_in': w((DEPTH, PLE_DIM, D_MODEL), PLE_DIM),
        'g_ple': gain((DEPTH, D_MODEL)),
        'w_ple_gate': w((DEPTH, D_MODEL, D_MODEL), D_MODEL),
        'g_final': gain((D_MODEL,)),
    }


def reference(x, p, positions, g_mix, w_in, lambda_q1, lambda_k1, lambda_q2, lambda_k2, g_subln,
              g_q_latent, w_q_up, g_kv_latent, w_kv_up, w_branch_diff, w_branch_mla, w_out,
              g_ffn, w_gate_up, w_down, w_ple_in, g_ple, w_ple_gate, g_final):
    f32 = jnp.float32
    b, s, _ = x.shape
    cos_p, sin_p = rope_tables(positions, PARTIAL_ROPE_DIM)
    cos_m, sin_m = rope_tables(positions, MLA_ROPE_DIM)
    split_at = [int(c) for c in np.cumsum(IN_SIZES)[:-1]]
    for i in range(DEPTH):
        lam_init = 0.8 - 0.6 * math.exp(-0.3 * i)
        h = rmsnorm(x, g_mix[i])
        z = h @ w_in[i]
        dq, dk, dv, cq, ckv, kr, ga, gb = jnp.split(z, split_at, axis=-1)

        dq = apply_partial_rope(dq.reshape(b, s, 2 * DIFF_HEADS, DIFF_HEAD_DIM), cos_p, sin_p)
        dk = apply_partial_rope(dk.reshape(b, s, 2 * DIFF_HEADS, DIFF_HEAD_DIM), cos_p, sin_p)
        dv = dv.reshape(b, s, DIFF_HEADS, 2 * DIFF_HEAD_DIM)
        lam = (jnp.exp(jnp.sum(lambda_q1[i].astype(f32) * lambda_k1[i].astype(f32)))
               - jnp.exp(jnp.sum(lambda_q2[i].astype(f32) * lambda_k2[i].astype(f32))) + lam_init)
        od = diff_attention(dq, dk, dv, lam)
        od = rmsnorm(od, g_subln[i], SUBLN_EPS) * (1.0 - lam_init)
        y_diff = od.reshape(b, s, DIFF_WIDTH) @ w_branch_diff[i]

        qf = (rmsnorm(cq, g_q_latent[i]) @ w_q_up[i]).reshape(b, s, MLA_HEADS, MLA_NOPE_DIM + MLA_ROPE_DIM)
        q_nope = qf[..., :MLA_NOPE_DIM]
        q_rope = apply_rope(qf[..., MLA_NOPE_DIM:], cos_m, sin_m)
        kvf = (rmsnorm(ckv, g_kv_latent[i]) @ w_kv_up[i]).reshape(b, s, MLA_HEADS, MLA_NOPE_DIM + MLA_V_DIM)
        k_nope = kvf[..., :MLA_NOPE_DIM]
        v_mla = kvf[..., MLA_NOPE_DIM:]
        k_rope = apply_rope(kr[:, :, None, :], cos_m, sin_m)[:, :, 0, :]
        om = latent_attention(q_nope, q_rope, k_nope, k_rope, v_mla)
        y_mla = om.reshape(b, s, MLA_WIDTH) @ w_branch_mla[i]

        merged = jax.nn.sigmoid(ga) * y_diff + jax.nn.sigmoid(gb) * y_mla
        x = x + merged @ w_out[i]

        gu = rmsnorm(x, g_ffn[i]) @ w_gate_up[i]
        gate, up = jnp.split(gu, 2, axis=-1)
        x = x + (jax.nn.silu(gate) * up) @ w_down[i]

        e = p[i] @ w_ple_in[i]
        x = x + jax.nn.sigmoid(rmsnorm(x, g_ple[i]) @ w_ple_gate[i]) * e
    return rmsnorm(x, g_final)
```

```python
import functools
import math

import jax
import jax.numpy as jnp
from jax import lax
from jax.experimental import pallas as pl
from jax.experimental.pallas import tpu as pltpu

F32 = jnp.float32
BF16 = jnp.bfloat16

DIFF_HEADS = 8
DIFF_HEAD_DIM = 64
DIFF_WIDTH = DIFF_HEADS * 2 * DIFF_HEAD_DIM
PARTIAL_ROPE_DIM = DIFF_HEAD_DIM // 4
MLA_HEADS = 8
MLA_Q_RANK = 512
MLA_KV_RANK = 256
MLA_NOPE_DIM = 128
MLA_ROPE_DIM = 64
MLA_V_DIM = 128
MLA_QK_PAD = 256
ROPE_THETA = 500000.0
NORM_EPS = 1e-6
SUBLN_EPS = 1e-5
LANES = 128
VMEM_LIMIT_BYTES = 56 * 1024 * 1024


def _cparams(*sem):
    return pltpu.CompilerParams(dimension_semantics=sem, vmem_limit_bytes=VMEM_LIMIT_BYTES)


def _dot(a, b):
    return jnp.dot(a, b, preferred_element_type=F32)


def _sigmoid(x):
    return 1.0 / (1.0 + jnp.exp(-x))


def _rms(x, g, eps):
    return x * lax.rsqrt(jnp.mean(x * x, axis=-1, keepdims=True) + eps) * g


def _rope_chunk(z, c, s_lo, s_hi, half):
    return z * c + pltpu.roll(z, LANES - half, 1) * s_lo + pltpu.roll(z, half, 1) * s_hi


def _rmsnorm_kernel(x_ref, g_ref, o_ref):
    o_ref[...] = _rms(x_ref[...], g_ref[...], NORM_EPS).astype(o_ref.dtype)


def _rmsnorm(x, g, out_dtype, tm):
    m, d = x.shape
    return pl.pallas_call(
        _rmsnorm_kernel,
        grid=(m // tm,),
        in_specs=[pl.BlockSpec((tm, d), lambda i: (i, 0)), pl.BlockSpec((1, d), lambda i: (0, 0))],
        out_specs=pl.BlockSpec((tm, d), lambda i: (i, 0)),
        out_shape=jax.ShapeDtypeStruct((m, d), out_dtype),
        compiler_params=_cparams("parallel"),
        name="rmsnorm",
    )(x, g.reshape(1, d))


def _qkv_kernel(h_ref, w_ref, c_ref, slo_ref, shi_ref, o_ref, *, n_rope_tiles, tn):
    j = pl.program_id(1)
    z = _dot(h_ref[...], w_ref[...])

    @pl.when(j < n_rope_tiles)
    def _():
        c, s_lo, s_hi = c_ref[...], slo_ref[...], shi_ref[...]
        for t in range(tn // LANES):
            sl = slice(t * LANES, (t + 1) * LANES)
            o_ref[:, sl] = _rope_chunk(z[:, sl], c, s_lo, s_hi, PARTIAL_ROPE_DIM // 2).astype(o_ref.dtype)

    @pl.when(j >= n_rope_tiles)
    def _():
        o_ref[...] = z.astype(o_ref.dtype)


def _qkv_proj(h, w, tabs, tm, tn):
    m, k = h.shape
    n = w.shape[1]
    tab_spec = pl.BlockSpec((tm, LANES), lambda i, j: (i, 0))
    return pl.pallas_call(
        functools.partial(_qkv_kernel, n_rope_tiles=2 * DIFF_WIDTH // tn, tn=tn),
        grid=(m // tm, n // tn),
        in_specs=[pl.BlockSpec((tm, k), lambda i, j: (i, 0)), pl.BlockSpec((k, tn), lambda i, j: (0, j)),
                  tab_spec, tab_spec, tab_spec],
        out_specs=pl.BlockSpec((tm, tn), lambda i, j: (i, j)),
        out_shape=jax.ShapeDtypeStruct((m, n), BF16),
        compiler_params=_cparams("parallel", "arbitrary"),
        name="qkv_proj",
    )(h, w, *tabs)


def _latent_kernel(h_ref, wl_ref, gq_ref, wq_ref, gkv_ref, wk_ref, wv_ref, c_ref, slo_ref, shi_ref,
                   q_ref, k_ref, v_ref, *, scale):
    z = _dot(h_ref[...], wl_ref[...])
    cq = _rms(z[:, :MLA_Q_RANK], gq_ref[...], NORM_EPS).astype(BF16)
    ckv = _rms(z[:, MLA_Q_RANK:MLA_Q_RANK + MLA_KV_RANK], gkv_ref[...], NORM_EPS).astype(BF16)
    kr_off = MLA_Q_RANK + MLA_KV_RANK
    c, s_lo, s_hi = c_ref[...], slo_ref[...], shi_ref[...]
    half = MLA_ROPE_DIM // 2
    kr = _rope_chunk(z[:, kr_off:kr_off + LANES], c, s_lo, s_hi, half).astype(BF16)
    qf = _dot(cq, wq_ref[...]) * scale
    kn = _dot(ckv, wk_ref[...])
    for h in range(MLA_HEADS):
        o = h * MLA_QK_PAD
        q_ref[:, o:o + LANES] = qf[:, o:o + LANES].astype(BF16)
        q_ref[:, o + LANES:o + 2 * LANES] = _rope_chunk(qf[:, o + LANES:o + 2 * LANES], c, s_lo, s_hi, half).astype(BF16)
        k_ref[:, o:o + LANES] = kn[:, h * LANES:(h + 1) * LANES].astype(BF16)
        k_ref[:, o + LANES:o + 2 * LANES] = kr
    v_ref[...] = _dot(ckv, wv_ref[...]).astype(BF16)


def _latent_proj(h, wl, gq, wq, gkv, wk, wv, tabs, tm):
    m, k = h.shape
    full = lambda a: pl.BlockSpec(a.shape, lambda i: (0, 0))
    tab_spec = pl.BlockSpec((tm, LANES), lambda i: (i, 0))
    nq, nv = MLA_HEADS * MLA_QK_PAD, MLA_HEADS * MLA_V_DIM
    row = lambda n: pl.BlockSpec((tm, n), lambda i: (i, 0))
    return pl.pallas_call(
        functools.partial(_latent_kernel, scale=(MLA_NOPE_DIM + MLA_ROPE_DIM) ** -0.5),
        grid=(m // tm,),
        in_specs=[row(k), full(wl), full(gq), full(wq), full(gkv), full(wk), full(wv), tab_spec, tab_spec, tab_spec],
        out_specs=[row(nq), row(nq), row(nv)],
        out_shape=[jax.ShapeDtypeStruct((m, nq), BF16), jax.ShapeDtypeStruct((m, nq), BF16),
                   jax.ShapeDtypeStruct((m, nv), BF16)],
        compiler_params=_cparams("parallel"),
        name="latent_proj",
    )(h, wl, gq, wq, gkv, wk, wv, *tabs)


def _softmax_pv_t(k, qe, vt):
    s_t = lax.dot_general(k, qe, (((1,), (1,)), ((), ())), preferred_element_type=F32)
    m = jnp.max(s_t, axis=0, keepdims=True)
    p = jnp.exp(s_t - m)
    l = jnp.sum(p, axis=0, keepdims=True)
    o_t = _dot(vt, p.astype(BF16))
    return o_t / l


def _diff_attn_kernel(lam_ref, g_ref, q_ref, k_ref, v_ref, o_ref, vt_ref, *, lam_init):
    @pl.when(pl.program_id(2) == 0)
    def _():
        vt_ref[...] = v_ref[...].astype(F32).T.astype(BF16)

    lp = lam_ref[...]
    lam = (jnp.exp(jnp.sum(lp[0:1] * lp[1:2], axis=-1, keepdims=True))
           - jnp.exp(jnp.sum(lp[2:3] * lp[3:4], axis=-1, keepdims=True)) + lam_init)
    q = q_ref[...]
    lane = lax.broadcasted_iota(jnp.int32, q.shape, 1)
    zero = jnp.zeros_like(q)
    k = k_ref[...]
    vt = vt_ref[...]
    o1 = _softmax_pv_t(k, jnp.where(lane < DIFF_HEAD_DIM, q, zero), vt)
    o2 = _softmax_pv_t(k, jnp.where(lane >= DIFF_HEAD_DIM, q, zero), vt)
    o = (o1 - lam * o2).T
    o_ref[...] = (_rms(o, g_ref[...], SUBLN_EPS) * (1.0 - lam_init)).astype(o_ref.dtype)


def _diff_attention(qkv, lam_params, g_subln, lam_init, tq):
    b, s, _ = qkv.shape
    dv = 2 * DIFF_HEAD_DIM
    nh = DIFF_HEADS
    return pl.pallas_call(
        functools.partial(_diff_attn_kernel, lam_init=lam_init),
        grid=(b, nh, s // tq),
        in_specs=[pl.BlockSpec(lam_params.shape, lambda bi, h, qi: (0, 0)),
                  pl.BlockSpec((1, dv), lambda bi, h, qi: (0, 0)),
                  pl.BlockSpec((None, tq, dv), lambda bi, h, qi: (bi, qi, h)),
                  pl.BlockSpec((None, s, dv), lambda bi, h, qi: (bi, 0, nh + h)),
                  pl.BlockSpec((None, s, dv), lambda bi, h, qi: (bi, 0, 2 * nh + h))],
        out_specs=pl.BlockSpec((None, tq, dv), lambda bi, h, qi: (bi, qi, h)),
        out_shape=jax.ShapeDtypeStruct((b, s, nh * dv), BF16),
        scratch_shapes=[pltpu.VMEM((dv, s), BF16)],
        compiler_params=_cparams("parallel", "parallel", "arbitrary"),
        name="diff_attention",
    )(lam_params, g_subln.reshape(1, dv), qkv, qkv, qkv)


def _mla_attn_kernel(q_ref, k_ref, v_ref, o_ref, vt_ref):
    @pl.when(pl.program_id(2) == 0)
    def _():
        vt_ref[...] = v_ref[...].astype(F32).T.astype(BF16)

    o_ref[...] = _softmax_pv_t(k_ref[...], q_ref[...], vt_ref[...]).T.astype(o_ref.dtype)


def _mla_attention(q, k, v, tq):
    b, s, _ = q.shape
    nh, dq, dv = MLA_HEADS, MLA_QK_PAD, MLA_V_DIM
    return pl.pallas_call(
        _mla_attn_kernel,
        grid=(b, nh, s // tq),
        in_specs=[pl.BlockSpec((None, tq, dq), lambda bi, h, qi: (bi, qi, h)),
                  pl.BlockSpec((None, s, dq), lambda bi, h, qi: (bi, 0, h)),
                  pl.BlockSpec((None, s, dv), lambda bi, h, qi: (bi, 0, h))],
        out_specs=pl.BlockSpec((None, tq, dv), lambda bi, h, qi: (bi, qi, h)),
        out_shape=jax.ShapeDtypeStruct((b, s, nh * dv), BF16),
        scratch_shapes=[pltpu.VMEM((dv, s), BF16)],
        compiler_params=_cparams("parallel", "parallel", "arbitrary"),
        name="mla_attention",
    )(q, k, v)


def _merge_kernel(h_ref, od_ref, om_ref, wga_ref, wgb_ref, wbd_ref, wbm_ref, o_ref):
    h = h_ref[...]
    ga = _sigmoid(_dot(h, wga_ref[...]))
    gb = _sigmoid(_dot(h, wgb_ref[...]))
    o_ref[...] = (ga * _dot(od_ref[...], wbd_ref[...]) + gb * _dot(om_ref[...], wbm_ref[...])).astype(o_ref.dtype)


def _merge(h, od, om, wga, wgb, wbd, wbm, tm, tn):
    m, d = h.shape
    n = wga.shape[1]
    row = lambda a: pl.BlockSpec((tm, a.shape[1]), lambda i, j: (i, 0))
    col = lambda a: pl.BlockSpec((a.shape[0], tn), lambda i, j: (0, j))
    return pl.pallas_call(
        _merge_kernel,
        grid=(m // tm, n // tn),
        in_specs=[row(h), row(od), row(om), col(wga), col(wgb), col(wbd), col(wbm)],
        out_specs=pl.BlockSpec((tm, tn), lambda i, j: (i, j)),
        out_shape=jax.ShapeDtypeStruct((m, n), BF16),
        compiler_params=_cparams("parallel", "arbitrary"),
        name="gated_merge",
    )(h, od, om, wga, wgb, wbd, wbm)


def _resid_mm_kernel(x_ref, a_ref, w_ref, o_ref):
    o_ref[...] = x_ref[...] + _dot(a_ref[...], w_ref[...])


def _resid_matmul(x, a, w, tm, tn):
    m, n = x.shape
    k = a.shape[1]
    return pl.pallas_call(
        _resid_mm_kernel,
        grid=(m // tm, n // tn),
        in_specs=[pl.BlockSpec((tm, tn), lambda i, j: (i, j)), pl.BlockSpec((tm, k), lambda i, j: (i, 0)),
                  pl.BlockSpec((k, tn), lambda i, j: (0, j))],
        out_specs=pl.BlockSpec((tm, tn), lambda i, j: (i, j)),
        out_shape=jax.ShapeDtypeStruct((m, n), F32),
        input_output_aliases={0: 0},
        compiler_params=_cparams("parallel", "arbitrary"),
        name="resid_matmul",
    )(x, a, w)


def _swiglu_kernel(h_ref, wg_ref, wu_ref, o_ref):
    h = h_ref[...]
    g = _dot(h, wg_ref[...])
    o_ref[...] = (g * _sigmoid(g) * _dot(h, wu_ref[...])).astype(o_ref.dtype)


def _swiglu(h, wg, wu, tm, tn):
    m, d = h.shape
    n = wg.shape[1]
    return pl.pallas_call(
        _swiglu_kernel,
        grid=(m // tm, n // tn),
        in_specs=[pl.BlockSpec((tm, d), lambda i, j: (i, 0)), pl.BlockSpec((d, tn), lambda i, j: (0, j)),
                  pl.BlockSpec((d, tn), lambda i, j: (0, j))],
        out_specs=pl.BlockSpec((tm, tn), lambda i, j: (i, j)),
        out_shape=jax.ShapeDtypeStruct((m, n), BF16),
        compiler_params=_cparams("parallel", "arbitrary"),
        name="swiglu",
    )(h, wg, wu)


def _ple_kernel(x_ref, h_ref, p_ref, wg_ref, wp_ref, o_ref):
    gate = _sigmoid(_dot(h_ref[...], wg_ref[...]))
    o_ref[...] = x_ref[...] + gate * _dot(p_ref[...].astype(BF16), wp_ref[...])


def _ple(x, h, p, wg, wp, tm, tn):
    m, n = x.shape
    d, e = h.shape[1], p.shape[1]
    return pl.pallas_call(
        _ple_kernel,
        grid=(m // tm, n // tn),
        in_specs=[pl.BlockSpec((tm, tn), lambda i, j: (i, j)), pl.BlockSpec((tm, d), lambda i, j: (i, 0)),
                  pl.BlockSpec((tm, e), lambda i, j: (i, 0)), pl.BlockSpec((d, tn), lambda i, j: (0, j)),
                  pl.BlockSpec((e, tn), lambda i, j: (0, j))],
        out_specs=pl.BlockSpec((tm, tn), lambda i, j: (i, j)),
        out_shape=jax.ShapeDtypeStruct((m, n), F32),
        input_output_aliases={0: 0},
        compiler_params=_cparams("parallel", "arbitrary"),
        name="ple_gate",
    )(x, h, p, wg, wp)


def _rope_tables(positions, dim, period):
    inv_freq = ROPE_THETA ** (-jnp.arange(0, dim, 2, dtype=F32) / dim)
    ang = positions.astype(F32).reshape(-1, 1) * inv_freq
    cos, sin = jnp.cos(ang), jnp.sin(ang)
    m, half = ang.shape
    pad = period - dim
    one, zero = jnp.ones((m, pad), F32), jnp.zeros((m, pad), F32)
    zh = jnp.zeros((m, half), F32)
    reps = LANES // period
    c = jnp.tile(jnp.concatenate([cos, cos, one], axis=1), (1, reps))
    s_lo = jnp.tile(jnp.concatenate([-sin, zh, zero], axis=1), (1, reps))
    s_hi = jnp.tile(jnp.concatenate([zh, sin, zero], axis=1), (1, reps))
    return c, s_lo, s_hi


def _tile(n, pref):
    t = min(n, pref)
    assert n % t == 0, (n, t)
    return t


def kernel(x, p, positions, g_mix, w_in, lambda_q1, lambda_k1, lambda_q2, lambda_k2, g_subln, g_q_latent, w_q_up, g_kv_latent, w_kv_up, w_branch_diff, w_branch_mla, w_out, g_ffn, w_gate_up, w_down, w_ple_in, g_ple, w_ple_gate, g_final):
    b, s, d = x.shape
    depth = w_in.shape[0]
    m = b * s
    d_ff = w_down.shape[1]
    tm = _tile(m, 1024)
    tm_small = _tile(m, 512)
    tq = _tile(s, 512)
    tn = 512

    tabs_p = _rope_tables(positions, PARTIAL_ROPE_DIM, DIFF_HEAD_DIM)
    tabs_m = _rope_tables(positions, MLA_ROPE_DIM, LANES)

    xf = x.reshape(m, d)
    lat0 = 3 * DIFF_WIDTH
    lat1 = lat0 + MLA_Q_RANK + MLA_KV_RANK + MLA_ROPE_DIM
    lat_pad = MLA_Q_RANK + MLA_KV_RANK + LANES
    qk_dim = MLA_NOPE_DIM + MLA_ROPE_DIM
    for i in range(depth):
        lam_init = 0.8 - 0.6 * math.exp(-0.3 * i)
        wi = w_in[i]
        w_qkv = jnp.concatenate([wi[:, :DIFF_WIDTH] * (DIFF_HEAD_DIM ** -0.5), wi[:, DIFF_WIDTH:lat0]], axis=1).astype(BF16)
        w_lat = jnp.pad(wi[:, lat0:lat1], ((0, 0), (0, lat_pad - (lat1 - lat0)))).astype(BF16)
        w_ga = wi[:, lat1:lat1 + d].astype(BF16)
        w_gb = wi[:, lat1 + d:].astype(BF16)
        w_q = jnp.pad(w_q_up[i].reshape(MLA_Q_RANK, MLA_HEADS, qk_dim),
                      ((0, 0), (0, 0), (0, MLA_QK_PAD - qk_dim))).reshape(MLA_Q_RANK, MLA_HEADS * MLA_QK_PAD).astype(BF16)
        w_kv = w_kv_up[i].reshape(MLA_KV_RANK, MLA_HEADS, MLA_NOPE_DIM + MLA_V_DIM)
        w_k = w_kv[:, :, :MLA_NOPE_DIM].reshape(MLA_KV_RANK, MLA_HEADS * MLA_NOPE_DIM).astype(BF16)
        w_v = w_kv[:, :, MLA_NOPE_DIM:].reshape(MLA_KV_RANK, MLA_HEADS * MLA_V_DIM).astype(BF16)
        lam_params = jnp.stack([lambda_q1[i], lambda_k1[i], lambda_q2[i], lambda_k2[i]]).astype(F32)

        h = _rmsnorm(xf, g_mix[i], BF16, tm_small)
        qkv = _qkv_proj(h, w_qkv, tabs_p, tm, DIFF_WIDTH)
        od = _diff_attention(qkv.reshape(b, s, 3 * DIFF_WIDTH), lam_params, g_subln[i], lam_init, tq)
        qm, km, vm = _latent_proj(h, w_lat, g_q_latent[i].reshape(1, -1), w_q, g_kv_latent[i].reshape(1, -1),
                                  w_k, w_v, tabs_m, tm_small)
        om = _mla_attention(qm.reshape(b, s, -1), km.reshape(b, s, -1), vm.reshape(b, s, -1), tq)
        merged = _merge(h, od.reshape(m, -1), om.reshape(m, -1), w_ga, w_gb,
                        w_branch_diff[i].astype(BF16), w_branch_mla[i].astype(BF16), tm, tn)
        xf = _resid_matmul(xf, merged, w_out[i].astype(BF16), tm, tn)

        h = _rmsnorm(xf, g_ffn[i], BF16, tm_small)
        wgu = w_gate_up[i].astype(BF16)
        act = _swiglu(h, wgu[:, :d_ff], wgu[:, d_ff:], tm, tn)
        xf = _resid_matmul(xf, act, w_down[i].astype(BF16), tm, tn)

        h = _rmsnorm(xf, g_ple[i], BF16, tm_small)
        xf = _ple(xf, h, p[i].reshape(m, -1), w_ple_gate[i].astype(BF16), w_ple_in[i].astype(BF16), tm, tn)
    return _rmsnorm(xf, g_final, x.dtype, tm_small).reshape(b, s, d)
```

```python
import functools
import math

import jax
import jax.numpy as jnp
from jax import lax
from jax.experimental import pallas as pl
from jax.experimental.pallas import tpu as pltpu

F32 = jnp.float32
BF16 = jnp.bfloat16

DIFF_HEADS = 8
DIFF_HEAD_DIM = 64
DIFF_WIDTH = DIFF_HEADS * 2 * DIFF_HEAD_DIM
PARTIAL_ROPE_DIM = DIFF_HEAD_DIM // 4
MLA_HEADS = 8
MLA_Q_RANK = 512
MLA_KV_RANK = 256
MLA_NOPE_DIM = 128
MLA_ROPE_DIM = 64
MLA_V_DIM = 128
MLA_QK_PAD = 256
MLA_HEADS_PER_STEP = 2
ROPE_THETA = 500000.0
NORM_EPS = 1e-6
SUBLN_EPS = 1e-5
LOG2E = math.log2(math.e)
LANES = 128
VMEM_LIMIT_BYTES = 56 * 1024 * 1024
KEY_CHUNK = 512


def _cparams(*sem):
    return pltpu.CompilerParams(dimension_semantics=sem, vmem_limit_bytes=VMEM_LIMIT_BYTES)


def _dot(a, b):
    return jnp.dot(a, b, preferred_element_type=F32)


def _sigmoid(x):
    return 1.0 / (1.0 + jnp.exp(-x))


def _rms(x, g, eps):
    return x * lax.rsqrt(jnp.mean(x * x, axis=-1, keepdims=True) + eps) * g


def _rope_chunk(z, c, s_lo, s_hi, half):
    return z * c + pltpu.roll(z, LANES - half, 1) * s_lo + pltpu.roll(z, half, 1) * s_hi


def _rmsnorm_kernel(x_ref, g_ref, o_ref):
    o_ref[...] = _rms(x_ref[...], g_ref[...], NORM_EPS).astype(o_ref.dtype)


def _rmsnorm(x, g, out_dtype, tm):
    m, d = x.shape
    return pl.pallas_call(
        _rmsnorm_kernel,
        grid=(m // tm,),
        in_specs=[pl.BlockSpec((tm, d), lambda i: (i, 0)), pl.BlockSpec((1, d), lambda i: (0, 0))],
        out_specs=pl.BlockSpec((tm, d), lambda i: (i, 0)),
        out_shape=jax.ShapeDtypeStruct((m, d), out_dtype),
        compiler_params=_cparams("parallel"),
        name="rmsnorm",
    )(x, g.reshape(1, d))


def _qkv_kernel(h_ref, w_ref, c_ref, slo_ref, shi_ref, o_ref, *, tn, q_scale):
    j = pl.program_id(1)
    z = _dot(h_ref[...], w_ref[...])

    @pl.when(j < 2)
    def _():
        zs = z * jnp.where(j == 0, q_scale, 1.0)
        c, s_lo, s_hi = c_ref[...], slo_ref[...], shi_ref[...]
        for t in range(tn // LANES):
            sl = slice(t * LANES, (t + 1) * LANES)
            o_ref[:, sl] = _rope_chunk(zs[:, sl], c, s_lo, s_hi, PARTIAL_ROPE_DIM // 2).astype(o_ref.dtype)

    @pl.when(j >= 2)
    def _():
        o_ref[...] = z.astype(o_ref.dtype)


def _qkv_proj(h, w, tabs, tm):
    m, k = h.shape
    tn = DIFF_WIDTH
    tab_spec = pl.BlockSpec((tm, LANES), lambda i, j: (i, 0))
    return pl.pallas_call(
        functools.partial(_qkv_kernel, tn=tn, q_scale=DIFF_HEAD_DIM ** -0.5 * LOG2E),
        grid=(m // tm, 3),
        in_specs=[pl.BlockSpec((tm, k), lambda i, j: (i, 0)), pl.BlockSpec((k, tn), lambda i, j: (0, j)),
                  tab_spec, tab_spec, tab_spec],
        out_specs=pl.BlockSpec((tm, tn), lambda i, j: (i, j)),
        out_shape=jax.ShapeDtypeStruct((m, 3 * tn), BF16),
        compiler_params=_cparams("parallel", "arbitrary"),
        name="qkv_proj",
    )(h, w, *tabs)


def _latent_kernel(h_ref, wl_ref, gq_ref, wq_ref, gkv_ref, wk_ref, wv_ref, c_ref, slo_ref, shi_ref,
                   q_ref, k_ref, v_ref, *, scale):
    z = _dot(h_ref[...], wl_ref[...])
    cq = _rms(z[:, :MLA_Q_RANK], gq_ref[...], NORM_EPS).astype(BF16)
    ckv = _rms(z[:, MLA_Q_RANK:MLA_Q_RANK + MLA_KV_RANK], gkv_ref[...], NORM_EPS).astype(BF16)
    kr_off = MLA_Q_RANK + MLA_KV_RANK
    c, s_lo, s_hi = c_ref[...], slo_ref[...], shi_ref[...]
    half = MLA_ROPE_DIM // 2
    kr = _rope_chunk(z[:, kr_off:kr_off + LANES], c, s_lo, s_hi, half).astype(BF16)
    qf = _dot(cq, wq_ref[...]) * scale
    kn = _dot(ckv, wk_ref[...])
    for h in range(MLA_HEADS):
        o = h * MLA_QK_PAD
        q_ref[:, o:o + LANES] = qf[:, o:o + LANES].astype(BF16)
        q_ref[:, o + LANES:o + 2 * LANES] = _rope_chunk(qf[:, o + LANES:o + 2 * LANES], c, s_lo, s_hi, half).astype(BF16)
        k_ref[:, o:o + LANES] = kn[:, h * LANES:(h + 1) * LANES].astype(BF16)
        k_ref[:, o + LANES:o + 2 * LANES] = kr
    v_ref[...] = _dot(ckv, wv_ref[...]).astype(BF16)


def _latent_proj(h, w_main, lat_tile, gq, wq, gkv, wk, wv, tabs, tm):
    m, k = h.shape
    full = lambda a: pl.BlockSpec(a.shape, lambda i: (0, 0))
    tab_spec = pl.BlockSpec((tm, LANES), lambda i: (i, 0))
    nq, nv = MLA_HEADS * MLA_QK_PAD, MLA_HEADS * MLA_V_DIM
    lat_w = MLA_Q_RANK + MLA_KV_RANK + 2 * LANES
    row = lambda n: pl.BlockSpec((tm, n), lambda i: (i, 0))
    return pl.pallas_call(
        functools.partial(_latent_kernel, scale=(MLA_NOPE_DIM + MLA_ROPE_DIM) ** -0.5 * LOG2E),
        grid=(m // tm,),
        in_specs=[row(k), pl.BlockSpec((k, lat_w), lambda i: (0, lat_tile)), full(gq), full(wq), full(gkv),
                  full(wk), full(wv), tab_spec, tab_spec, tab_spec],
        out_specs=[row(nq), row(nq), row(nv)],
        out_shape=[jax.ShapeDtypeStruct((m, nq), BF16), jax.ShapeDtypeStruct((m, nq), BF16),
                   jax.ShapeDtypeStruct((m, nv), BF16)],
        compiler_params=_cparams("parallel"),
        name="latent_proj",
    )(h, w_main, gq, wq, gkv, wk, wv, *tabs)


def _attend_pipelined(a_chains, b_chains, s_ref, m_ref, n_keys):
    n = len(a_chains)
    nt = (((1,), (1,)), ((), ()))
    m_prev = [m_ref[u] for u in range(n)]
    acc, l, m_new = [None] * n, [None] * n, [None] * n
    for c in range(n_keys // KEY_CHUNK):
        rows = slice(c * KEY_CHUNK, (c + 1) * KEY_CHUNK)
        for u in range(n):
            p = jnp.exp2(s_ref[u, rows, :] - m_prev[u])
            lc = jnp.sum(p, axis=0, keepdims=True)
            pv = _dot(b_chains[u](rows), p.astype(BF16))
            acc[u] = pv if c == 0 else acc[u] + pv
            l[u] = lc if c == 0 else l[u] + lc
        for u, (load_k, qe) in enumerate(a_chains):
            sc = lax.dot_general(load_k(rows), qe, nt, preferred_element_type=F32)
            s_ref[u, rows, :] = sc
            mc = jnp.max(sc, axis=0, keepdims=True)
            m_new[u] = mc if c == 0 else jnp.maximum(m_new[u], mc)
    for u in range(n):
        m_ref[u] = m_new[u]
    return [acc[u] / l[u] for u in range(n)]


def _init_attn_scratch(t, tiles_per_head, v_ref, vt_ref, s_ref, m_ref):
    @pl.when(t == 0)
    def _():
        s_ref[...] = jnp.zeros_like(s_ref)
        m_ref[...] = jnp.zeros_like(m_ref)

    @pl.when(jnp.maximum(t - 1, 0) % tiles_per_head == 0)
    def _():
        vt_ref[...] = v_ref[...].astype(F32).T.astype(BF16)


def _flat_tile(u, n_heads, tiles_per_head):
    return u // (n_heads * tiles_per_head), (u // tiles_per_head) % n_heads, u % tiles_per_head


def _diff_attn_kernel(lam_ref, g_ref, q_ref, k_ref, v_ref, o_ref, vt_ref, s_ref, m_ref, *, lam_init, tiles_per_head):
    _init_attn_scratch(pl.program_id(0), tiles_per_head, v_ref, vt_ref, s_ref, m_ref)
    lp = lam_ref[...]
    lam = (jnp.exp(jnp.sum(lp[0:1] * lp[1:2], axis=-1, keepdims=True))
           - jnp.exp(jnp.sum(lp[2:3] * lp[3:4], axis=-1, keepdims=True)) + lam_init)
    q = q_ref[...]
    lane = lax.broadcasted_iota(jnp.int32, q.shape, 1)
    zero = jnp.zeros_like(q)
    load_k = lambda rows: k_ref[rows, :]
    load_vt = lambda cols: vt_ref[:, cols]
    o1, o2 = _attend_pipelined([(load_k, jnp.where(lane < DIFF_HEAD_DIM, q, zero)),
                                (load_k, jnp.where(lane >= DIFF_HEAD_DIM, q, zero))],
                               [load_vt, load_vt], s_ref, m_ref, k_ref.shape[0])
    o = (o1 - lam * o2).T
    o_ref[...] = (_rms(o, g_ref[...], SUBLN_EPS) * (1.0 - lam_init)).astype(o_ref.dtype)


def _diff_attention(qkv, lam_params, g_subln, lam_init, tq):
    b, s, _ = qkv.shape
    dv = 2 * DIFF_HEAD_DIM
    nh, nq = DIFF_HEADS, s // tq
    n_tiles = b * nh * nq
    tile_a = lambda t: _flat_tile(jnp.minimum(t, n_tiles - 1), nh, nq)
    tile_b = lambda t: _flat_tile(jnp.maximum(t - 1, 0), nh, nq)

    def q_map(t):
        bi, h, qi = tile_a(t)
        return bi, qi, h

    def k_map(t):
        bi, h, _ = tile_a(t)
        return bi, 0, nh + h

    def v_map(t):
        bi, h, _ = tile_b(t)
        return bi, 0, 2 * nh + h

    def o_map(t):
        bi, h, qi = tile_b(t)
        return bi, qi, h

    return pl.pallas_call(
        functools.partial(_diff_attn_kernel, lam_init=lam_init, tiles_per_head=nq),
        grid=(n_tiles + 1,),
        in_specs=[pl.BlockSpec(lam_params.shape, lambda t: (0, 0)),
                  pl.BlockSpec((1, dv), lambda t: (0, 0)),
                  pl.BlockSpec((None, tq, dv), q_map),
                  pl.BlockSpec((None, s, dv), k_map),
                  pl.BlockSpec((None, s, dv), v_map)],
        out_specs=pl.BlockSpec((None, tq, dv), o_map),
        out_shape=jax.ShapeDtypeStruct((b, s, nh * dv), BF16),
        scratch_shapes=[pltpu.VMEM((dv, s), BF16), pltpu.VMEM((2, s, tq), F32), pltpu.VMEM((2, 1, tq), F32)],
        compiler_params=_cparams("arbitrary"),
        name="diff_attention",
    )(lam_params, g_subln.reshape(1, dv), qkv, qkv, qkv)


def _mla_attn_kernel(q_ref, k_ref, v_ref, o_ref, vt_ref, s_ref, m_ref, *, tiles_per_head):
    _init_attn_scratch(pl.program_id(0), tiles_per_head, v_ref, vt_ref, s_ref, m_ref)
    a_chains, b_chains = [], []
    for h in range(MLA_HEADS_PER_STEP):
        kc = slice(h * MLA_QK_PAD, (h + 1) * MLA_QK_PAD)
        vr = slice(h * MLA_V_DIM, (h + 1) * MLA_V_DIM)
        a_chains.append((lambda rows, kc=kc: k_ref[rows, kc], q_ref[:, kc]))
        b_chains.append(lambda cols, vr=vr: vt_ref[vr, cols])
    outs = _attend_pipelined(a_chains, b_chains, s_ref, m_ref, k_ref.shape[0])
    for h, o in enumerate(outs):
        o_ref[:, h * MLA_V_DIM:(h + 1) * MLA_V_DIM] = o.T.astype(o_ref.dtype)


def _mla_attention(q, k, v, tq):
    b, s, _ = q.shape
    hp = MLA_HEADS_PER_STEP
    nh, dq, dv = MLA_HEADS // hp, hp * MLA_QK_PAD, hp * MLA_V_DIM
    nq = s // tq
    n_tiles = b * nh * nq
    tile_a = lambda t: _flat_tile(jnp.minimum(t, n_tiles - 1), nh, nq)
    tile_b = lambda t: _flat_tile(jnp.maximum(t - 1, 0), nh, nq)

    def q_map(t):
        bi, h, qi = tile_a(t)
        return bi, qi, h

    def k_map(t):
        bi, h, _ = tile_a(t)
        return bi, 0, h

    def v_map(t):
        bi, h, _ = tile_b(t)
        return bi, 0, h

    def o_map(t):
        bi, h, qi = tile_b(t)
        return bi, qi, h

    return pl.pallas_call(
        functools.partial(_mla_attn_kernel, tiles_per_head=nq),
        grid=(n_tiles + 1,),
        in_specs=[pl.BlockSpec((None, tq, dq), q_map), pl.BlockSpec((None, s, dq), k_map),
                  pl.BlockSpec((None, s, dv), v_map)],
        out_specs=pl.BlockSpec((None, tq, dv), o_map),
        out_shape=jax.ShapeDtypeStruct((b, s, nh * dv), BF16),
        scratch_shapes=[pltpu.VMEM((dv, s), BF16), pltpu.VMEM((hp, s, tq), F32), pltpu.VMEM((hp, 1, tq), F32)],
        compiler_params=_cparams("arbitrary"),
        name="mla_attention",
    )(q, k, v)


def _merge_kernel(h_ref, od_ref, om_ref, wga_ref, wgb_ref, wbd_ref, wbm_ref, o_ref):
    h = h_ref[...]
    ga = _sigmoid(_dot(h, wga_ref[...]))
    gb = _sigmoid(_dot(h, wgb_ref[...]))
    o_ref[...] = (ga * _dot(od_ref[...], wbd_ref[...]) + gb * _dot(om_ref[...], wbm_ref[...])).astype(o_ref.dtype)


def _merge(h, od, om, w_gates, wbd, wbm, tm, tn):
    m, d = h.shape
    n = wbd.shape[1]
    row = lambda a: pl.BlockSpec((tm, a.shape[1]), lambda i, j: (i, 0))
    col = lambda a, off: pl.BlockSpec((a.shape[0], tn), lambda i, j: (0, j + off))
    return pl.pallas_call(
        _merge_kernel,
        grid=(m // tm, n // tn),
        in_specs=[row(h), row(od), row(om), col(w_gates, 0), col(w_gates, n // tn), col(wbd, 0), col(wbm, 0)],
        out_specs=pl.BlockSpec((tm, tn), lambda i, j: (i, j)),
        out_shape=jax.ShapeDtypeStruct((m, n), BF16),
        compiler_params=_cparams("parallel", "arbitrary"),
        name="gated_merge",
    )(h, od, om, w_gates, w_gates, wbd, wbm)


def _resid_mm_kernel(x_ref, a_ref, w_ref, o_ref):
    o_ref[...] = x_ref[...] + _dot(a_ref[...], w_ref[...])


def _resid_matmul(x, a, w, tm, tn):
    m, n = x.shape
    k = a.shape[1]
    return pl.pallas_call(
        _resid_mm_kernel,
        grid=(m // tm, n // tn),
        in_specs=[pl.BlockSpec((tm, tn), lambda i, j: (i, j)), pl.BlockSpec((tm, k), lambda i, j: (i, 0)),
                  pl.BlockSpec((k, tn), lambda i, j: (0, j))],
        out_specs=pl.BlockSpec((tm, tn), lambda i, j: (i, j)),
        out_shape=jax.ShapeDtypeStruct((m, n), F32),
        compiler_params=_cparams("parallel", "arbitrary"),
        name="resid_matmul",
    )(x, a, w)


def _swiglu_kernel(h_ref, wg_ref, wu_ref, o_ref):
    h = h_ref[...]
    g = _dot(h, wg_ref[...])
    o_ref[...] = (g * _sigmoid(g) * _dot(h, wu_ref[...])).astype(o_ref.dtype)


def _swiglu(h, w_gate_up, tm, tn):
    m, d = h.shape
    n = w_gate_up.shape[1] // 2
    return pl.pallas_call(
        _swiglu_kernel,
        grid=(m // tm, n // tn),
        in_specs=[pl.BlockSpec((tm, d), lambda i, j: (i, 0)), pl.BlockSpec((d, tn), lambda i, j: (0, j)),
                  pl.BlockSpec((d, tn), lambda i, j: (0, j + n // tn))],
        out_specs=pl.BlockSpec((tm, tn), lambda i, j: (i, j)),
        out_shape=jax.ShapeDtypeStruct((m, n), BF16),
        compiler_params=_cparams("parallel", "arbitrary"),
        name="swiglu",
    )(h, w_gate_up, w_gate_up)


def _ple_kernel(x_ref, h_ref, p_ref, wg_ref, wp_ref, o_ref):
    gate = _sigmoid(_dot(h_ref[...], wg_ref[...]))
    o_ref[...] = x_ref[...] + gate * _dot(p_ref[...].astype(BF16), wp_ref[...])


def _ple(x, h, p, wg, wp, tm, tn):
    m, n = x.shape
    d, e = h.shape[1], p.shape[1]
    return pl.pallas_call(
        _ple_kernel,
        grid=(m // tm, n // tn),
        in_specs=[pl.BlockSpec((tm, tn), lambda i, j: (i, j)), pl.BlockSpec((tm, d), lambda i, j: (i, 0)),
                  pl.BlockSpec((tm, e), lambda i, j: (i, 0)), pl.BlockSpec((d, tn), lambda i, j: (0, j)),
                  pl.BlockSpec((e, tn), lambda i, j: (0, j))],
        out_specs=pl.BlockSpec((tm, tn), lambda i, j: (i, j)),
        out_shape=jax.ShapeDtypeStruct((m, n), F32),
        compiler_params=_cparams("parallel", "arbitrary"),
        name="ple_gate",
    )(x, h, p, wg, wp)


def _rope_tables(positions, dim, period, pass_through):
    inv_freq = ROPE_THETA ** (-jnp.arange(0, dim, 2, dtype=F32) / dim)
    ang = positions.astype(F32).reshape(-1, 1) * inv_freq
    cos, sin = jnp.cos(ang), jnp.sin(ang)
    m, half = ang.shape
    pad = period - dim
    rest, zero = jnp.full((m, pad), pass_through, F32), jnp.zeros((m, pad), F32)
    zh = jnp.zeros((m, half), F32)
    reps = LANES // period
    c = jnp.tile(jnp.concatenate([cos, cos, rest], axis=1), (1, reps))
    s_lo = jnp.tile(jnp.concatenate([-sin, zh, zero], axis=1), (1, reps))
    s_hi = jnp.tile(jnp.concatenate([zh, sin, zero], axis=1), (1, reps))
    return c, s_lo, s_hi


def _tile(n, pref):
    t = min(n, pref)
    assert n % t == 0, (n, t)
    return t


def kernel(x, p, positions, g_mix, w_in, lambda_q1, lambda_k1, lambda_q2, lambda_k2, g_subln, g_q_latent, w_q_up, g_kv_latent, w_kv_up, w_branch_diff, w_branch_mla, w_out, g_ffn, w_gate_up, w_down, w_ple_in, g_ple, w_ple_gate, g_final):
    b, s, d = x.shape
    depth = w_in.shape[0]
    m = b * s
    tm = _tile(m, 1024)
    tm_small = _tile(m, 512)
    tq = _tile(s, 512)
    tn = 512
    assert s % KEY_CHUNK == 0

    tabs_p = _rope_tables(positions, PARTIAL_ROPE_DIM, DIFF_HEAD_DIM, 1.0)
    tabs_m = _rope_tables(positions, MLA_ROPE_DIM, LANES, 0.0)

    xf = x.reshape(m, d)
    lat0 = 3 * DIFF_WIDTH
    lat_w = MLA_Q_RANK + MLA_KV_RANK + 2 * LANES
    gates0 = lat0 + MLA_Q_RANK + MLA_KV_RANK + MLA_ROPE_DIM
    assert lat0 % lat_w == 0 and gates0 - lat0 <= lat_w
    qk_dim = MLA_NOPE_DIM + MLA_ROPE_DIM
    for i in range(depth):
        lam_init = 0.8 - 0.6 * math.exp(-0.3 * i)
        w_main = w_in[i, :, :lat0 + lat_w].astype(BF16)
        w_gates = w_in[i, :, gates0:].astype(BF16)
        w_q = jnp.pad(w_q_up[i].reshape(MLA_Q_RANK, MLA_HEADS, qk_dim),
                      ((0, 0), (0, 0), (0, MLA_QK_PAD - qk_dim))).reshape(MLA_Q_RANK, MLA_HEADS * MLA_QK_PAD).astype(BF16)
        w_kv = w_kv_up[i].reshape(MLA_KV_RANK, MLA_HEADS, MLA_NOPE_DIM + MLA_V_DIM)
        w_k = w_kv[:, :, :MLA_NOPE_DIM].reshape(MLA_KV_RANK, MLA_HEADS * MLA_NOPE_DIM).astype(BF16)
        w_v = w_kv[:, :, MLA_NOPE_DIM:].reshape(MLA_KV_RANK, MLA_HEADS * MLA_V_DIM).astype(BF16)
        lam_params = jnp.stack([lambda_q1[i], lambda_k1[i], lambda_q2[i], lambda_k2[i]]).astype(F32)

        h = _rmsnorm(xf, g_mix[i], BF16, tm_small)
        qkv = _qkv_proj(h, w_main, tabs_p, tm)
        od = _diff_attention(qkv.reshape(b, s, 3 * DIFF_WIDTH), lam_params, g_subln[i], lam_init, tq)
        qm, km, vm = _latent_proj(h, w_main, lat0 // lat_w, g_q_latent[i].reshape(1, -1), w_q,
                                  g_kv_latent[i].reshape(1, -1), w_k, w_v, tabs_m, tm_small)
        om = _mla_attention(qm.reshape(b, s, -1), km.reshape(b, s, -1), vm.reshape(b, s, -1), tq)
        merged = _merge(h, od.reshape(m, -1), om.reshape(m, -1), w_gates,
                        w_branch_diff[i].astype(BF16), w_branch_mla[i].astype(BF16), tm, tn)
        xf = _resid_matmul(xf, merged, w_out[i].astype(BF16), tm, tn)

        h = _rmsnorm(xf, g_ffn[i], BF16, tm_small)
        act = _swiglu(h, w_gate_up[i].astype(BF16), tm, tn)
        xf = _resid_matmul(xf, act, w_down[i].astype(BF16), tm, tn)

        h = _rmsnorm(xf, g_ple[i], BF16, tm_small)
        xf = _ple(xf, h, p[i].reshape(m, -1), w_ple_gate[i].astype(BF16), w_ple_in[i].astype(BF16), tm, tn)
    return _rmsnorm(xf, g_final, x.dtype, tm_small).reshape(b, s, d)
```

```python
import functools
import math

import jax
import jax.numpy as jnp
from jax import lax
from jax.experimental import pallas as pl
from jax.experimental.pallas import tpu as pltpu

F32 = jnp.float32
BF16 = jnp.bfloat16

DIFF_HEADS = 8
DIFF_HEAD_DIM = 64
DIFF_WIDTH = DIFF_HEADS * 2 * DIFF_HEAD_DIM
PARTIAL_ROPE_DIM = DIFF_HEAD_DIM // 4
MLA_HEADS = 8
MLA_Q_RANK = 512
MLA_KV_RANK = 256
MLA_NOPE_DIM = 128
MLA_ROPE_DIM = 64
MLA_V_DIM = 128
MLA_QK_PAD = 256
MLA_HEADS_PER_STEP = 2
ROPE_THETA = 500000.0
NORM_EPS = 1e-6
SUBLN_EPS = 1e-5
LOG2E = math.log2(math.e)
LANES = 128
VMEM_LIMIT_BYTES = 56 * 1024 * 1024
KEY_CHUNK = 512


def _cparams(*sem):
    return pltpu.CompilerParams(dimension_semantics=sem, vmem_limit_bytes=VMEM_LIMIT_BYTES)


def _dot(a, b):
    return jnp.dot(a, b, preferred_element_type=F32)


def _sigmoid(x):
    return 1.0 / (1.0 + jnp.exp(-x))


def _rms(x, g, eps):
    return x * lax.rsqrt(jnp.mean(x * x, axis=-1, keepdims=True) + eps) * g


def _rope_chunk(z, c, s_lo, s_hi, half):
    return z * c + pltpu.roll(z, LANES - half, 1) * s_lo + pltpu.roll(z, half, 1) * s_hi


def _rmsnorm_kernel(x_ref, g_ref, o_ref):
    o_ref[...] = _rms(x_ref[...], g_ref[...], NORM_EPS).astype(o_ref.dtype)


def _rmsnorm(x, g, out_dtype, tm):
    m, d = x.shape
    return pl.pallas_call(
        _rmsnorm_kernel,
        grid=(m // tm,),
        in_specs=[pl.BlockSpec((tm, d), lambda i: (i, 0)), pl.BlockSpec((1, d), lambda i: (0, 0))],
        out_specs=pl.BlockSpec((tm, d), lambda i: (i, 0)),
        out_shape=jax.ShapeDtypeStruct((m, d), out_dtype),
        compiler_params=_cparams("parallel"),
        name="rmsnorm",
    )(x, g.reshape(1, d))


def _qkv_kernel(h_ref, w_ref, c_ref, slo_ref, shi_ref, o_ref, *, tn, q_scale):
    j = pl.program_id(1)
    z = _dot(h_ref[...], w_ref[...]) * jnp.where(j == 0, q_scale, 1.0)
    rotary = j < 2
    c = jnp.where(rotary, c_ref[...], 1.0)
    s_lo = jnp.where(rotary, slo_ref[...], 0.0)
    s_hi = jnp.where(rotary, shi_ref[...], 0.0)
    for t in range(tn // LANES):
        sl = slice(t * LANES, (t + 1) * LANES)
        o_ref[:, sl] = _rope_chunk(z[:, sl], c, s_lo, s_hi, PARTIAL_ROPE_DIM // 2).astype(o_ref.dtype)


def _qkv_proj(h, w, tabs, tm):
    m, k = h.shape
    tn = DIFF_WIDTH
    tab_spec = pl.BlockSpec((tm, LANES), lambda i, j: (i, 0))
    return pl.pallas_call(
        functools.partial(_qkv_kernel, tn=tn, q_scale=DIFF_HEAD_DIM ** -0.5 * LOG2E),
        grid=(m // tm, 3),
        in_specs=[pl.BlockSpec((tm, k), lambda i, j: (i, 0)), pl.BlockSpec((k, tn), lambda i, j: (0, j)),
                  tab_spec, tab_spec, tab_spec],
        out_specs=pl.BlockSpec((tm, tn), lambda i, j: (i, j)),
        out_shape=jax.ShapeDtypeStruct((m, 3 * tn), BF16),
        compiler_params=_cparams("parallel", "arbitrary"),
        name="qkv_proj",
    )(h, w, *tabs)


def _latent_kernel(h_ref, wl_ref, gq_ref, wq_ref, gkv_ref, wk_ref, wv_ref, c_ref, slo_ref, shi_ref,
                   q_ref, k_ref, v_ref, *, scale):
    z = _dot(h_ref[...], wl_ref[...])
    cq = _rms(z[:, :MLA_Q_RANK], gq_ref[...], NORM_EPS).astype(BF16)
    ckv = _rms(z[:, MLA_Q_RANK:MLA_Q_RANK + MLA_KV_RANK], gkv_ref[...], NORM_EPS).astype(BF16)
    kr_off = MLA_Q_RANK + MLA_KV_RANK
    c, s_lo, s_hi = c_ref[...], slo_ref[...], shi_ref[...]
    half = MLA_ROPE_DIM // 2
    kr = _rope_chunk(z[:, kr_off:kr_off + LANES], c, s_lo, s_hi, half).astype(BF16)
    qf = _dot(cq, wq_ref[...]) * scale
    kn = _dot(ckv, wk_ref[...])
    for h in range(MLA_HEADS):
        o = h * MLA_QK_PAD
        q_ref[:, o:o + LANES] = qf[:, o:o + LANES].astype(BF16)
        q_ref[:, o + LANES:o + 2 * LANES] = _rope_chunk(qf[:, o + LANES:o + 2 * LANES], c, s_lo, s_hi, half).astype(BF16)
        k_ref[:, o:o + LANES] = kn[:, h * LANES:(h + 1) * LANES].astype(BF16)
        k_ref[:, o + LANES:o + 2 * LANES] = kr
    v_ref[...] = _dot(ckv, wv_ref[...]).astype(BF16)


def _latent_proj(h, w_main, lat_tile, gq, wq, gkv, wk, wv, tabs, tm):
    m, k = h.shape
    full = lambda a: pl.BlockSpec(a.shape, lambda i: (0, 0))
    tab_spec = pl.BlockSpec((tm, LANES), lambda i: (i, 0))
    nq, nv = MLA_HEADS * MLA_QK_PAD, MLA_HEADS * MLA_V_DIM
    lat_w = MLA_Q_RANK + MLA_KV_RANK + 2 * LANES
    row = lambda n: pl.BlockSpec((tm, n), lambda i: (i, 0))
    return pl.pallas_call(
        functools.partial(_latent_kernel, scale=(MLA_NOPE_DIM + MLA_ROPE_DIM) ** -0.5 * LOG2E),
        grid=(m // tm,),
        in_specs=[row(k), pl.BlockSpec((k, lat_w), lambda i: (0, lat_tile)), full(gq), full(wq), full(gkv),
                  full(wk), full(wv), tab_spec, tab_spec, tab_spec],
        out_specs=[row(nq), row(nq), row(nv)],
        out_shape=[jax.ShapeDtypeStruct((m, nq), BF16), jax.ShapeDtypeStruct((m, nq), BF16),
                   jax.ShapeDtypeStruct((m, nv), BF16)],
        compiler_params=_cparams("parallel"),
        name="latent_proj",
    )(h, w_main, gq, wq, gkv, wk, wv, *tabs)


def _attend_pipelined(a_chains, b_chains, s_ref, m_ref, n_keys):
    n = len(a_chains)
    nt = (((1,), (1,)), ((), ()))
    m_prev = [m_ref[u] for u in range(n)]
    acc, l, m_new = [None] * n, [None] * n, [None] * n
    for c in range(n_keys // KEY_CHUNK):
        rows = slice(c * KEY_CHUNK, (c + 1) * KEY_CHUNK)
        for u in range(n):
            p = jnp.exp2(s_ref[u, rows, :] - m_prev[u])
            lc = jnp.sum(p, axis=0, keepdims=True)
            pv = _dot(b_chains[u](rows), p.astype(BF16))
            acc[u] = pv if c == 0 else acc[u] + pv
            l[u] = lc if c == 0 else l[u] + lc
        for u, (load_k, qe) in enumerate(a_chains):
            sc = lax.dot_general(load_k(rows), qe, nt, preferred_element_type=F32)
            s_ref[u, rows, :] = sc
            mc = jnp.max(sc, axis=0, keepdims=True)
            m_new[u] = mc if c == 0 else jnp.maximum(m_new[u], mc)
    for u in range(n):
        m_ref[u] = m_new[u]
    return [acc[u] / l[u] for u in range(n)]


def _init_attn_scratch(t, tiles_per_head, v_ref, vt_ref, s_ref, m_ref):
    @pl.when(t == 0)
    def _():
        s_ref[...] = jnp.zeros_like(s_ref)
        m_ref[...] = jnp.zeros_like(m_ref)

    @pl.when(jnp.maximum(t - 1, 0) % tiles_per_head == 0)
    def _():
        vt_ref[...] = v_ref[...].astype(F32).T.astype(BF16)


def _flat_tile(u, n_heads, tiles_per_head):
    return u // (n_heads * tiles_per_head), (u // tiles_per_head) % n_heads, u % tiles_per_head


def _diff_attn_kernel(lam_ref, g_ref, q_ref, k_ref, v_ref, o_ref, vt_ref, s_ref, m_ref, *, lam_init, tiles_per_head):
    _init_attn_scratch(pl.program_id(0), tiles_per_head, v_ref, vt_ref, s_ref, m_ref)
    lp = lam_ref[...]
    lam = (jnp.exp(jnp.sum(lp[0:1] * lp[1:2], axis=-1, keepdims=True))
           - jnp.exp(jnp.sum(lp[2:3] * lp[3:4], axis=-1, keepdims=True)) + lam_init)
    q = q_ref[...]
    lane = lax.broadcasted_iota(jnp.int32, q.shape, 1)
    zero = jnp.zeros_like(q)
    load_k = lambda rows: k_ref[rows, :]
    load_vt = lambda cols: vt_ref[:, cols]
    o1, o2 = _attend_pipelined([(load_k, jnp.where(lane < DIFF_HEAD_DIM, q, zero)),
                                (load_k, jnp.where(lane >= DIFF_HEAD_DIM, q, zero))],
                               [load_vt, load_vt], s_ref, m_ref, k_ref.shape[0])
    o = (o1 - lam * o2).T
    o_ref[...] = (_rms(o, g_ref[...], SUBLN_EPS) * (1.0 - lam_init)).astype(o_ref.dtype)


def _diff_attention(qkv, lam_params, g_subln, lam_init, tq):
    b, s, _ = qkv.shape
    dv = 2 * DIFF_HEAD_DIM
    nh, nq = DIFF_HEADS, s // tq
    n_tiles = b * nh * nq
    tile_a = lambda t: _flat_tile(jnp.minimum(t, n_tiles - 1), nh, nq)
    tile_b = lambda t: _flat_tile(jnp.maximum(t - 1, 0), nh, nq)

    def q_map(t):
        bi, h, qi = tile_a(t)
        return bi, qi, h

    def k_map(t):
        bi, h, _ = tile_a(t)
        return bi, 0, nh + h

    def v_map(t):
        bi, h, _ = tile_b(t)
        return bi, 0, 2 * nh + h

    def o_map(t):
        bi, h, qi = tile_b(t)
        return bi, qi, h

    return pl.pallas_call(
        functools.partial(_diff_attn_kernel, lam_init=lam_init, tiles_per_head=nq),
        grid=(n_tiles + 1,),
        in_specs=[pl.BlockSpec(lam_params.shape, lambda t: (0, 0)),
                  pl.BlockSpec((1, dv), lambda t: (0, 0)),
                  pl.BlockSpec((None, tq, dv), q_map),
                  pl.BlockSpec((None, s, dv), k_map),
                  pl.BlockSpec((None, s, dv), v_map)],
        out_specs=pl.BlockSpec((None, tq, dv), o_map),
        out_shape=jax.ShapeDtypeStruct((b, s, nh * dv), BF16),
        scratch_shapes=[pltpu.VMEM((dv, s), BF16), pltpu.VMEM((2, s, tq), F32), pltpu.VMEM((2, 1, tq), F32)],
        compiler_params=_cparams("arbitrary"),
        name="diff_attention",
    )(lam_params, g_subln.reshape(1, dv), qkv, qkv, qkv)


def _mla_attn_kernel(q_ref, k_ref, v_ref, o_ref, vt_ref, s_ref, m_ref, *, tiles_per_head):
    _init_attn_scratch(pl.program_id(0), tiles_per_head, v_ref, vt_ref, s_ref, m_ref)
    a_chains, b_chains = [], []
    for h in range(MLA_HEADS_PER_STEP):
        kc = slice(h * MLA_QK_PAD, (h + 1) * MLA_QK_PAD)
        vr = slice(h * MLA_V_DIM, (h + 1) * MLA_V_DIM)
        a_chains.append((lambda rows, kc=kc: k_ref[rows, kc], q_ref[:, kc]))
        b_chains.append(lambda cols, vr=vr: vt_ref[vr, cols])
    outs = _attend_pipelined(a_chains, b_chains, s_ref, m_ref, k_ref.shape[0])
    for h, o in enumerate(outs):
        o_ref[:, h * MLA_V_DIM:(h + 1) * MLA_V_DIM] = o.T.astype(o_ref.dtype)


def _mla_attention(q, k, v, tq):
    b, s, _ = q.shape
    hp = MLA_HEADS_PER_STEP
    nh, dq, dv = MLA_HEADS // hp, hp * MLA_QK_PAD, hp * MLA_V_DIM
    nq = s // tq
    n_tiles = b * nh * nq
    tile_a = lambda t: _flat_tile(jnp.minimum(t, n_tiles - 1), nh, nq)
    tile_b = lambda t: _flat_tile(jnp.maximum(t - 1, 0), nh, nq)

    def q_map(t):
        bi, h, qi = tile_a(t)
        return bi, qi, h

    def k_map(t):
        bi, h, _ = tile_a(t)
        return bi, 0, h

    def v_map(t):
        bi, h, _ = tile_b(t)
        return bi, 0, h

    def o_map(t):
        bi, h, qi = tile_b(t)
        return bi, qi, h

    return pl.pallas_call(
        functools.partial(_mla_attn_kernel, tiles_per_head=nq),
        grid=(n_tiles + 1,),
        in_specs=[pl.BlockSpec((None, tq, dq), q_map), pl.BlockSpec((None, s, dq), k_map),
                  pl.BlockSpec((None, s, dv), v_map)],
        out_specs=pl.BlockSpec((None, tq, dv), o_map),
        out_shape=jax.ShapeDtypeStruct((b, s, nh * dv), BF16),
        scratch_shapes=[pltpu.VMEM((dv, s), BF16), pltpu.VMEM((hp, s, tq), F32), pltpu.VMEM((hp, 1, tq), F32)],
        compiler_params=_cparams("arbitrary"),
        name="mla_attention",
    )(q, k, v)


def _begin_rows(x_ref, xo_ref):
    @pl.when(pl.program_id(1) == 0)
    def _():
        xo_ref[...] = x_ref[...]


def _finish_rows(xo_ref, g_ref, ho_ref):
    @pl.when(pl.program_id(1) == pl.num_programs(1) - 1)
    def _():
        ho_ref[...] = _rms(xo_ref[...], g_ref[...], NORM_EPS).astype(ho_ref.dtype)


def _mix_out_kernel(x_ref, h_ref, od_ref, om_ref, wga_ref, wgb_ref, wbd_ref, wbm_ref, wo_ref, g_ref, xo_ref, ho_ref):
    _begin_rows(x_ref, xo_ref)
    h = h_ref[...]
    ga = _sigmoid(_dot(h, wga_ref[...]))
    gb = _sigmoid(_dot(h, wgb_ref[...]))
    merged = (ga * _dot(od_ref[...], wbd_ref[...]) + gb * _dot(om_ref[...], wbm_ref[...])).astype(BF16)
    xo_ref[...] += _dot(merged, wo_ref[...])
    _finish_rows(xo_ref, g_ref, ho_ref)


def _mix_out(x, h, od, om, w_gates, wbd, wbm, wo, g_next, tm, tn):
    m, d = x.shape
    n = wbd.shape[1]
    row = lambda a: pl.BlockSpec((tm, a.shape[1]), lambda i, j: (i, 0))
    col = lambda a, off: pl.BlockSpec((a.shape[0], tn), lambda i, j: (0, j + off))
    return pl.pallas_call(
        _mix_out_kernel,
        grid=(m // tm, n // tn),
        in_specs=[row(x), row(h), row(od), row(om), col(w_gates, 0), col(w_gates, n // tn), col(wbd, 0), col(wbm, 0),
                  pl.BlockSpec((tn, d), lambda i, j: (j, 0)), pl.BlockSpec((1, d), lambda i, j: (0, 0))],
        out_specs=[row(x), row(x)],
        out_shape=[jax.ShapeDtypeStruct((m, d), F32), jax.ShapeDtypeStruct((m, d), BF16)],
        compiler_params=_cparams("parallel", "arbitrary"),
        name="mix_out",
    )(x, h, od, om, w_gates, w_gates, wbd, wbm, wo, g_next.reshape(1, d))


def _ffn_kernel(x_ref, h_ref, wg_ref, wu_ref, wd_ref, g_ref, xo_ref, ho_ref):
    _begin_rows(x_ref, xo_ref)
    h = h_ref[...]
    gate = _dot(h, wg_ref[...])
    act = (gate * _sigmoid(gate) * _dot(h, wu_ref[...])).astype(BF16)
    xo_ref[...] += _dot(act, wd_ref[...])
    _finish_rows(xo_ref, g_ref, ho_ref)


def _ffn(x, h, w_gate_up, w_down, g_next, tm, tn):
    m, d = x.shape
    n = w_down.shape[0]
    row = lambda a: pl.BlockSpec((tm, a.shape[1]), lambda i, j: (i, 0))
    return pl.pallas_call(
        _ffn_kernel,
        grid=(m // tm, n // tn),
        in_specs=[row(x), row(h), pl.BlockSpec((d, tn), lambda i, j: (0, j)),
                  pl.BlockSpec((d, tn), lambda i, j: (0, j + n // tn)),
                  pl.BlockSpec((tn, d), lambda i, j: (j, 0)), pl.BlockSpec((1, d), lambda i, j: (0, 0))],
        out_specs=[row(x), row(x)],
        out_shape=[jax.ShapeDtypeStruct((m, d), F32), jax.ShapeDtypeStruct((m, d), BF16)],
        compiler_params=_cparams("parallel", "arbitrary"),
        name="ffn",
    )(x, h, w_gate_up, w_gate_up, w_down, g_next.reshape(1, d))


def _ple_kernel(x_ref, h_ref, p_ref, wg_ref, wp_ref, g_ref, *out_refs, last_layer):
    gate = _sigmoid(_dot(h_ref[...], wg_ref[...]))
    x_new = x_ref[...] + gate * _dot(p_ref[...].astype(BF16), wp_ref[...])
    normed = _rms(x_new, g_ref[...], NORM_EPS)
    if last_layer:
        out_refs[0][...] = normed
    else:
        out_refs[0][...] = x_new
        out_refs[1][...] = normed.astype(BF16)


def _ple(x, h, p, wg, wp, g_next, last_layer, tm):
    m, d = x.shape
    row = lambda a: pl.BlockSpec((tm, a.shape[1]), lambda i: (i, 0))
    full = lambda a: pl.BlockSpec(a.shape, lambda i: (0, 0))
    g2 = g_next.reshape(1, d)
    x_out = jax.ShapeDtypeStruct((m, d), F32)
    return pl.pallas_call(
        functools.partial(_ple_kernel, last_layer=last_layer),
        grid=(m // tm,),
        in_specs=[row(x), row(h), row(p), full(wg), full(wp), full(g2)],
        out_specs=[row(x)] if last_layer else [row(x), row(x)],
        out_shape=[x_out] if last_layer else [x_out, jax.ShapeDtypeStruct((m, d), BF16)],
        compiler_params=_cparams("parallel"),
        name="ple_gate",
    )(x, h, p, wg, wp, g2)


def _rope_tables(positions, dim, period, pass_through):
    inv_freq = ROPE_THETA ** (-jnp.arange(0, dim, 2, dtype=F32) / dim)
    ang = positions.astype(F32).reshape(-1, 1) * inv_freq
    cos, sin = jnp.cos(ang), jnp.sin(ang)
    m, half = ang.shape
    pad = period - dim
    rest, zero = jnp.full((m, pad), pass_through, F32), jnp.zeros((m, pad), F32)
    zh = jnp.zeros((m, half), F32)
    reps = LANES // period
    c = jnp.tile(jnp.concatenate([cos, cos, rest], axis=1), (1, reps))
    s_lo = jnp.tile(jnp.concatenate([-sin, zh, zero], axis=1), (1, reps))
    s_hi = jnp.tile(jnp.concatenate([zh, sin, zero], axis=1), (1, reps))
    return c, s_lo, s_hi


def _tile(n, pref):
    t = min(n, pref)
    assert n % t == 0, (n, t)
    return t


def kernel(x, p, positions, g_mix, w_in, lambda_q1, lambda_k1, lambda_q2, lambda_k2, g_subln, g_q_latent, w_q_up, g_kv_latent, w_kv_up, w_branch_diff, w_branch_mla, w_out, g_ffn, w_gate_up, w_down, w_ple_in, g_ple, w_ple_gate, g_final):
    b, s, d = x.shape
    depth = w_in.shape[0]
    m = b * s
    tm = _tile(m, 1024)
    tm_small = _tile(m, 512)
    tq = _tile(s, 512)
    tn = 512
    assert s % KEY_CHUNK == 0

    tabs_p = _rope_tables(positions, PARTIAL_ROPE_DIM, DIFF_HEAD_DIM, 1.0)
    tabs_m = _rope_tables(positions, MLA_ROPE_DIM, LANES, 0.0)

    xf = x.reshape(m, d)
    h = _rmsnorm(xf, g_mix[0], BF16, tm_small)
    lat0 = 3 * DIFF_WIDTH
    lat_w = MLA_Q_RANK + MLA_KV_RANK + 2 * LANES
    gates0 = lat0 + MLA_Q_RANK + MLA_KV_RANK + MLA_ROPE_DIM
    assert lat0 % lat_w == 0 and gates0 - lat0 <= lat_w
    qk_dim = MLA_NOPE_DIM + MLA_ROPE_DIM
    for i in range(depth):
        lam_init = 0.8 - 0.6 * math.exp(-0.3 * i)
        w_main = w_in[i, :, :lat0 + lat_w].astype(BF16)
        w_gates = w_in[i, :, gates0:].astype(BF16)
        w_q = jnp.pad(w_q_up[i].reshape(MLA_Q_RANK, MLA_HEADS, qk_dim),
                      ((0, 0), (0, 0), (0, MLA_QK_PAD - qk_dim))).reshape(MLA_Q_RANK, MLA_HEADS * MLA_QK_PAD).astype(BF16)
        w_kv = w_kv_up[i].reshape(MLA_KV_RANK, MLA_HEADS, MLA_NOPE_DIM + MLA_V_DIM)
        w_k = w_kv[:, :, :MLA_NOPE_DIM].reshape(MLA_KV_RANK, MLA_HEADS * MLA_NOPE_DIM).astype(BF16)
        w_v = w_kv[:, :, MLA_NOPE_DIM:].reshape(MLA_KV_RANK, MLA_HEADS * MLA_V_DIM).astype(BF16)
        lam_params = jnp.stack([lambda_q1[i], lambda_k1[i], lambda_q2[i], lambda_k2[i]]).astype(F32)

        qkv = _qkv_proj(h, w_main, tabs_p, tm)
        od = _diff_attention(qkv.reshape(b, s, 3 * DIFF_WIDTH), lam_params, g_subln[i], lam_init, tq)
        qm, km, vm = _latent_proj(h, w_main, lat0 // lat_w, g_q_latent[i].reshape(1, -1), w_q,
                                  g_kv_latent[i].reshape(1, -1), w_k, w_v, tabs_m, tm_small)
        om = _mla_attention(qm.reshape(b, s, -1), km.reshape(b, s, -1), vm.reshape(b, s, -1), tq)
        xf, h = _mix_out(xf, h, od.reshape(m, -1), om.reshape(m, -1), w_gates, w_branch_diff[i].astype(BF16),
                         w_branch_mla[i].astype(BF16), w_out[i].astype(BF16), g_ffn[i], tm_small, tn)

        xf, h = _ffn(xf, h, w_gate_up[i].astype(BF16), w_down[i].astype(BF16), g_ple[i], tm_small, tn)

        last = i == depth - 1
        outs = _ple(xf, h, p[i].reshape(m, -1), w_ple_gate[i].astype(BF16), w_ple_in[i].astype(BF16),
                    g_final if last else g_mix[i + 1], last, tm_small)
        if last:
            return outs[0].reshape(b, s, d)
        xf, h = outs
```

```python
import functools
import math

import jax
import jax.numpy as jnp
from jax import lax
from jax.experimental import pallas as pl
from jax.experimental.pallas import tpu as pltpu

F32 = jnp.float32
BF16 = jnp.bfloat16

DIFF_HEADS = 8
DIFF_HEAD_DIM = 64
DIFF_WIDTH = DIFF_HEADS * 2 * DIFF_HEAD_DIM
PARTIAL_ROPE_DIM = DIFF_HEAD_DIM // 4
MLA_HEADS = 8
MLA_Q_RANK = 512
MLA_KV_RANK = 256
MLA_NOPE_DIM = 128
MLA_ROPE_DIM = 64
MLA_V_DIM = 128
MLA_QK_PAD = 256
MLA_HEADS_PER_STEP = 2
ROPE_THETA = 500000.0
NORM_EPS = 1e-6
SUBLN_EPS = 1e-5
LOG2E = math.log2(math.e)
LANES = 128
VMEM_LIMIT_BYTES = 56 * 1024 * 1024
KEY_CHUNK = 512


def _cparams(*sem):
    return pltpu.CompilerParams(dimension_semantics=sem, vmem_limit_bytes=VMEM_LIMIT_BYTES)


def _dot(a, b):
    return jnp.dot(a, b, preferred_element_type=F32)


def _sigmoid(x):
    return 1.0 / (1.0 + jnp.exp(-x))


def _rms(x, g, eps):
    return x * lax.rsqrt(jnp.mean(x * x, axis=-1, keepdims=True) + eps) * g


def _rope_chunk(z, c, s_lo, s_hi, half):
    return z * c + pltpu.roll(z, LANES - half, 1) * s_lo + pltpu.roll(z, half, 1) * s_hi


def _rmsnorm_kernel(x_ref, g_ref, o_ref):
    o_ref[...] = _rms(x_ref[...], g_ref[...], NORM_EPS).astype(o_ref.dtype)


def _rmsnorm(x, g, out_dtype, tm):
    m, d = x.shape
    return pl.pallas_call(
        _rmsnorm_kernel,
        grid=(m // tm,),
        in_specs=[pl.BlockSpec((tm, d), lambda i: (i, 0)), pl.BlockSpec((1, d), lambda i: (0, 0))],
        out_specs=pl.BlockSpec((tm, d), lambda i: (i, 0)),
        out_shape=jax.ShapeDtypeStruct((m, d), out_dtype),
        compiler_params=_cparams("parallel"),
        name="rmsnorm",
    )(x, g.reshape(1, d))


def _qkv_kernel(h_ref, w_ref, c_ref, slo_ref, shi_ref, o_ref, *, tn, q_scale):
    j = pl.program_id(1)
    z = _dot(h_ref[...], w_ref[...]) * jnp.where(j == 0, q_scale, 1.0)
    rotary = j < 2
    c = jnp.where(rotary, c_ref[...], 1.0)
    s_lo = jnp.where(rotary, slo_ref[...], 0.0)
    s_hi = jnp.where(rotary, shi_ref[...], 0.0)
    for t in range(tn // LANES):
        sl = slice(t * LANES, (t + 1) * LANES)
        o_ref[:, sl] = _rope_chunk(z[:, sl], c, s_lo, s_hi, PARTIAL_ROPE_DIM // 2).astype(o_ref.dtype)


def _qkv_proj(h, w, tabs, tm):
    m, k = h.shape
    tn = DIFF_WIDTH
    tab_spec = pl.BlockSpec((tm, LANES), lambda i, j: (i, 0))
    return pl.pallas_call(
        functools.partial(_qkv_kernel, tn=tn, q_scale=DIFF_HEAD_DIM ** -0.5 * LOG2E),
        grid=(m // tm, 3),
        in_specs=[pl.BlockSpec((tm, k), lambda i, j: (i, 0)), pl.BlockSpec((k, tn), lambda i, j: (0, j)),
                  tab_spec, tab_spec, tab_spec],
        out_specs=pl.BlockSpec((tm, tn), lambda i, j: (i, j)),
        out_shape=jax.ShapeDtypeStruct((m, 3 * tn), BF16),
        compiler_params=_cparams("parallel", "arbitrary"),
        name="qkv_proj",
    )(h, w, *tabs)


def _latent_kernel(h_ref, wl_ref, gq_ref, wq_ref, gkv_ref, wk_ref, wv_ref, c_ref, slo_ref, shi_ref,
                   q_ref, k_ref, v_ref, *, scale):
    z = _dot(h_ref[...], wl_ref[...])
    cq = _rms(z[:, :MLA_Q_RANK], gq_ref[...], NORM_EPS).astype(BF16)
    ckv = _rms(z[:, MLA_Q_RANK:MLA_Q_RANK + MLA_KV_RANK], gkv_ref[...], NORM_EPS).astype(BF16)
    kr_off = MLA_Q_RANK + MLA_KV_RANK
    c, s_lo, s_hi = c_ref[...], slo_ref[...], shi_ref[...]
    half = MLA_ROPE_DIM // 2
    kr = _rope_chunk(z[:, kr_off:kr_off + LANES], c, s_lo, s_hi, half).astype(BF16)
    qf = _dot(cq, wq_ref[...]) * scale
    kn = _dot(ckv, wk_ref[...])
    for h in range(MLA_HEADS):
        o = h * MLA_QK_PAD
        q_ref[:, o:o + LANES] = qf[:, o:o + LANES].astype(BF16)
        q_ref[:, o + LANES:o + 2 * LANES] = _rope_chunk(qf[:, o + LANES:o + 2 * LANES], c, s_lo, s_hi, half).astype(BF16)
        k_ref[:, o:o + LANES] = kn[:, h * LANES:(h + 1) * LANES].astype(BF16)
        k_ref[:, o + LANES:o + 2 * LANES] = kr
    v_ref[...] = _dot(ckv, wv_ref[...]).astype(BF16)


def _latent_proj(h, w_main, lat_tile, gq, wq, gkv, wk, wv, tabs, tm):
    m, k = h.shape
    full = lambda a: pl.BlockSpec(a.shape, lambda i: (0, 0))
    tab_spec = pl.BlockSpec((tm, LANES), lambda i: (i, 0))
    nq, nv = MLA_HEADS * MLA_QK_PAD, MLA_HEADS * MLA_V_DIM
    lat_w = MLA_Q_RANK + MLA_KV_RANK + 2 * LANES
    row = lambda n: pl.BlockSpec((tm, n), lambda i: (i, 0))
    return pl.pallas_call(
        functools.partial(_latent_kernel, scale=(MLA_NOPE_DIM + MLA_ROPE_DIM) ** -0.5 * LOG2E),
        grid=(m // tm,),
        in_specs=[row(k), pl.BlockSpec((k, lat_w), lambda i: (0, lat_tile)), full(gq), full(wq), full(gkv),
                  full(wk), full(wv), tab_spec, tab_spec, tab_spec],
        out_specs=[row(nq), row(nq), row(nv)],
        out_shape=[jax.ShapeDtypeStruct((m, nq), BF16), jax.ShapeDtypeStruct((m, nq), BF16),
                   jax.ShapeDtypeStruct((m, nv), BF16)],
        compiler_params=_cparams("parallel"),
        name="latent_proj",
    )(h, w_main, gq, wq, gkv, wk, wv, *tabs)


def _attend_pipelined(a_chains, b_chains, s_ref, m_ref, n_keys):
    n = len(a_chains)
    tq = m_ref.shape[-1]
    nt = (((1,), (1,)), ((), ()))
    m_prev = [m_ref[u] for u in range(n)]
    acc, l, m_new = [None] * n, [None] * n, [None] * n
    for c in range(n_keys // KEY_CHUNK):
        rows = slice(c * KEY_CHUNK, (c + 1) * KEY_CHUNK)
        for u in range(n):
            p = jnp.exp2(s_ref[u, rows, :tq] - m_prev[u])
            lc = jnp.sum(p, axis=0, keepdims=True)
            pv = _dot(b_chains[u](rows), p.astype(BF16))
            acc[u] = pv if c == 0 else acc[u] + pv
            l[u] = lc if c == 0 else l[u] + lc
        for u, (load_k, qe) in enumerate(a_chains):
            sc = lax.dot_general(load_k(rows), qe, nt, preferred_element_type=F32)
            s_ref[u, rows, :tq] = sc
            mc = jnp.max(sc, axis=0, keepdims=True)
            m_new[u] = mc if c == 0 else jnp.maximum(m_new[u], mc)
    for u in range(n):
        m_ref[u] = m_new[u]
    return [acc[u] / l[u] for u in range(n)]


def _logit_scratch(n_chains, n_keys, tq):
    return pltpu.VMEM((n_chains, n_keys, tq + LANES), F32)


def _init_attn_scratch(t, tiles_per_head, v_ref, vt_ref, s_ref, m_ref):
    @pl.when(t == 0)
    def _():
        s_ref[...] = jnp.zeros_like(s_ref)
        m_ref[...] = jnp.zeros_like(m_ref)

    @pl.when(jnp.maximum(t - 1, 0) % tiles_per_head == 0)
    def _():
        vt_ref[...] = v_ref[...].astype(F32).T.astype(BF16)


def _flat_tile(u, n_heads, tiles_per_head):
    return u // (n_heads * tiles_per_head), (u // tiles_per_head) % n_heads, u % tiles_per_head


def _diff_attn_kernel(lam_ref, g_ref, q_ref, k_ref, v_ref, o_ref, vt_ref, s_ref, m_ref, *, lam_init, tiles_per_head):
    _init_attn_scratch(pl.program_id(0), tiles_per_head, v_ref, vt_ref, s_ref, m_ref)
    lp = lam_ref[...]
    lam = (jnp.exp(jnp.sum(lp[0:1] * lp[1:2], axis=-1, keepdims=True))
           - jnp.exp(jnp.sum(lp[2:3] * lp[3:4], axis=-1, keepdims=True)) + lam_init)
    q = q_ref[...]
    lane = lax.broadcasted_iota(jnp.int32, q.shape, 1)
    zero = jnp.zeros_like(q)
    load_k = lambda rows: k_ref[rows, :]
    load_vt = lambda cols: vt_ref[:, cols]
    o1, o2 = _attend_pipelined([(load_k, jnp.where(lane < DIFF_HEAD_DIM, q, zero)),
                                (load_k, jnp.where(lane >= DIFF_HEAD_DIM, q, zero))],
                               [load_vt, load_vt], s_ref, m_ref, k_ref.shape[0])
    o = (o1 - lam * o2).T
    o_ref[...] = (_rms(o, g_ref[...], SUBLN_EPS) * (1.0 - lam_init)).astype(o_ref.dtype)


def _diff_attention(qkv, lam_params, g_subln, lam_init, tq):
    b, s, _ = qkv.shape
    dv = 2 * DIFF_HEAD_DIM
    nh, nq = DIFF_HEADS, s // tq
    n_tiles = b * nh * nq
    tile_a = lambda t: _flat_tile(jnp.minimum(t, n_tiles - 1), nh, nq)
    tile_b = lambda t: _flat_tile(jnp.maximum(t - 1, 0), nh, nq)

    def q_map(t):
        bi, h, qi = tile_a(t)
        return bi, qi, h

    def k_map(t):
        bi, h, _ = tile_a(t)
        return bi, 0, nh + h

    def v_map(t):
        bi, h, _ = tile_b(t)
        return bi, 0, 2 * nh + h

    def o_map(t):
        bi, h, qi = tile_b(t)
        return bi, qi, h

    return pl.pallas_call(
        functools.partial(_diff_attn_kernel, lam_init=lam_init, tiles_per_head=nq),
        grid=(n_tiles + 1,),
        in_specs=[pl.BlockSpec(lam_params.shape, lambda t: (0, 0)),
                  pl.BlockSpec((1, dv), lambda t: (0, 0)),
                  pl.BlockSpec((None, tq, dv), q_map),
                  pl.BlockSpec((None, s, dv), k_map),
                  pl.BlockSpec((None, s, dv), v_map)],
        out_specs=pl.BlockSpec((None, tq, dv), o_map),
        out_shape=jax.ShapeDtypeStruct((b, s, nh * dv), BF16),
        scratch_shapes=[pltpu.VMEM((dv, s), BF16), _logit_scratch(2, s, tq), pltpu.VMEM((2, 1, tq), F32)],
        compiler_params=_cparams("arbitrary"),
        name="diff_attention",
    )(lam_params, g_subln.reshape(1, dv), qkv, qkv, qkv)


def _mla_attn_kernel(q_ref, k_ref, v_ref, o_ref, vt_ref, s_ref, m_ref, *, tiles_per_head):
    _init_attn_scratch(pl.program_id(0), tiles_per_head, v_ref, vt_ref, s_ref, m_ref)
    a_chains, b_chains = [], []
    for h in range(MLA_HEADS_PER_STEP):
        kc = slice(h * MLA_QK_PAD, (h + 1) * MLA_QK_PAD)
        vr = slice(h * MLA_V_DIM, (h + 1) * MLA_V_DIM)
        a_chains.append((lambda rows, kc=kc: k_ref[rows, kc], q_ref[:, kc]))
        b_chains.append(lambda cols, vr=vr: vt_ref[vr, cols])
    outs = _attend_pipelined(a_chains, b_chains, s_ref, m_ref, k_ref.shape[0])
    for h, o in enumerate(outs):
        o_ref[:, h * MLA_V_DIM:(h + 1) * MLA_V_DIM] = o.T.astype(o_ref.dtype)


def _mla_attention(q, k, v, tq):
    b, s, _ = q.shape
    hp = MLA_HEADS_PER_STEP
    nh, dq, dv = MLA_HEADS // hp, hp * MLA_QK_PAD, hp * MLA_V_DIM
    nq = s // tq
    n_tiles = b * nh * nq
    tile_a = lambda t: _flat_tile(jnp.minimum(t, n_tiles - 1), nh, nq)
    tile_b = lambda t: _flat_tile(jnp.maximum(t - 1, 0), nh, nq)

    def q_map(t):
        bi, h, qi = tile_a(t)
        return bi, qi, h

    def k_map(t):
        bi, h, _ = tile_a(t)
        return bi, 0, h

    def v_map(t):
        bi, h, _ = tile_b(t)
        return bi, 0, h

    def o_map(t):
        bi, h, qi = tile_b(t)
        return bi, qi, h

    return pl.pallas_call(
        functools.partial(_mla_attn_kernel, tiles_per_head=nq),
        grid=(n_tiles + 1,),
        in_specs=[pl.BlockSpec((None, tq, dq), q_map), pl.BlockSpec((None, s, dq), k_map),
                  pl.BlockSpec((None, s, dv), v_map)],
        out_specs=pl.BlockSpec((None, tq, dv), o_map),
        out_shape=jax.ShapeDtypeStruct((b, s, nh * dv), BF16),
        scratch_shapes=[pltpu.VMEM((dv, s), BF16), _logit_scratch(hp, s, tq), pltpu.VMEM((hp, 1, tq), F32)],
        compiler_params=_cparams("arbitrary"),
        name="mla_attention",
    )(q, k, v)


def _begin_rows(x_ref, xo_ref):
    @pl.when(pl.program_id(1) == 0)
    def _():
        xo_ref[...] = x_ref[...]


def _finish_rows(xo_ref, g_ref, ho_ref):
    @pl.when(pl.program_id(1) == pl.num_programs(1) - 1)
    def _():
        ho_ref[...] = _rms(xo_ref[...], g_ref[...], NORM_EPS).astype(ho_ref.dtype)


def _mix_out_kernel(x_ref, h_ref, od_ref, om_ref, wga_ref, wgb_ref, wbd_ref, wbm_ref, wo_ref, g_ref, xo_ref, ho_ref):
    _begin_rows(x_ref, xo_ref)
    h = h_ref[...]
    ga = _sigmoid(_dot(h, wga_ref[...]))
    gb = _sigmoid(_dot(h, wgb_ref[...]))
    merged = (ga * _dot(od_ref[...], wbd_ref[...]) + gb * _dot(om_ref[...], wbm_ref[...])).astype(BF16)
    xo_ref[...] += _dot(merged, wo_ref[...])
    _finish_rows(xo_ref, g_ref, ho_ref)


def _mix_out(x, h, od, om, w_gates, wbd, wbm, wo, g_next, tm, tn):
    m, d = x.shape
    n = wbd.shape[1]
    row = lambda a: pl.BlockSpec((tm, a.shape[1]), lambda i, j: (i, 0))
    col = lambda a, off: pl.BlockSpec((a.shape[0], tn), lambda i, j: (0, j + off))
    return pl.pallas_call(
        _mix_out_kernel,
        grid=(m // tm, n // tn),
        in_specs=[row(x), row(h), row(od), row(om), col(w_gates, 0), col(w_gates, n // tn), col(wbd, 0), col(wbm, 0),
                  pl.BlockSpec((tn, d), lambda i, j: (j, 0)), pl.BlockSpec((1, d), lambda i, j: (0, 0))],
        out_specs=[row(x), row(x)],
        out_shape=[jax.ShapeDtypeStruct((m, d), F32), jax.ShapeDtypeStruct((m, d), BF16)],
        compiler_params=_cparams("parallel", "arbitrary"),
        name="mix_out",
    )(x, h, od, om, w_gates, w_gates, wbd, wbm, wo, g_next.reshape(1, d))


def _ffn_kernel(x_ref, h_ref, wg_ref, wu_ref, wd_ref, g_ref, xo_ref, ho_ref):
    _begin_rows(x_ref, xo_ref)
    h = h_ref[...]
    gate = _dot(h, wg_ref[...])
    act = (gate * _sigmoid(gate) * _dot(h, wu_ref[...])).astype(BF16)
    xo_ref[...] += _dot(act, wd_ref[...])
    _finish_rows(xo_ref, g_ref, ho_ref)


def _ffn(x, h, w_gate_up, w_down, g_next, tm, tn):
    m, d = x.shape
    n = w_down.shape[0]
    row = lambda a: pl.BlockSpec((tm, a.shape[1]), lambda i, j: (i, 0))
    return pl.pallas_call(
        _ffn_kernel,
        grid=(m // tm, n // tn),
        in_specs=[row(x), row(h), pl.BlockSpec((d, tn), lambda i, j: (0, j)),
                  pl.BlockSpec((d, tn), lambda i, j: (0, j + n // tn)),
                  pl.BlockSpec((tn, d), lambda i, j: (j, 0)), pl.BlockSpec((1, d), lambda i, j: (0, 0))],
        out_specs=[row(x), row(x)],
        out_shape=[jax.ShapeDtypeStruct((m, d), F32), jax.ShapeDtypeStruct((m, d), BF16)],
        compiler_params=_cparams("parallel", "arbitrary"),
        name="ffn",
    )(x, h, w_gate_up, w_gate_up, w_down, g_next.reshape(1, d))


def _ple_kernel(x_ref, h_ref, p_ref, wg_ref, wp_ref, g_ref, *out_refs, last_layer):
    gate = _sigmoid(_dot(h_ref[...], wg_ref[...]))
    x_new = x_ref[...] + gate * _dot(p_ref[...].astype(BF16), wp_ref[...])
    normed = _rms(x_new, g_ref[...], NORM_EPS)
    if last_layer:
        out_refs[0][...] = normed
    else:
        out_refs[0][...] = x_new
        out_refs[1][...] = normed.astype(BF16)


def _ple(x, h, p, wg, wp, g_next, last_layer, tm):
    m, d = x.shape
    row = lambda a: pl.BlockSpec((tm, a.shape[1]), lambda i: (i, 0))
    full = lambda a: pl.BlockSpec(a.shape, lambda i: (0, 0))
    g2 = g_next.reshape(1, d)
    x_out = jax.ShapeDtypeStruct((m, d), F32)
    return pl.pallas_call(
        functools.partial(_ple_kernel, last_layer=last_layer),
        grid=(m // tm,),
        in_specs=[row(x), row(h), row(p), full(wg), full(wp), full(g2)],
        out_specs=[row(x)] if last_layer else [row(x), row(x)],
        out_shape=[x_out] if last_layer else [x_out, jax.ShapeDtypeStruct((m, d), BF16)],
        compiler_params=_cparams("parallel"),
        name="ple_gate",
    )(x, h, p, wg, wp, g2)


def _rope_tables(positions, dim, period, pass_through):
    inv_freq = ROPE_THETA ** (-jnp.arange(0, dim, 2, dtype=F32) / dim)
    ang = positions.astype(F32).reshape(-1, 1) * inv_freq
    cos, sin = jnp.cos(ang), jnp.sin(ang)
    m, half = ang.shape
    pad = period - dim
    rest, zero = jnp.full((m, pad), pass_through, F32), jnp.zeros((m, pad), F32)
    zh = jnp.zeros((m, half), F32)
    reps = LANES // period
    c = jnp.tile(jnp.concatenate([cos, cos, rest], axis=1), (1, reps))
    s_lo = jnp.tile(jnp.concatenate([-sin, zh, zero], axis=1), (1, reps))
    s_hi = jnp.tile(jnp.concatenate([zh, sin, zero], axis=1), (1, reps))
    return c, s_lo, s_hi


def _tile(n, pref):
    t = min(n, pref)
    assert n % t == 0, (n, t)
    return t


def kernel(x, p, positions, g_mix, w_in, lambda_q1, lambda_k1, lambda_q2, lambda_k2, g_subln, g_q_latent, w_q_up, g_kv_latent, w_kv_up, w_branch_diff, w_branch_mla, w_out, g_ffn, w_gate_up, w_down, w_ple_in, g_ple, w_ple_gate, g_final):
    b, s, d = x.shape
    depth = w_in.shape[0]
    m = b * s
    tm = _tile(m, 1024)
    tm_small = _tile(m, 512)
    tq = _tile(s, 512)
    tn = 512
    assert s % KEY_CHUNK == 0

    tabs_p = _rope_tables(positions, PARTIAL_ROPE_DIM, DIFF_HEAD_DIM, 1.0)
    tabs_m = _rope_tables(positions, MLA_ROPE_DIM, LANES, 0.0)

    xf = x.reshape(m, d)
    h = _rmsnorm(xf, g_mix[0], BF16, tm_small)
    lat0 = 3 * DIFF_WIDTH
    lat_w = MLA_Q_RANK + MLA_KV_RANK + 2 * LANES
    gates0 = lat0 + MLA_Q_RANK + MLA_KV_RANK + MLA_ROPE_DIM
    assert lat0 % lat_w == 0 and gates0 - lat0 <= lat_w
    qk_dim = MLA_NOPE_DIM + MLA_ROPE_DIM
    for i in range(depth):
        lam_init = 0.8 - 0.6 * math.exp(-0.3 * i)
        w_main = w_in[i, :, :lat0 + lat_w].astype(BF16)
        w_gates = w_in[i, :, gates0:].astype(BF16)
        w_q = jnp.pad(w_q_up[i].reshape(MLA_Q_RANK, MLA_HEADS, qk_dim),
                      ((0, 0), (0, 0), (0, MLA_QK_PAD - qk_dim))).reshape(MLA_Q_RANK, MLA_HEADS * MLA_QK_PAD).astype(BF16)
        w_kv = w_kv_up[i].reshape(MLA_KV_RANK, MLA_HEADS, MLA_NOPE_DIM + MLA_V_DIM)
        w_k = w_kv[:, :, :MLA_NOPE_DIM].reshape(MLA_KV_RANK, MLA_HEADS * MLA_NOPE_DIM).astype(BF16)
        w_v = w_kv[:, :, MLA_NOPE_DIM:].reshape(MLA_KV_RANK, MLA_HEADS * MLA_V_DIM).astype(BF16)
        lam_params = jnp.stack([lambda_q1[i], lambda_k1[i], lambda_q2[i], lambda_k2[i]]).astype(F32)

        qkv = _qkv_proj(h, w_main, tabs_p, tm)
        od = _diff_attention(qkv.reshape(b, s, 3 * DIFF_WIDTH), lam_params, g_subln[i], lam_init, tq)
        qm, km, vm = _latent_proj(h, w_main, lat0 // lat_w, g_q_latent[i].reshape(1, -1), w_q,
                                  g_kv_latent[i].reshape(1, -1), w_k, w_v, tabs_m, tm_small)
        om = _mla_attention(qm.reshape(b, s, -1), km.reshape(b, s, -1), vm.reshape(b, s, -1), tq)
        xf, h = _mix_out(xf, h, od.reshape(m, -1), om.reshape(m, -1), w_gates, w_branch_diff[i].astype(BF16),
                         w_branch_mla[i].astype(BF16), w_out[i].astype(BF16), g_ffn[i], tm_small, tn)

        xf, h = _ffn(xf, h, w_gate_up[i].astype(BF16), w_down[i].astype(BF16), g_ple[i], tm_small, tn)

        last = i == depth - 1
        outs = _ple(xf, h, p[i].reshape(m, -1), w_ple_gate[i].astype(BF16), w_ple_in[i].astype(BF16),
                    g_final if last else g_mix[i + 1], last, tm_small)
        if last:
            return outs[0].reshape(b, s, d)
        xf, h = outs
```

```python
import functools
import math

import jax
import jax.numpy as jnp
from jax import lax
from jax.experimental import pallas as pl
from jax.experimental.pallas import tpu as pltpu

F32 = jnp.float32
BF16 = jnp.bfloat16

DIFF_HEADS = 8
DIFF_HEAD_DIM = 64
DIFF_WIDTH = DIFF_HEADS * 2 * DIFF_HEAD_DIM
PARTIAL_ROPE_DIM = DIFF_HEAD_DIM // 4
MLA_HEADS = 8
MLA_Q_RANK = 512
MLA_KV_RANK = 256
MLA_NOPE_DIM = 128
MLA_ROPE_DIM = 64
MLA_V_DIM = 128
MLA_QK_PAD = 256
MLA_HEADS_PER_STEP = 2
ROPE_THETA = 500000.0
NORM_EPS = 1e-6
SUBLN_EPS = 1e-5
LOG2E = math.log2(math.e)
LANES = 128
VMEM_LIMIT_BYTES = 56 * 1024 * 1024
KEY_CHUNK = 256


def _cparams(*sem):
    return pltpu.CompilerParams(dimension_semantics=sem, vmem_limit_bytes=VMEM_LIMIT_BYTES)


def _dot(a, b):
    return jnp.dot(a, b, preferred_element_type=F32)


def _sigmoid(x):
    return 1.0 / (1.0 + jnp.exp(-x))


def _rms(x, g, eps):
    return x * lax.rsqrt(jnp.mean(x * x, axis=-1, keepdims=True) + eps) * g


def _rope_chunk(z, c, s_lo, s_hi, half):
    return z * c + pltpu.roll(z, LANES - half, 1) * s_lo + pltpu.roll(z, half, 1) * s_hi


def _rmsnorm_kernel(x_ref, g_ref, o_ref):
    o_ref[...] = _rms(x_ref[...], g_ref[...], NORM_EPS).astype(o_ref.dtype)


def _rmsnorm(x, g, out_dtype, tm):
    m, d = x.shape
    return pl.pallas_call(
        _rmsnorm_kernel,
        grid=(m // tm,),
        in_specs=[pl.BlockSpec((tm, d), lambda i: (i, 0)), pl.BlockSpec((1, d), lambda i: (0, 0))],
        out_specs=pl.BlockSpec((tm, d), lambda i: (i, 0)),
        out_shape=jax.ShapeDtypeStruct((m, d), out_dtype),
        compiler_params=_cparams("parallel"),
        name="rmsnorm",
    )(x, g.reshape(1, d))


def _qkv_kernel(h_ref, w_ref, c_ref, slo_ref, shi_ref, o_ref, *, tn, q_scale):
    j = pl.program_id(1)
    z = _dot(h_ref[...], w_ref[...]) * jnp.where(j == 0, q_scale, 1.0)
    rotary = j < 2
    c = jnp.where(rotary, c_ref[...], 1.0)
    s_lo = jnp.where(rotary, slo_ref[...], 0.0)
    s_hi = jnp.where(rotary, shi_ref[...], 0.0)
    for t in range(tn // LANES):
        sl = slice(t * LANES, (t + 1) * LANES)
        o_ref[:, sl] = _rope_chunk(z[:, sl], c, s_lo, s_hi, PARTIAL_ROPE_DIM // 2).astype(o_ref.dtype)


def _qkv_proj(h, w, layer, tabs, tm):
    m, k = h.shape
    tn = DIFF_WIDTH
    tab_spec = pl.BlockSpec((tm, LANES), lambda i, j: (i, 0))
    return pl.pallas_call(
        functools.partial(_qkv_kernel, tn=tn, q_scale=DIFF_HEAD_DIM ** -0.5 * LOG2E),
        grid=(m // tm, 3),
        in_specs=[pl.BlockSpec((tm, k), lambda i, j: (i, 0)), pl.BlockSpec((None, k, tn), lambda i, j: (layer, 0, j)),
                  tab_spec, tab_spec, tab_spec],
        out_specs=pl.BlockSpec((tm, tn), lambda i, j: (i, j)),
        out_shape=jax.ShapeDtypeStruct((m, 3 * tn), BF16),
        compiler_params=_cparams("parallel", "arbitrary"),
        name="qkv_proj",
    )(h, w, *tabs)


def _latent_kernel(h_ref, wl_ref, gq_ref, wq_ref, gkv_ref, wk_ref, wv_ref, c_ref, slo_ref, shi_ref,
                   q_ref, k_ref, v_ref, *, scale):
    z = _dot(h_ref[...], wl_ref[...])
    cq = _rms(z[:, :MLA_Q_RANK], gq_ref[...], NORM_EPS).astype(BF16)
    ckv = _rms(z[:, MLA_Q_RANK:MLA_Q_RANK + MLA_KV_RANK], gkv_ref[...], NORM_EPS).astype(BF16)
    kr_off = MLA_Q_RANK + MLA_KV_RANK
    c, s_lo, s_hi = c_ref[...], slo_ref[...], shi_ref[...]
    half = MLA_ROPE_DIM // 2
    kr = _rope_chunk(z[:, kr_off:kr_off + LANES], c, s_lo, s_hi, half).astype(BF16)
    qf = _dot(cq, wq_ref[...]) * scale
    kn = _dot(ckv, wk_ref[...])
    for h in range(MLA_HEADS):
        o = h * MLA_QK_PAD
        q_ref[:, o:o + LANES] = qf[:, o:o + LANES].astype(BF16)
        q_ref[:, o + LANES:o + 2 * LANES] = _rope_chunk(qf[:, o + LANES:o + 2 * LANES], c, s_lo, s_hi, half).astype(BF16)
        k_ref[:, o:o + LANES] = kn[:, h * LANES:(h + 1) * LANES].astype(BF16)
        k_ref[:, o + LANES:o + 2 * LANES] = kr
    v_ref[...] = _dot(ckv, wv_ref[...]).astype(BF16)


def _latent_proj(h, w_main, layer, lat_tile, gq, wq, gkv, wk, wv, tabs, tm):
    m, k = h.shape
    full = lambda a: pl.BlockSpec(a.shape, lambda i: (0, 0))
    tab_spec = pl.BlockSpec((tm, LANES), lambda i: (i, 0))
    nq, nv = MLA_HEADS * MLA_QK_PAD, MLA_HEADS * MLA_V_DIM
    lat_w = MLA_Q_RANK + MLA_KV_RANK + 2 * LANES
    row = lambda n: pl.BlockSpec((tm, n), lambda i: (i, 0))
    return pl.pallas_call(
        functools.partial(_latent_kernel, scale=(MLA_NOPE_DIM + MLA_ROPE_DIM) ** -0.5 * LOG2E),
        grid=(m // tm,),
        in_specs=[row(k), pl.BlockSpec((None, k, lat_w), lambda i: (layer, 0, lat_tile)), full(gq), full(wq), full(gkv),
                  full(wk), full(wv), tab_spec, tab_spec, tab_spec],
        out_specs=[row(nq), row(nq), row(nv)],
        out_shape=[jax.ShapeDtypeStruct((m, nq), BF16), jax.ShapeDtypeStruct((m, nq), BF16),
                   jax.ShapeDtypeStruct((m, nv), BF16)],
        compiler_params=_cparams("parallel"),
        name="latent_proj",
    )(h, w_main, gq, wq, gkv, wk, wv, *tabs)


def _attend_pipelined(a_chains, b_chains, s_ref, m_ref, n_keys):
    n = len(a_chains)
    nt = (((1,), (1,)), ((), ()))
    m_prev = [m_ref[u] for u in range(n)]
    acc, l, m_new = [None] * n, [None] * n, [None] * n
    for c in range(n_keys // KEY_CHUNK):
        rows = slice(c * KEY_CHUNK, (c + 1) * KEY_CHUNK)
        for u in range(n):
            p = jnp.exp2(s_ref[u, rows, :] - m_prev[u])
            lc = jnp.sum(p, axis=0, keepdims=True)
            pv = _dot(b_chains[u](rows), p.astype(BF16))
            acc[u] = pv if c == 0 else acc[u] + pv
            l[u] = lc if c == 0 else l[u] + lc
        for u, (load_k, qe) in enumerate(a_chains):
            sc = lax.dot_general(load_k(rows), qe, nt, preferred_element_type=F32)
            s_ref[u, rows, :] = sc
            mc = jnp.max(sc, axis=0, keepdims=True)
            m_new[u] = mc if c == 0 else jnp.maximum(m_new[u], mc)
    for u in range(n):
        m_ref[u] = m_new[u]
    return [acc[u] / l[u] for u in range(n)]


def _attn_scratch(n_chains, n_keys, tq, dv):
    return [pltpu.VMEM((dv, n_keys), BF16), pltpu.VMEM((n_chains, n_keys, tq), F32),
            pltpu.VMEM((n_chains, 1, tq), F32)]


def _init_attn_scratch(t, tiles_per_head, v_ref, vt_ref, s_ref, m_ref):
    @pl.when(t == 0)
    def _():
        s_ref[...] = jnp.zeros_like(s_ref)
        m_ref[...] = jnp.zeros_like(m_ref)

    @pl.when(jnp.maximum(t - 1, 0) % tiles_per_head == 0)
    def _():
        vt_ref[...] = v_ref[...].astype(F32).T.astype(BF16)


def _flat_tile(u, n_heads, tiles_per_head):
    return u // (n_heads * tiles_per_head), (u // tiles_per_head) % n_heads, u % tiles_per_head


def _diff_attn_kernel(lam_ref, g_ref, q_ref, k_ref, v_ref, o_ref, vt_ref, s_ref, m_ref, *, lam_init, tiles_per_head):
    _init_attn_scratch(pl.program_id(0), tiles_per_head, v_ref, vt_ref, s_ref, m_ref)
    lp = lam_ref[...]
    lam = (jnp.exp(jnp.sum(lp[0:1] * lp[1:2], axis=-1, keepdims=True))
           - jnp.exp(jnp.sum(lp[2:3] * lp[3:4], axis=-1, keepdims=True)) + lam_init)
    q = q_ref[...]
    lane = lax.broadcasted_iota(jnp.int32, q.shape, 1)
    zero = jnp.zeros_like(q)
    load_k = lambda rows: k_ref[rows, :]
    load_vt = lambda cols: vt_ref[:, cols]
    o1, o2 = _attend_pipelined([(load_k, jnp.where(lane < DIFF_HEAD_DIM, q, zero)),
                                (load_k, jnp.where(lane >= DIFF_HEAD_DIM, q, zero))],
                               [load_vt, load_vt], s_ref, m_ref, k_ref.shape[0])
    o = (o1 - lam * o2).T
    o_ref[...] = (_rms(o, g_ref[...], SUBLN_EPS) * (1.0 - lam_init)).astype(o_ref.dtype)


def _diff_attention(qkv, lam_params, g_subln, lam_init, tq):
    b, s, _ = qkv.shape
    dv = 2 * DIFF_HEAD_DIM
    nh, nq = DIFF_HEADS, s // tq
    n_tiles = b * nh * nq
    tile_a = lambda t: _flat_tile(jnp.minimum(t, n_tiles - 1), nh, nq)
    tile_b = lambda t: _flat_tile(jnp.maximum(t - 1, 0), nh, nq)

    def q_map(t):
        bi, h, qi = tile_a(t)
        return bi, qi, h

    def k_map(t):
        bi, h, _ = tile_a(t)
        return bi, 0, nh + h

    def v_map(t):
        bi, h, _ = tile_b(t)
        return bi, 0, 2 * nh + h

    def o_map(t):
        bi, h, qi = tile_b(t)
        return bi, qi, h

    return pl.pallas_call(
        functools.partial(_diff_attn_kernel, lam_init=lam_init, tiles_per_head=nq),
        grid=(n_tiles + 1,),
        in_specs=[pl.BlockSpec(lam_params.shape, lambda t: (0, 0)),
                  pl.BlockSpec((1, dv), lambda t: (0, 0)),
                  pl.BlockSpec((None, tq, dv), q_map),
                  pl.BlockSpec((None, s, dv), k_map),
                  pl.BlockSpec((None, s, dv), v_map)],
        out_specs=pl.BlockSpec((None, tq, dv), o_map),
        out_shape=jax.ShapeDtypeStruct((b, s, nh * dv), BF16),
        scratch_shapes=_attn_scratch(2, s, tq, dv),
        compiler_params=_cparams("arbitrary"),
        name="diff_attention",
    )(lam_params, g_subln.reshape(1, dv), qkv, qkv, qkv)


def _mla_attn_kernel(q_ref, k_ref, v_ref, o_ref, vt_ref, s_ref, m_ref, *, tiles_per_head):
    _init_attn_scratch(pl.program_id(0), tiles_per_head, v_ref, vt_ref, s_ref, m_ref)
    a_chains, b_chains = [], []
    for h in range(MLA_HEADS_PER_STEP):
        kc = slice(h * MLA_QK_PAD, (h + 1) * MLA_QK_PAD)
        vr = slice(h * MLA_V_DIM, (h + 1) * MLA_V_DIM)
        a_chains.append((lambda rows, kc=kc: k_ref[rows, kc], q_ref[:, kc]))
        b_chains.append(lambda cols, vr=vr: vt_ref[vr, cols])
    outs = _attend_pipelined(a_chains, b_chains, s_ref, m_ref, k_ref.shape[0])
    for h, o in enumerate(outs):
        o_ref[:, h * MLA_V_DIM:(h + 1) * MLA_V_DIM] = o.T.astype(o_ref.dtype)


def _mla_attention(q, k, v, tq):
    b, s, _ = q.shape
    hp = MLA_HEADS_PER_STEP
    nh, dq, dv = MLA_HEADS // hp, hp * MLA_QK_PAD, hp * MLA_V_DIM
    nq = s // tq
    n_tiles = b * nh * nq
    tile_a = lambda t: _flat_tile(jnp.minimum(t, n_tiles - 1), nh, nq)
    tile_b = lambda t: _flat_tile(jnp.maximum(t - 1, 0), nh, nq)

    def q_map(t):
        bi, h, qi = tile_a(t)
        return bi, qi, h

    def k_map(t):
        bi, h, _ = tile_a(t)
        return bi, 0, h

    def v_map(t):
        bi, h, _ = tile_b(t)
        return bi, 0, h

    def o_map(t):
        bi, h, qi = tile_b(t)
        return bi, qi, h

    return pl.pallas_call(
        functools.partial(_mla_attn_kernel, tiles_per_head=nq),
        grid=(n_tiles + 1,),
        in_specs=[pl.BlockSpec((None, tq, dq), q_map), pl.BlockSpec((None, s, dq), k_map),
                  pl.BlockSpec((None, s, dv), v_map)],
        out_specs=pl.BlockSpec((None, tq, dv), o_map),
        out_shape=jax.ShapeDtypeStruct((b, s, nh * dv), BF16),
        scratch_shapes=_attn_scratch(hp, s, tq, dv),
        compiler_params=_cparams("arbitrary"),
        name="mla_attention",
    )(q, k, v)


def _begin_rows(x_ref, xo_ref):
    @pl.when(pl.program_id(1) == 0)
    def _():
        xo_ref[...] = x_ref[...]


def _finish_rows(xo_ref, g_ref, ho_ref):
    @pl.when(pl.program_id(1) == pl.num_programs(1) - 1)
    def _():
        ho_ref[...] = _rms(xo_ref[...], g_ref[...], NORM_EPS).astype(ho_ref.dtype)


def _mix_out_kernel(x_ref, h_ref, od_ref, om_ref, wga_ref, wgb_ref, wbd_ref, wbm_ref, wo_ref, g_ref, xo_ref, ho_ref):
    _begin_rows(x_ref, xo_ref)
    h = h_ref[...]
    ga = _sigmoid(_dot(h, wga_ref[...]))
    gb = _sigmoid(_dot(h, wgb_ref[...]))
    merged = (ga * _dot(od_ref[...], wbd_ref[...]) + gb * _dot(om_ref[...], wbm_ref[...])).astype(BF16)
    xo_ref[...] += _dot(merged, wo_ref[...])
    _finish_rows(xo_ref, g_ref, ho_ref)


def _mix_out(x, h, od, om, w_gates, wbd, wbm, wo, layer, g_next, tm, tn):
    m, d = x.shape
    n = wbd.shape[2]
    row = lambda a: pl.BlockSpec((tm, a.shape[1]), lambda i, j: (i, 0))
    col = lambda a, off: pl.BlockSpec((None, a.shape[1], tn), lambda i, j: (layer, 0, j + off))
    return pl.pallas_call(
        _mix_out_kernel,
        grid=(m // tm, n // tn),
        in_specs=[row(x), row(h), row(od), row(om), col(w_gates, 0), col(w_gates, n // tn), col(wbd, 0), col(wbm, 0),
                  pl.BlockSpec((None, tn, d), lambda i, j: (layer, j, 0)), pl.BlockSpec((1, d), lambda i, j: (0, 0))],
        out_specs=[row(x), row(x)],
        out_shape=[jax.ShapeDtypeStruct((m, d), F32), jax.ShapeDtypeStruct((m, d), BF16)],
        compiler_params=_cparams("parallel", "arbitrary"),
        name="mix_out",
    )(x, h, od, om, w_gates, w_gates, wbd, wbm, wo, g_next.reshape(1, d))


def _ffn_kernel(x_ref, h_ref, wg_ref, wu_ref, wd_ref, g_ref, xo_ref, ho_ref):
    _begin_rows(x_ref, xo_ref)
    h = h_ref[...]
    gate = _dot(h, wg_ref[...])
    act = (gate * _sigmoid(gate) * _dot(h, wu_ref[...])).astype(BF16)
    xo_ref[...] += _dot(act, wd_ref[...])
    _finish_rows(xo_ref, g_ref, ho_ref)


def _ffn(x, h, w_gate_up, w_down, layer, g_next, tm, tn):
    m, d = x.shape
    n = w_down.shape[1]
    row = lambda a: pl.BlockSpec((tm, a.shape[1]), lambda i, j: (i, 0))
    return pl.pallas_call(
        _ffn_kernel,
        grid=(m // tm, n // tn),
        in_specs=[row(x), row(h), pl.BlockSpec((None, d, tn), lambda i, j: (layer, 0, j)),
                  pl.BlockSpec((None, d, tn), lambda i, j: (layer, 0, j + n // tn)),
                  pl.BlockSpec((None, tn, d), lambda i, j: (layer, j, 0)), pl.BlockSpec((1, d), lambda i, j: (0, 0))],
        out_specs=[row(x), row(x)],
        out_shape=[jax.ShapeDtypeStruct((m, d), F32), jax.ShapeDtypeStruct((m, d), BF16)],
        compiler_params=_cparams("parallel", "arbitrary"),
        name="ffn",
    )(x, h, w_gate_up, w_gate_up, w_down, g_next.reshape(1, d))


def _ple_kernel(x_ref, h_ref, p_ref, wg_ref, wp_ref, g_ref, *out_refs, last_layer):
    gate = _sigmoid(_dot(h_ref[...], wg_ref[...]))
    x_new = x_ref[...] + gate * _dot(p_ref[...].astype(BF16), wp_ref[...])
    normed = _rms(x_new, g_ref[...], NORM_EPS)
    if last_layer:
        out_refs[0][...] = normed
    else:
        out_refs[0][...] = x_new
        out_refs[1][...] = normed.astype(BF16)


def _ple(x, h, p, wg, wp, layer, g_next, last_layer, tm):
    m, d = x.shape
    row = lambda a: pl.BlockSpec((tm, a.shape[1]), lambda i: (i, 0))
    full = lambda a: pl.BlockSpec((None,) + a.shape[1:], lambda i: (layer, 0, 0))
    g2 = g_next.reshape(1, d)
    x_out = jax.ShapeDtypeStruct((m, d), F32)
    return pl.pallas_call(
        functools.partial(_ple_kernel, last_layer=last_layer),
        grid=(m // tm,),
        in_specs=[row(x), row(h), pl.BlockSpec((None, tm, p.shape[2]), lambda i: (layer, i, 0)), full(wg), full(wp),
                  pl.BlockSpec((1, d), lambda i: (0, 0))],
        out_specs=[row(x)] if last_layer else [row(x), row(x)],
        out_shape=[x_out] if last_layer else [x_out, jax.ShapeDtypeStruct((m, d), BF16)],
        compiler_params=_cparams("parallel"),
        name="ple_gate",
    )(x, h, p, wg, wp, g2)


def _rope_tables(positions, dim, period, pass_through):
    inv_freq = ROPE_THETA ** (-jnp.arange(0, dim, 2, dtype=F32) / dim)
    ang = positions.astype(F32).reshape(-1, 1) * inv_freq
    cos, sin = jnp.cos(ang), jnp.sin(ang)
    m, half = ang.shape
    pad = period - dim
    rest, zero = jnp.full((m, pad), pass_through, F32), jnp.zeros((m, pad), F32)
    zh = jnp.zeros((m, half), F32)
    reps = LANES // period
    c = jnp.tile(jnp.concatenate([cos, cos, rest], axis=1), (1, reps))
    s_lo = jnp.tile(jnp.concatenate([-sin, zh, zero], axis=1), (1, reps))
    s_hi = jnp.tile(jnp.concatenate([zh, sin, zero], axis=1), (1, reps))
    return c, s_lo, s_hi


def _tile(n, pref):
    t = min(n, pref)
    assert n % t == 0, (n, t)
    return t


def kernel(x, p, positions, g_mix, w_in, lambda_q1, lambda_k1, lambda_q2, lambda_k2, g_subln, g_q_latent, w_q_up, g_kv_latent, w_kv_up, w_branch_diff, w_branch_mla, w_out, g_ffn, w_gate_up, w_down, w_ple_in, g_ple, w_ple_gate, g_final):
    b, s, d = x.shape
    depth = w_in.shape[0]
    m = b * s
    tm = _tile(m, 1024)
    tm_small = _tile(m, 512)
    tq = _tile(s, 512)
    tn = 512
    assert s % KEY_CHUNK == 0

    tabs_p = _rope_tables(positions, PARTIAL_ROPE_DIM, DIFF_HEAD_DIM, 1.0)
    tabs_m = _rope_tables(positions, MLA_ROPE_DIM, LANES, 0.0)

    xf = x.reshape(m, d)
    h = _rmsnorm(xf, g_mix[0], BF16, tm_small)
    lat0 = 3 * DIFF_WIDTH
    lat_w = MLA_Q_RANK + MLA_KV_RANK + 2 * LANES
    gates0 = lat0 + MLA_Q_RANK + MLA_KV_RANK + MLA_ROPE_DIM
    assert lat0 % lat_w == 0 and gates0 - lat0 <= lat_w
    qk_dim = MLA_NOPE_DIM + MLA_ROPE_DIM
    w_main = w_in[:, :, :lat0 + lat_w].astype(BF16)
    w_gates = w_in[:, :, gates0:].astype(BF16)
    w_bd, w_bm, w_o = w_branch_diff.astype(BF16), w_branch_mla.astype(BF16), w_out.astype(BF16)
    w_gu, w_dn = w_gate_up.astype(BF16), w_down.astype(BF16)
    w_pg, w_pi = w_ple_gate.astype(BF16), w_ple_in.astype(BF16)
    pf = p.reshape(depth, m, -1)
    for i in range(depth):
        lam_init = 0.8 - 0.6 * math.exp(-0.3 * i)
        w_q = jnp.pad(w_q_up[i].reshape(MLA_Q_RANK, MLA_HEADS, qk_dim),
                      ((0, 0), (0, 0), (0, MLA_QK_PAD - qk_dim))).reshape(MLA_Q_RANK, MLA_HEADS * MLA_QK_PAD).astype(BF16)
        w_kv = w_kv_up[i].reshape(MLA_KV_RANK, MLA_HEADS, MLA_NOPE_DIM + MLA_V_DIM)
        w_k = w_kv[:, :, :MLA_NOPE_DIM].reshape(MLA_KV_RANK, MLA_HEADS * MLA_NOPE_DIM).astype(BF16)
        w_v = w_kv[:, :, MLA_NOPE_DIM:].reshape(MLA_KV_RANK, MLA_HEADS * MLA_V_DIM).astype(BF16)
        lam_params = jnp.stack([lambda_q1[i], lambda_k1[i], lambda_q2[i], lambda_k2[i]]).astype(F32)

        qkv = _qkv_proj(h, w_main, i, tabs_p, tm)
        od = _diff_attention(qkv.reshape(b, s, 3 * DIFF_WIDTH), lam_params, g_subln[i], lam_init, tq)
        qm, km, vm = _latent_proj(h, w_main, i, lat0 // lat_w, g_q_latent[i].reshape(1, -1), w_q,
                                  g_kv_latent[i].reshape(1, -1), w_k, w_v, tabs_m, tm_small)
        om = _mla_attention(qm.reshape(b, s, -1), km.reshape(b, s, -1), vm.reshape(b, s, -1), tq)
        xf, h = _mix_out(xf, h, od.reshape(m, -1), om.reshape(m, -1), w_gates, w_bd, w_bm, w_o, i, g_ffn[i],
                         tm_small, tn)

        xf, h = _ffn(xf, h, w_gu, w_dn, i, g_ple[i], tm_small, tn)

        last = i == depth - 1
        outs = _ple(xf, h, pf, w_pg, w_pi, i, g_final if last else g_mix[i + 1], last, tm_small)
        if last:
            return outs[0].reshape(b, s, d)
        xf, h = outs
```

```python
import functools
import math

import jax
import jax.numpy as jnp
from jax import lax
from jax.experimental import pallas as pl
from jax.experimental.pallas import tpu as pltpu

F32 = jnp.float32
BF16 = jnp.bfloat16

DIFF_HEADS = 8
DIFF_HEAD_DIM = 64
DIFF_WIDTH = DIFF_HEADS * 2 * DIFF_HEAD_DIM
PARTIAL_ROPE_DIM = DIFF_HEAD_DIM // 4
MLA_HEADS = 8
MLA_Q_RANK = 512
MLA_KV_RANK = 256
MLA_NOPE_DIM = 128
MLA_ROPE_DIM = 64
MLA_V_DIM = 128
MLA_QK_PAD = 256
MLA_HEADS_PER_STEP = 2
ROPE_THETA = 500000.0
NORM_EPS = 1e-6
SUBLN_EPS = 1e-5
LOG2E = math.log2(math.e)
LANES = 128
VMEM_LIMIT_BYTES = 56 * 1024 * 1024
KEY_CHUNK = 256
QKV_SUBTILE = 256

def _cparams(*sem):
    return pltpu.CompilerParams(dimension_semantics=sem, vmem_limit_bytes=VMEM_LIMIT_BYTES)


def _dot(a, b):
    return jnp.dot(a, b, preferred_element_type=F32)


def _sigmoid(x):
    return 1.0 / (1.0 + jnp.exp(-x))


def _rms(x, g, eps):
    return x * lax.rsqrt(jnp.mean(x * x, axis=-1, keepdims=True) + eps) * g


def _rope_chunk(z, c, s_lo, s_hi, half):
    return z * c + pltpu.roll(z, LANES - half, 1) * s_lo + pltpu.roll(z, half, 1) * s_hi


def _rmsnorm_kernel(x_ref, g_ref, o_ref):
    o_ref[...] = _rms(x_ref[...], g_ref[...], NORM_EPS).astype(o_ref.dtype)


def _rmsnorm(x, g, out_dtype, tm):
    m, d = x.shape
    return pl.pallas_call(
        _rmsnorm_kernel,
        grid=(m // tm,),
        in_specs=[pl.BlockSpec((tm, d), lambda i: (i, 0)), pl.BlockSpec((1, d), lambda i: (0, 0))],
        out_specs=pl.BlockSpec((tm, d), lambda i: (i, 0)),
        out_shape=jax.ShapeDtypeStruct((m, d), out_dtype),
        compiler_params=_cparams("parallel"),
        name="rmsnorm",
    )(x, g.reshape(1, d))


def _qkv_kernel(h_ref, w_ref, c_ref, slo_ref, shi_ref, o_ref, *, tn, q_scale):
    j = pl.program_id(1)
    scale = jnp.where(j == 0, q_scale, 1.0)
    rotary = j < 2
    c = jnp.where(rotary, c_ref[...], 1.0) * scale
    s_lo = jnp.where(rotary, slo_ref[...], 0.0) * scale
    s_hi = jnp.where(rotary, shi_ref[...], 0.0) * scale
    h = h_ref[...]
    for u in range(tn // QKV_SUBTILE):
        z = _dot(h, w_ref[:, u * QKV_SUBTILE:(u + 1) * QKV_SUBTILE])
        for t in range(QKV_SUBTILE // LANES):
            sl = slice(u * QKV_SUBTILE + t * LANES, u * QKV_SUBTILE + (t + 1) * LANES)
            zc = z[:, t * LANES:(t + 1) * LANES]
            o_ref[:, sl] = _rope_chunk(zc, c, s_lo, s_hi, PARTIAL_ROPE_DIM // 2).astype(o_ref.dtype)


def _qkv_proj(h, w, layer, tabs, tm):
    m, k = h.shape
    tn = DIFF_WIDTH
    tab_spec = pl.BlockSpec((tm, LANES), lambda i, j: (i, 0))
    return pl.pallas_call(
        functools.partial(_qkv_kernel, tn=tn, q_scale=DIFF_HEAD_DIM ** -0.5 * LOG2E),
        grid=(m // tm, 3),
        in_specs=[pl.BlockSpec((tm, k), lambda i, j: (i, 0)), pl.BlockSpec((None, k, tn), lambda i, j: (layer, 0, j)),
                  tab_spec, tab_spec, tab_spec],
        out_specs=pl.BlockSpec((tm, tn), lambda i, j: (i, j)),
        out_shape=jax.ShapeDtypeStruct((m, 3 * tn), BF16),
        compiler_params=_cparams("parallel", "arbitrary"),
        name="qkv_proj",
    )(h, w, *tabs)


def _latent_kernel(h_ref, wl_ref, gq_ref, wq_ref, gkv_ref, wk_ref, wv_ref, c_ref, slo_ref, shi_ref,
                   q_ref, k_ref, v_ref, *, scale):
    z = _dot(h_ref[...], wl_ref[...])
    cq = _rms(z[:, :MLA_Q_RANK], gq_ref[...], NORM_EPS).astype(BF16)
    ckv = _rms(z[:, MLA_Q_RANK:MLA_Q_RANK + MLA_KV_RANK], gkv_ref[...], NORM_EPS).astype(BF16)
    kr_off = MLA_Q_RANK + MLA_KV_RANK
    c, s_lo, s_hi = c_ref[...], slo_ref[...], shi_ref[...]
    half = MLA_ROPE_DIM // 2
    kr = _rope_chunk(z[:, kr_off:kr_off + LANES], c, s_lo, s_hi, half).astype(BF16)
    qf = _dot(cq, wq_ref[...]) * scale
    kn = _dot(ckv, wk_ref[...])
    for h in range(MLA_HEADS):
        o = h * MLA_QK_PAD
        q_ref[:, o:o + LANES] = qf[:, o:o + LANES].astype(BF16)
        q_ref[:, o + LANES:o + 2 * LANES] = _rope_chunk(qf[:, o + LANES:o + 2 * LANES], c, s_lo, s_hi, half).astype(BF16)
        k_ref[:, o:o + LANES] = kn[:, h * LANES:(h + 1) * LANES].astype(BF16)
        k_ref[:, o + LANES:o + 2 * LANES] = kr
    v_ref[...] = _dot(ckv, wv_ref[...]).astype(BF16)


def _latent_proj(h, w_main, layer, lat_tile, gq, wq, gkv, wk, wv, tabs, tm):
    m, k = h.shape
    full = lambda a: pl.BlockSpec(a.shape, lambda i: (0, 0))
    tab_spec = pl.BlockSpec((tm, LANES), lambda i: (i, 0))
    nq, nv = MLA_HEADS * MLA_QK_PAD, MLA_HEADS * MLA_V_DIM
    lat_w = MLA_Q_RANK + MLA_KV_RANK + 2 * LANES
    row = lambda n: pl.BlockSpec((tm, n), lambda i: (i, 0))
    return pl.pallas_call(
        functools.partial(_latent_kernel, scale=(MLA_NOPE_DIM + MLA_ROPE_DIM) ** -0.5 * LOG2E),
        grid=(m // tm,),
        in_specs=[row(k), pl.BlockSpec((None, k, lat_w), lambda i: (layer, 0, lat_tile)), full(gq), full(wq), full(gkv),
                  full(wk), full(wv), tab_spec, tab_spec, tab_spec],
        out_specs=[row(nq), row(nq), row(nv)],
        out_shape=[jax.ShapeDtypeStruct((m, nq), BF16), jax.ShapeDtypeStruct((m, nq), BF16),
                   jax.ShapeDtypeStruct((m, nv), BF16)],
        compiler_params=_cparams("parallel"),
        name="latent_proj",
    )(h, w_main, gq, wq, gkv, wk, wv, *tabs)


def _attend_pipelined(a_chains, b_chains, s_ref, m_ref, n_keys):
    n = len(a_chains)
    nt = (((1,), (1,)), ((), ()))
    m_prev = [m_ref[u] for u in range(n)]
    acc, l, m_new = [None] * n, [None] * n, [None] * n
    for c in range(n_keys // KEY_CHUNK):
        rows = slice(c * KEY_CHUNK, (c + 1) * KEY_CHUNK)
        for u in range(n):
            p = jnp.exp2(s_ref[u, rows, :] - m_prev[u])
            lc = jnp.sum(p, axis=0, keepdims=True)
            pv = _dot(b_chains[u](rows), p.astype(BF16))
            acc[u] = pv if c == 0 else acc[u] + pv
            l[u] = lc if c == 0 else l[u] + lc
        for u, (load_k, qe) in enumerate(a_chains):
            sc = lax.dot_general(load_k(rows), qe, nt, preferred_element_type=F32)
            s_ref[u, rows, :] = sc
            mc = jnp.max(sc, axis=0, keepdims=True)
            m_new[u] = mc if c == 0 else jnp.maximum(m_new[u], mc)
    for u in range(n):
        m_ref[u] = m_new[u]
    return [acc[u] / l[u] for u in range(n)]


def _attn_scratch(n_chains, n_keys, tq, dv):
    return [pltpu.VMEM((dv, n_keys), BF16), pltpu.VMEM((n_chains, n_keys, tq), F32),
            pltpu.VMEM((n_chains, 1, tq), F32)]


def _init_attn_scratch(t, tiles_per_head, v_ref, vt_ref, s_ref, m_ref):
    @pl.when(t == 0)
    def _():
        s_ref[...] = jnp.zeros_like(s_ref)
        m_ref[...] = jnp.zeros_like(m_ref)

    @pl.when(jnp.maximum(t - 1, 0) % tiles_per_head == 0)
    def _():
        vt_ref[...] = v_ref[...].astype(F32).T.astype(BF16)


def _flat_tile(u, n_heads, tiles_per_head):
    return u // (n_heads * tiles_per_head), (u // tiles_per_head) % n_heads, u % tiles_per_head


def _diff_attn_kernel(lam_ref, g_ref, q_ref, k_ref, v_ref, o_ref, vt_ref, s_ref, m_ref, *, lam_init, tiles_per_head):
    _init_attn_scratch(pl.program_id(0), tiles_per_head, v_ref, vt_ref, s_ref, m_ref)
    lp = lam_ref[...]
    lam = (jnp.exp(jnp.sum(lp[0:1] * lp[1:2], axis=-1, keepdims=True))
           - jnp.exp(jnp.sum(lp[2:3] * lp[3:4], axis=-1, keepdims=True)) + lam_init)
    q = q_ref[...]
    lane = lax.broadcasted_iota(jnp.int32, q.shape, 1)
    zero = jnp.zeros_like(q)
    load_k = lambda rows: k_ref[rows, :]
    load_vt = lambda cols: vt_ref[:, cols]
    o1, o2 = _attend_pipelined([(load_k, jnp.where(lane < DIFF_HEAD_DIM, q, zero)),
                                (load_k, jnp.where(lane >= DIFF_HEAD_DIM, q, zero))],
                               [load_vt, load_vt], s_ref, m_ref, k_ref.shape[0])
    o = (o1 - lam * o2).T
    o_ref[...] = (_rms(o, g_ref[...], SUBLN_EPS) * (1.0 - lam_init)).astype(o_ref.dtype)


def _diff_attention(qkv, lam_params, g_subln, lam_init, tq):
    b, s, _ = qkv.shape
    dv = 2 * DIFF_HEAD_DIM
    nh, nq = DIFF_HEADS, s // tq
    n_tiles = b * nh * nq
    tile_a = lambda t: _flat_tile(jnp.minimum(t, n_tiles - 1), nh, nq)
    tile_b = lambda t: _flat_tile(jnp.maximum(t - 1, 0), nh, nq)

    def q_map(t):
        bi, h, qi = tile_a(t)
        return bi, qi, h

    def k_map(t):
        bi, h, _ = tile_a(t)
        return bi, 0, nh + h

    def v_map(t):
        bi, h, _ = tile_b(t)
        return bi, 0, 2 * nh + h

    def o_map(t):
        bi, h, qi = tile_b(t)
        return bi, qi, h

    return pl.pallas_call(
        functools.partial(_diff_attn_kernel, lam_init=lam_init, tiles_per_head=nq),
        grid=(n_tiles + 1,),
        in_specs=[pl.BlockSpec(lam_params.shape, lambda t: (0, 0)),
                  pl.BlockSpec((1, dv), lambda t: (0, 0)),
                  pl.BlockSpec((None, tq, dv), q_map),
                  pl.BlockSpec((None, s, dv), k_map),
                  pl.BlockSpec((None, s, dv), v_map)],
        out_specs=pl.BlockSpec((None, tq, dv), o_map),
        out_shape=jax.ShapeDtypeStruct((b, s, nh * dv), BF16),
        scratch_shapes=_attn_scratch(2, s, tq, dv),
        compiler_params=_cparams("arbitrary"),
        name="diff_attention",
    )(lam_params, g_subln.reshape(1, dv), qkv, qkv, qkv)


def _mla_attn_kernel(q_ref, *refs, tiles_per_head):
    k_refs = refs[:MLA_HEADS_PER_STEP]
    v_ref, o_ref, vt_ref, s_ref, m_ref = refs[MLA_HEADS_PER_STEP:]
    _init_attn_scratch(pl.program_id(0), tiles_per_head, v_ref, vt_ref, s_ref, m_ref)
    a_chains, b_chains = [], []
    for h, k_ref in enumerate(k_refs):
        kc = slice(h * MLA_QK_PAD, (h + 1) * MLA_QK_PAD)
        vr = slice(h * MLA_V_DIM, (h + 1) * MLA_V_DIM)
        a_chains.append((lambda rows, k_ref=k_ref: k_ref[rows, :], q_ref[:, kc]))
        b_chains.append(lambda cols, vr=vr: vt_ref[vr, cols])
    outs = _attend_pipelined(a_chains, b_chains, s_ref, m_ref, v_ref.shape[0])
    for h, o in enumerate(outs):
        o_ref[:, h * MLA_V_DIM:(h + 1) * MLA_V_DIM] = o.T.astype(o_ref.dtype)


def _mla_attention(q, k, v, tq):
    b, s, _ = q.shape
    hp = MLA_HEADS_PER_STEP
    nh, dq, dv = MLA_HEADS // hp, hp * MLA_QK_PAD, hp * MLA_V_DIM
    nq = s // tq
    n_tiles = b * nh * nq
    tile_a = lambda t: _flat_tile(jnp.minimum(t, n_tiles - 1), nh, nq)
    tile_b = lambda t: _flat_tile(jnp.maximum(t - 1, 0), nh, nq)

    def q_map(t):
        bi, h, qi = tile_a(t)
        return bi, qi, h

    def k_map(i):
        def index(t):
            bi, h, _ = tile_a(t)
            return bi, 0, hp * h + i
        return index

    def v_map(t):
        bi, h, _ = tile_b(t)
        return bi, 0, h

    def o_map(t):
        bi, h, qi = tile_b(t)
        return bi, qi, h

    return pl.pallas_call(
        functools.partial(_mla_attn_kernel, tiles_per_head=nq),
        grid=(n_tiles + 1,),
        in_specs=[pl.BlockSpec((None, tq, dq), q_map)]
                 + [pl.BlockSpec((None, s, MLA_QK_PAD), k_map(i)) for i in range(hp)]
                 + [pl.BlockSpec((None, s, dv), v_map)],
        out_specs=pl.BlockSpec((None, tq, dv), o_map),
        out_shape=jax.ShapeDtypeStruct((b, s, nh * dv), BF16),
        scratch_shapes=_attn_scratch(hp, s, tq, dv),
        compiler_params=_cparams("arbitrary"),
        name="mla_attention",
    )(q, *([k] * hp), v)


def _begin_rows(x_ref, xo_ref):
    @pl.when(pl.program_id(1) == 0)
    def _():
        xo_ref[...] = x_ref[...]


def _finish_rows(xo_ref, g_ref, ho_ref):
    @pl.when(pl.program_id(1) == pl.num_programs(1) - 1)
    def _():
        ho_ref[...] = _rms(xo_ref[...], g_ref[...], NORM_EPS).astype(ho_ref.dtype)


def _mix_out_kernel(x_ref, h_ref, od_ref, om_ref, wga_ref, wgb_ref, wbd_ref, wbm_ref, wo_ref, g_ref, xo_ref, ho_ref):
    _begin_rows(x_ref, xo_ref)
    h = h_ref[...]
    ga = _sigmoid(_dot(h, wga_ref[...]))
    gb = _sigmoid(_dot(h, wgb_ref[...]))
    merged = (ga * _dot(od_ref[...], wbd_ref[...]) + gb * _dot(om_ref[...], wbm_ref[...])).astype(BF16)
    xo_ref[...] += _dot(merged, wo_ref[...])
    _finish_rows(xo_ref, g_ref, ho_ref)


def _mix_out(x, h, od, om, w_gates, wbd, wbm, wo, layer, g_next, tm, tn):
    m, d = x.shape
    n = wbd.shape[2]
    row = lambda a: pl.BlockSpec((tm, a.shape[1]), lambda i, j: (i, 0))
    col = lambda a, off: pl.BlockSpec((None, a.shape[1], tn), lambda i, j: (layer, 0, j + off))
    return pl.pallas_call(
        _mix_out_kernel,
        grid=(m // tm, n // tn),
        in_specs=[row(x), row(h), row(od), row(om), col(w_gates, 0), col(w_gates, n // tn), col(wbd, 0), col(wbm, 0),
                  pl.BlockSpec((None, tn, d), lambda i, j: (layer, j, 0)), pl.BlockSpec((1, d), lambda i, j: (0, 0))],
        out_specs=[row(x), row(x)],
        out_shape=[jax.ShapeDtypeStruct((m, d), F32), jax.ShapeDtypeStruct((m, d), BF16)],
        compiler_params=_cparams("parallel", "arbitrary"),
        name="mix_out",
    )(x, h, od, om, w_gates, w_gates, wbd, wbm, wo, g_next.reshape(1, d))


def _ffn_kernel(x_ref, h_ref, wg_ref, wu_ref, wd_ref, g_ref, xo_ref, ho_ref):
    _begin_rows(x_ref, xo_ref)
    h = h_ref[...]
    gate = _dot(h, wg_ref[...])
    act = (gate * _sigmoid(gate) * _dot(h, wu_ref[...])).astype(BF16)
    xo_ref[...] += _dot(act, wd_ref[...])
    _finish_rows(xo_ref, g_ref, ho_ref)


def _ffn(x, h, w_gate_up, w_down, layer, g_next, tm, tn):
    m, d = x.shape
    n = w_down.shape[1]
    row = lambda a: pl.BlockSpec((tm, a.shape[1]), lambda i, j: (i, 0))
    return pl.pallas_call(
        _ffn_kernel,
        grid=(m // tm, n // tn),
        in_specs=[row(x), row(h), pl.BlockSpec((None, d, tn), lambda i, j: (layer, 0, j)),
                  pl.BlockSpec((None, d, tn), lambda i, j: (layer, 0, j + n // tn)),
                  pl.BlockSpec((None, tn, d), lambda i, j: (layer, j, 0)), pl.BlockSpec((1, d), lambda i, j: (0, 0))],
        out_specs=[row(x), row(x)],
        out_shape=[jax.ShapeDtypeStruct((m, d), F32), jax.ShapeDtypeStruct((m, d), BF16)],
        compiler_params=_cparams("parallel", "arbitrary"),
        name="ffn",
    )(x, h, w_gate_up, w_gate_up, w_down, g_next.reshape(1, d))


def _ple_kernel(x_ref, h_ref, p_ref, wg_ref, wp_ref, g_ref, *out_refs, last_layer):
    gate = _sigmoid(_dot(h_ref[...], wg_ref[...]))
    x_new = x_ref[...] + gate * _dot(p_ref[...].astype(BF16), wp_ref[...])
    normed = _rms(x_new, g_ref[...], NORM_EPS)
    if last_layer:
        out_refs[0][...] = normed
    else:
        out_refs[0][...] = x_new
        out_refs[1][...] = normed.astype(BF16)


def _ple(x, h, p, wg, wp, layer, g_next, last_layer, tm):
    m, d = x.shape
    row = lambda a: pl.BlockSpec((tm, a.shape[1]), lambda i: (i, 0))
    full = lambda a: pl.BlockSpec((None,) + a.shape[1:], lambda i: (layer, 0, 0))
    g2 = g_next.reshape(1, d)
    x_out = jax.ShapeDtypeStruct((m, d), F32)
    return pl.pallas_call(
        functools.partial(_ple_kernel, last_layer=last_layer),
        grid=(m // tm,),
        in_specs=[row(x), row(h), pl.BlockSpec((None, tm, p.shape[2]), lambda i: (layer, i, 0)), full(wg), full(wp),
                  pl.BlockSpec((1, d), lambda i: (0, 0))],
        out_specs=[row(x)] if last_layer else [row(x), row(x)],
        out_shape=[x_out] if last_layer else [x_out, jax.ShapeDtypeStruct((m, d), BF16)],
        compiler_params=_cparams("parallel"),
        name="ple_gate",
    )(x, h, p, wg, wp, g2)


def _rope_tables(positions, dim, period, pass_through):
    inv_freq = ROPE_THETA ** (-jnp.arange(0, dim, 2, dtype=F32) / dim)
    ang = positions.astype(F32).reshape(-1, 1) * inv_freq
    cos, sin = jnp.cos(ang), jnp.sin(ang)
    m, half = ang.shape
    pad = period - dim
    rest, zero = jnp.full((m, pad), pass_through, F32), jnp.zeros((m, pad), F32)
    zh = jnp.zeros((m, half), F32)
    reps = LANES // period
    c = jnp.tile(jnp.concatenate([cos, cos, rest], axis=1), (1, reps))
    s_lo = jnp.tile(jnp.concatenate([-sin, zh, zero], axis=1), (1, reps))
    s_hi = jnp.tile(jnp.concatenate([zh, sin, zero], axis=1), (1, reps))
    return c, s_lo, s_hi


def _tile(n, pref):
    t = min(n, pref)
    assert n % t == 0, (n, t)
    return t


def kernel(x, p, positions, g_mix, w_in, lambda_q1, lambda_k1, lambda_q2, lambda_k2, g_subln, g_q_latent, w_q_up, g_kv_latent, w_kv_up, w_branch_diff, w_branch_mla, w_out, g_ffn, w_gate_up, w_down, w_ple_in, g_ple, w_ple_gate, g_final):
    b, s, d = x.shape
    depth = w_in.shape[0]
    m = b * s
    tm = _tile(m, 1024)
    tm_small = _tile(m, 512)
    tq = _tile(s, 512)
    tn = 512
    assert s % KEY_CHUNK == 0

    tabs_p = _rope_tables(positions, PARTIAL_ROPE_DIM, DIFF_HEAD_DIM, 1.0)
    tabs_m = _rope_tables(positions, MLA_ROPE_DIM, LANES, 0.0)

    xf = x.reshape(m, d)
    h = _rmsnorm(xf, g_mix[0], BF16, tm_small)
    lat0 = 3 * DIFF_WIDTH
    lat_w = MLA_Q_RANK + MLA_KV_RANK + 2 * LANES
    gates0 = lat0 + MLA_Q_RANK + MLA_KV_RANK + MLA_ROPE_DIM
    assert lat0 % lat_w == 0 and gates0 - lat0 <= lat_w
    qk_dim = MLA_NOPE_DIM + MLA_ROPE_DIM
    w_main = w_in[:, :, :lat0 + lat_w].astype(BF16)
    w_gates = w_in[:, :, gates0:].astype(BF16)
    w_bd, w_bm, w_o = w_branch_diff.astype(BF16), w_branch_mla.astype(BF16), w_out.astype(BF16)
    w_gu, w_dn = w_gate_up.astype(BF16), w_down.astype(BF16)
    w_pg, w_pi = w_ple_gate.astype(BF16), w_ple_in.astype(BF16)
    pf = p.reshape(depth, m, -1)
    for i in range(depth):
        lam_init = 0.8 - 0.6 * math.exp(-0.3 * i)
        w_q = jnp.pad(w_q_up[i].reshape(MLA_Q_RANK, MLA_HEADS, qk_dim),
                      ((0, 0), (0, 0), (0, MLA_QK_PAD - qk_dim))).reshape(MLA_Q_RANK, MLA_HEADS * MLA_QK_PAD).astype(BF16)
        w_kv = w_kv_up[i].reshape(MLA_KV_RANK, MLA_HEADS, MLA_NOPE_DIM + MLA_V_DIM)
        w_k = w_kv[:, :, :MLA_NOPE_DIM].reshape(MLA_KV_RANK, MLA_HEADS * MLA_NOPE_DIM).astype(BF16)
        w_v = w_kv[:, :, MLA_NOPE_DIM:].reshape(MLA_KV_RANK, MLA_HEADS * MLA_V_DIM).astype(BF16)
        lam_params = jnp.stack([lambda_q1[i], lambda_k1[i], lambda_q2[i], lambda_k2[i]]).astype(F32)

        qkv = _qkv_proj(h, w_main, i, tabs_p, tm)
        od = _diff_attention(qkv.reshape(b, s, 3 * DIFF_WIDTH), lam_params, g_subln[i], lam_init, tq)
        qm, km, vm = _latent_proj(h, w_main, i, lat0 // lat_w, g_q_latent[i].reshape(1, -1), w_q,
                                  g_kv_latent[i].reshape(1, -1), w_k, w_v, tabs_m, tm_small)
        om = _mla_attention(qm.reshape(b, s, -1), km.reshape(b, s, -1), vm.reshape(b, s, -1), tq)
        xf, h = _mix_out(xf, h, od.reshape(m, -1), om.reshape(m, -1), w_gates, w_bd, w_bm, w_o, i, g_ffn[i],
                         tm_small, tn)

        xf, h = _ffn(xf, h, w_gu, w_dn, i, g_ple[i], tm_small, tn)

        last = i == depth - 1
        outs = _ple(xf, h, pf, w_pg, w_pi, i, g_final if last else g_mix[i + 1], last, tm_small)
        if last:
            return outs[0].reshape(b, s, d)
        xf, h = outs
```

```python
import functools
import math

import jax
import jax.numpy as jnp
from jax import lax
from jax.experimental import pallas as pl
from jax.experimental.pallas import tpu as pltpu

F32 = jnp.float32
BF16 = jnp.bfloat16

DIFF_HEADS = 8
DIFF_HEAD_DIM = 64
DIFF_WIDTH = DIFF_HEADS * 2 * DIFF_HEAD_DIM
PARTIAL_ROPE_DIM = DIFF_HEAD_DIM // 4
MLA_HEADS = 8
MLA_Q_RANK = 512
MLA_KV_RANK = 256
MLA_NOPE_DIM = 128
MLA_ROPE_DIM = 64
MLA_V_DIM = 128
MLA_QK_PAD = 256
MLA_HEADS_PER_STEP = 2
ROPE_THETA = 500000.0
NORM_EPS = 1e-6
SUBLN_EPS = 1e-5
LOG2E = math.log2(math.e)
LANES = 128
VMEM_LIMIT_BYTES = 56 * 1024 * 1024
KEY_CHUNK = 256
QKV_SUBTILE = 256

def _cparams(*sem):
    return pltpu.CompilerParams(dimension_semantics=sem, vmem_limit_bytes=VMEM_LIMIT_BYTES)


def _dot(a, b):
    return jnp.dot(a, b, preferred_element_type=F32)


def _sigmoid(x):
    return 1.0 / (1.0 + jnp.exp(-x))


def _rms(x, g, eps):
    return x * lax.rsqrt(jnp.mean(x * x, axis=-1, keepdims=True) + eps) * g


def _rope_chunk(z, c, s_lo, s_hi, half):
    return z * c + pltpu.roll(z, LANES - half, 1) * s_lo + pltpu.roll(z, half, 1) * s_hi


def _rmsnorm_kernel(x_ref, g_ref, o_ref):
    o_ref[...] = _rms(x_ref[...], g_ref[...], NORM_EPS).astype(o_ref.dtype)


def _rmsnorm(x, g, out_dtype, tm):
    m, d = x.shape
    return pl.pallas_call(
        _rmsnorm_kernel,
        grid=(m // tm,),
        in_specs=[pl.BlockSpec((tm, d), lambda i: (i, 0)), pl.BlockSpec((1, d), lambda i: (0, 0))],
        out_specs=pl.BlockSpec((tm, d), lambda i: (i, 0)),
        out_shape=jax.ShapeDtypeStruct((m, d), out_dtype),
        compiler_params=_cparams("parallel"),
        name="rmsnorm",
    )(x, g.reshape(1, d))


def _qkv_kernel(h_ref, w_ref, c_ref, slo_ref, shi_ref, o_ref, *, tn, q_scale):
    j = pl.program_id(1)
    scale = jnp.where(j == 0, q_scale, 1.0)
    rotary = j < 2
    c = jnp.where(rotary, c_ref[...], 1.0) * scale
    s_lo = jnp.where(rotary, slo_ref[...], 0.0) * scale
    s_hi = jnp.where(rotary, shi_ref[...], 0.0) * scale
    h = h_ref[...]
    for u in range(tn // QKV_SUBTILE):
        z = _dot(h, w_ref[:, u * QKV_SUBTILE:(u + 1) * QKV_SUBTILE])
        for t in range(QKV_SUBTILE // LANES):
            sl = slice(u * QKV_SUBTILE + t * LANES, u * QKV_SUBTILE + (t + 1) * LANES)
            zc = z[:, t * LANES:(t + 1) * LANES]
            o_ref[:, sl] = _rope_chunk(zc, c, s_lo, s_hi, PARTIAL_ROPE_DIM // 2).astype(o_ref.dtype)


def _qkv_proj(h, w, layer, tabs, tm):
    m, k = h.shape
    tn = DIFF_WIDTH
    tab_spec = pl.BlockSpec((tm, LANES), lambda i, j: (i, 0))
    return pl.pallas_call(
        functools.partial(_qkv_kernel, tn=tn, q_scale=DIFF_HEAD_DIM ** -0.5 * LOG2E),
        grid=(m // tm, 3),
        in_specs=[pl.BlockSpec((tm, k), lambda i, j: (i, 0)), pl.BlockSpec((None, k, tn), lambda i, j: (layer, 0, j)),
                  tab_spec, tab_spec, tab_spec],
        out_specs=pl.BlockSpec((tm, tn), lambda i, j: (i, j)),
        out_shape=jax.ShapeDtypeStruct((m, 3 * tn), BF16),
        compiler_params=_cparams("parallel", "arbitrary"),
        name="qkv_proj",
    )(h, w, *tabs)


def _latent_kernel(h_ref, wl_ref, gq_ref, wq_ref, gkv_ref, wk_ref, wv_ref, c_ref, slo_ref, shi_ref,
                   q_ref, k_ref, v_ref, *, scale):
    z = _dot(h_ref[...], wl_ref[...])
    cq = _rms(z[:, :MLA_Q_RANK], gq_ref[...], NORM_EPS).astype(BF16)
    ckv = _rms(z[:, MLA_Q_RANK:MLA_Q_RANK + MLA_KV_RANK], gkv_ref[...], NORM_EPS).astype(BF16)
    kr_off = MLA_Q_RANK + MLA_KV_RANK
    c, s_lo, s_hi = c_ref[...], slo_ref[...], shi_ref[...]
    half = MLA_ROPE_DIM // 2
    kr = _rope_chunk(z[:, kr_off:kr_off + LANES], c, s_lo, s_hi, half).astype(BF16)
    qf = _dot(cq, wq_ref[...]) * scale
    kn = _dot(ckv, wk_ref[...])
    for h in range(MLA_HEADS):
        o = h * MLA_QK_PAD
        q_ref[:, o:o + LANES] = qf[:, o:o + LANES].astype(BF16)
        q_ref[:, o + LANES:o + 2 * LANES] = _rope_chunk(qf[:, o + LANES:o + 2 * LANES], c, s_lo, s_hi, half).astype(BF16)
        k_ref[:, o:o + LANES] = kn[:, h * LANES:(h + 1) * LANES].astype(BF16)
        k_ref[:, o + LANES:o + 2 * LANES] = kr
    v_ref[...] = _dot(ckv, wv_ref[...]).astype(BF16)


def _latent_proj(h, w_main, layer, lat_tile, gq, wq, gkv, wk, wv, tabs, tm):
    m, k = h.shape
    full = lambda a: pl.BlockSpec(a.shape, lambda i: (0, 0))
    tab_spec = pl.BlockSpec((tm, LANES), lambda i: (i, 0))
    nq, nv = MLA_HEADS * MLA_QK_PAD, MLA_HEADS * MLA_V_DIM
    lat_w = MLA_Q_RANK + MLA_KV_RANK + 2 * LANES
    row = lambda n: pl.BlockSpec((tm, n), lambda i: (i, 0))
    return pl.pallas_call(
        functools.partial(_latent_kernel, scale=(MLA_NOPE_DIM + MLA_ROPE_DIM) ** -0.5 * LOG2E),
        grid=(m // tm,),
        in_specs=[row(k), pl.BlockSpec((None, k, lat_w), lambda i: (layer, 0, lat_tile)), full(gq), full(wq), full(gkv),
                  full(wk), full(wv), tab_spec, tab_spec, tab_spec],
        out_specs=[row(nq), row(nq), row(nv)],
        out_shape=[jax.ShapeDtypeStruct((m, nq), BF16), jax.ShapeDtypeStruct((m, nq), BF16),
                   jax.ShapeDtypeStruct((m, nv), BF16)],
        compiler_params=_cparams("parallel"),
        name="latent_proj",
    )(h, w_main, gq, wq, gkv, wk, wv, *tabs)


def _attend_pipelined(a_chains, b_chains, s_ref, m_ref, n_keys):
    n = len(a_chains)
    nt = (((1,), (1,)), ((), ()))
    m_prev = [m_ref[u] for u in range(n)]
    acc, l, m_new = [None] * n, [None] * n, [None] * n
    for c in range(n_keys // KEY_CHUNK):
        rows = slice(c * KEY_CHUNK, (c + 1) * KEY_CHUNK)
        for u in range(n):
            p = jnp.exp2(s_ref[u, rows, :] - m_prev[u])
            lc = jnp.sum(p, axis=0, keepdims=True)
            pv = _dot(b_chains[u](rows), p.astype(BF16))
            acc[u] = pv if c == 0 else acc[u] + pv
            l[u] = lc if c == 0 else l[u] + lc
        for u, (load_k, qe) in enumerate(a_chains):
            sc = lax.dot_general(load_k(rows), qe, nt, preferred_element_type=F32)
            s_ref[u, rows, :] = sc
            mc = jnp.max(sc, axis=0, keepdims=True)
            m_new[u] = mc if c == 0 else jnp.maximum(m_new[u], mc)
    for u in range(n):
        m_ref[u] = m_new[u]
    return [acc[u] / l[u] for u in range(n)]


def _attn_scratch(n_chains, n_keys, tq, dv):
    return [pltpu.VMEM((dv, n_keys), BF16), pltpu.VMEM((n_chains, n_keys, tq), F32),
            pltpu.VMEM((n_chains, 1, tq), F32)]


def _init_attn_scratch(t, tiles_per_head, v_ref, vt_ref, s_ref, m_ref):
    @pl.when(t == 0)
    def _():
        s_ref[...] = jnp.zeros_like(s_ref)
        m_ref[...] = jnp.zeros_like(m_ref)

    @pl.when(jnp.maximum(t - 1, 0) % tiles_per_head == 0)
    def _():
        vt_ref[...] = v_ref[...].astype(F32).T.astype(BF16)


def _flat_tile(u, n_heads, tiles_per_head):
    return u // (n_heads * tiles_per_head), (u // tiles_per_head) % n_heads, u % tiles_per_head


def _diff_attn_kernel(lam_ref, g_ref, q_ref, k_ref, v_ref, o_ref, vt_ref, s_ref, m_ref, *, lam_init, tiles_per_head):
    _init_attn_scratch(pl.program_id(0), tiles_per_head, v_ref, vt_ref, s_ref, m_ref)
    lp = lam_ref[...]
    lam = (jnp.exp(jnp.sum(lp[0:1] * lp[1:2], axis=-1, keepdims=True))
           - jnp.exp(jnp.sum(lp[2:3] * lp[3:4], axis=-1, keepdims=True)) + lam_init)
    q = q_ref[...]
    lane = lax.broadcasted_iota(jnp.int32, q.shape, 1)
    zero = jnp.zeros_like(q)
    load_k = lambda rows: k_ref[rows, :]
    load_vt = lambda cols: vt_ref[:, cols]
    o1, o2 = _attend_pipelined([(load_k, jnp.where(lane < DIFF_HEAD_DIM, q, zero)),
                                (load_k, jnp.where(lane >= DIFF_HEAD_DIM, q, zero))],
                               [load_vt, load_vt], s_ref, m_ref, k_ref.shape[0])
    o = (o1 - lam * o2).T
    o_ref[...] = (_rms(o, g_ref[...], SUBLN_EPS) * (1.0 - lam_init)).astype(o_ref.dtype)


def _diff_attention(qkv, lam_params, g_subln, lam_init, tq):
    b, s, _ = qkv.shape
    dv = 2 * DIFF_HEAD_DIM
    nh, nq = DIFF_HEADS, s // tq
    n_tiles = b * nh * nq
    tile_a = lambda t: _flat_tile(jnp.minimum(t, n_tiles - 1), nh, nq)
    tile_b = lambda t: _flat_tile(jnp.maximum(t - 1, 0), nh, nq)

    def q_map(t):
        bi, h, qi = tile_a(t)
        return bi, qi, h

    def k_map(t):
        bi, h, _ = tile_a(t)
        return bi, 0, nh + h

    def v_map(t):
        bi, h, _ = tile_b(t)
        return bi, 0, 2 * nh + h

    def o_map(t):
        bi, h, qi = tile_b(t)
        return bi, qi, h

    return pl.pallas_call(
        functools.partial(_diff_attn_kernel, lam_init=lam_init, tiles_per_head=nq),
        grid=(n_tiles + 1,),
        in_specs=[pl.BlockSpec(lam_params.shape, lambda t: (0, 0)),
                  pl.BlockSpec((1, dv), lambda t: (0, 0)),
                  pl.BlockSpec((None, tq, dv), q_map),
                  pl.BlockSpec((None, s, dv), k_map),
                  pl.BlockSpec((None, s, dv), v_map)],
        out_specs=pl.BlockSpec((None, tq, dv), o_map),
        out_shape=jax.ShapeDtypeStruct((b, s, nh * dv), BF16),
        scratch_shapes=_attn_scratch(2, s, tq, dv),
        compiler_params=_cparams("arbitrary"),
        name="diff_attention",
    )(lam_params, g_subln.reshape(1, dv), qkv, qkv, qkv)


def _mla_attn_kernel(q_ref, k_ref, v_ref, o_ref, vt_ref, s_ref, m_ref, *, tiles_per_head):
    _init_attn_scratch(pl.program_id(0), tiles_per_head, v_ref, vt_ref, s_ref, m_ref)
    a_chains, b_chains = [], []
    for h in range(MLA_HEADS_PER_STEP):
        kc = slice(h * MLA_QK_PAD, (h + 1) * MLA_QK_PAD)
        vr = slice(h * MLA_V_DIM, (h + 1) * MLA_V_DIM)
        a_chains.append((lambda rows, kc=kc: k_ref[rows, kc], q_ref[:, kc]))
        b_chains.append(lambda cols, vr=vr: vt_ref[vr, cols])
    outs = _attend_pipelined(a_chains, b_chains, s_ref, m_ref, k_ref.shape[0])
    for h, o in enumerate(outs):
        o_ref[:, h * MLA_V_DIM:(h + 1) * MLA_V_DIM] = o.T.astype(o_ref.dtype)


def _mla_attention(q, k, v, tq):
    b, s, _ = q.shape
    hp = MLA_HEADS_PER_STEP
    nh, dq, dv = MLA_HEADS // hp, hp * MLA_QK_PAD, hp * MLA_V_DIM
    nq = s // tq
    n_tiles = b * nh * nq
    tile_a = lambda t: _flat_tile(jnp.minimum(t, n_tiles - 1), nh, nq)
    tile_b = lambda t: _flat_tile(jnp.maximum(t - 1, 0), nh, nq)

    def q_map(t):
        bi, h, qi = tile_a(t)
        return bi, qi, h

    def k_map(t):
        bi, h, _ = tile_a(t)
        return bi, 0, h

    def v_map(t):
        bi, h, _ = tile_b(t)
        return bi, 0, h

    def o_map(t):
        bi, h, qi = tile_b(t)
        return bi, qi, h

    return pl.pallas_call(
        functools.partial(_mla_attn_kernel, tiles_per_head=nq),
        grid=(n_tiles + 1,),
        in_specs=[pl.BlockSpec((None, tq, dq), q_map), pl.BlockSpec((None, s, dq), k_map),
                  pl.BlockSpec((None, s, dv), v_map)],
        out_specs=pl.BlockSpec((None, tq, dv), o_map),
        out_shape=jax.ShapeDtypeStruct((b, s, nh * dv), BF16),
        scratch_shapes=_attn_scratch(hp, s, tq, dv),
        compiler_params=_cparams("arbitrary"),
        name="mla_attention",
    )(q, k, v)


def _begin_rows(x_ref, xo_ref):
    @pl.when(pl.program_id(1) == 0)
    def _():
        xo_ref[...] = x_ref[...]


def _mix_out_kernel(x_ref, h_ref, od_ref, om_ref, wga_ref, wgb_ref, wbd_ref, wbm_ref, wo_ref, xo_ref):
    _begin_rows(x_ref, xo_ref)
    h = h_ref[...]
    ga = _sigmoid(_dot(h, wga_ref[...]))
    gb = _sigmoid(_dot(h, wgb_ref[...]))
    merged = (ga * _dot(od_ref[...], wbd_ref[...]) + gb * _dot(om_ref[...], wbm_ref[...])).astype(BF16)
    xo_ref[...] += _dot(merged, wo_ref[...])


def _mix_out(x, h, od, om, w_gates, wbd, wbm, wo, layer, tm, tn):
    m, d = x.shape
    n = wbd.shape[2]
    row = lambda a: pl.BlockSpec((tm, a.shape[1]), lambda i, j: (i, 0))
    col = lambda a, off: pl.BlockSpec((None, a.shape[1], tn), lambda i, j: (layer, 0, j + off))
    return pl.pallas_call(
        _mix_out_kernel,
        grid=(m // tm, n // tn),
        in_specs=[row(x), row(h), row(od), row(om), col(w_gates, 0), col(w_gates, n // tn), col(wbd, 0), col(wbm, 0),
                  pl.BlockSpec((None, tn, d), lambda i, j: (layer, j, 0))],
        out_specs=row(x),
        out_shape=jax.ShapeDtypeStruct((m, d), F32),
        compiler_params=_cparams("parallel", "arbitrary"),
        name="mix_out",
    )(x, h, od, om, w_gates, w_gates, wbd, wbm, wo)


def _ffn_kernel(x_ref, g_ref, wg_ref, wu_ref, wd_ref, xo_ref, h_ref):
    @pl.when(pl.program_id(1) == 0)
    def _():
        x = x_ref[...]
        xo_ref[...] = x
        h_ref[...] = _rms(x, g_ref[...], NORM_EPS).astype(h_ref.dtype)

    h = h_ref[...]
    gate = _dot(h, wg_ref[...])
    act = (gate * _sigmoid(gate) * _dot(h, wu_ref[...])).astype(BF16)
    xo_ref[...] += _dot(act, wd_ref[...])


def _ffn(x, g, w_gate_up, w_down, layer, tm, tn):
    m, d = x.shape
    n = w_down.shape[1]
    row = pl.BlockSpec((tm, d), lambda i, j: (i, 0))
    return pl.pallas_call(
        _ffn_kernel,
        grid=(m // tm, n // tn),
        in_specs=[row, pl.BlockSpec((1, d), lambda i, j: (0, 0)),
                  pl.BlockSpec((None, d, tn), lambda i, j: (layer, 0, j)),
                  pl.BlockSpec((None, d, tn), lambda i, j: (layer, 0, j + n // tn)),
                  pl.BlockSpec((None, tn, d), lambda i, j: (layer, j, 0))],
        out_specs=row,
        out_shape=jax.ShapeDtypeStruct((m, d), F32),
        scratch_shapes=[pltpu.VMEM((tm, d), BF16)],
        compiler_params=_cparams("parallel", "arbitrary"),
        name="ffn",
    )(x, g.reshape(1, d), w_gate_up, w_gate_up, w_down)


def _ple_kernel(x_ref, g_ref, p_ref, wg_ref, wp_ref, gn_ref, *out_refs, last_layer):
    x = x_ref[...]
    gate = _sigmoid(_dot(_rms(x, g_ref[...], NORM_EPS).astype(BF16), wg_ref[...]))
    x_new = x + gate * _dot(p_ref[...].astype(BF16), wp_ref[...])
    normed = _rms(x_new, gn_ref[...], NORM_EPS)
    if last_layer:
        out_refs[0][...] = normed
    else:
        out_refs[0][...] = x_new
        out_refs[1][...] = normed.astype(BF16)


def _ple(x, g, p, wg, wp, layer, g_next, last_layer, tm):
    m, d = x.shape
    row = pl.BlockSpec((tm, d), lambda i: (i, 0))
    vec = pl.BlockSpec((1, d), lambda i: (0, 0))
    full = lambda a: pl.BlockSpec((None,) + a.shape[1:], lambda i: (layer, 0, 0))
    x_out = jax.ShapeDtypeStruct((m, d), F32)
    return pl.pallas_call(
        functools.partial(_ple_kernel, last_layer=last_layer),
        grid=(m // tm,),
        in_specs=[row, vec, pl.BlockSpec((None, tm, p.shape[2]), lambda i: (layer, i, 0)), full(wg), full(wp), vec],
        out_specs=[row] if last_layer else [row, row],
        out_shape=[x_out] if last_layer else [x_out, jax.ShapeDtypeStruct((m, d), BF16)],
        compiler_params=_cparams("parallel"),
        name="ple_gate",
    )(x, g.reshape(1, d), p, wg, wp, g_next.reshape(1, d))


def _rope_tables(positions, dim, period, pass_through):
    inv_freq = ROPE_THETA ** (-jnp.arange(0, dim, 2, dtype=F32) / dim)
    ang = positions.astype(F32).reshape(-1, 1) * inv_freq
    cos, sin = jnp.cos(ang), jnp.sin(ang)
    m, half = ang.shape
    pad = period - dim
    rest, zero = jnp.full((m, pad), pass_through, F32), jnp.zeros((m, pad), F32)
    zh = jnp.zeros((m, half), F32)
    reps = LANES // period
    c = jnp.tile(jnp.concatenate([cos, cos, rest], axis=1), (1, reps))
    s_lo = jnp.tile(jnp.concatenate([-sin, zh, zero], axis=1), (1, reps))
    s_hi = jnp.tile(jnp.concatenate([zh, sin, zero], axis=1), (1, reps))
    return c, s_lo, s_hi


def _tile(n, pref):
    t = min(n, pref)
    assert n % t == 0, (n, t)
    return t


def kernel(x, p, positions, g_mix, w_in, lambda_q1, lambda_k1, lambda_q2, lambda_k2, g_subln, g_q_latent, w_q_up, g_kv_latent, w_kv_up, w_branch_diff, w_branch_mla, w_out, g_ffn, w_gate_up, w_down, w_ple_in, g_ple, w_ple_gate, g_final):
    b, s, d = x.shape
    depth = w_in.shape[0]
    m = b * s
    tm = _tile(m, 1024)
    tm_small = _tile(m, 512)
    tq = _tile(s, 512)
    tn = 512
    assert s % KEY_CHUNK == 0

    tabs_p = _rope_tables(positions, PARTIAL_ROPE_DIM, DIFF_HEAD_DIM, 1.0)
    tabs_m = _rope_tables(positions, MLA_ROPE_DIM, LANES, 0.0)

    xf = x.reshape(m, d)
    h = _rmsnorm(xf, g_mix[0], BF16, tm_small)
    lat0 = 3 * DIFF_WIDTH
    lat_w = MLA_Q_RANK + MLA_KV_RANK + 2 * LANES
    gates0 = lat0 + MLA_Q_RANK + MLA_KV_RANK + MLA_ROPE_DIM
    assert lat0 % lat_w == 0 and gates0 - lat0 <= lat_w
    qk_dim = MLA_NOPE_DIM + MLA_ROPE_DIM
    w_main = w_in[:, :, :lat0 + lat_w].astype(BF16)
    w_gates = w_in[:, :, gates0:].astype(BF16)
    w_bd, w_bm, w_o = w_branch_diff.astype(BF16), w_branch_mla.astype(BF16), w_out.astype(BF16)
    w_gu, w_dn = w_gate_up.astype(BF16), w_down.astype(BF16)
    w_pg, w_pi = w_ple_gate.astype(BF16), w_ple_in.astype(BF16)
    pf = p.reshape(depth, m, -1)
    for i in range(depth):
        lam_init = 0.8 - 0.6 * math.exp(-0.3 * i)
        w_q = jnp.pad(w_q_up[i].reshape(MLA_Q_RANK, MLA_HEADS, qk_dim),
                      ((0, 0), (0, 0), (0, MLA_QK_PAD - qk_dim))).reshape(MLA_Q_RANK, MLA_HEADS * MLA_QK_PAD).astype(BF16)
        w_kv = w_kv_up[i].reshape(MLA_KV_RANK, MLA_HEADS, MLA_NOPE_DIM + MLA_V_DIM)
        w_k = w_kv[:, :, :MLA_NOPE_DIM].reshape(MLA_KV_RANK, MLA_HEADS * MLA_NOPE_DIM).astype(BF16)
        w_v = w_kv[:, :, MLA_NOPE_DIM:].reshape(MLA_KV_RANK, MLA_HEADS * MLA_V_DIM).astype(BF16)
        lam_params = jnp.stack([lambda_q1[i], lambda_k1[i], lambda_q2[i], lambda_k2[i]]).astype(F32)

        qkv = _qkv_proj(h, w_main, i, tabs_p, tm)
        od = _diff_attention(qkv.reshape(b, s, 3 * DIFF_WIDTH), lam_params, g_subln[i], lam_init, tq)
        qm, km, vm = _latent_proj(h, w_main, i, lat0 // lat_w, g_q_latent[i].reshape(1, -1), w_q,
                                  g_kv_latent[i].reshape(1, -1), w_k, w_v, tabs_m, tm_small)
        om = _mla_attention(qm.reshape(b, s, -1), km.reshape(b, s, -1), vm.reshape(b, s, -1), tq)
        xf = _mix_out(xf, h, od.reshape(m, -1), om.reshape(m, -1), w_gates, w_bd, w_bm, w_o, i, tm_small, tn)

        xf = _ffn(xf, g_ffn[i], w_gu, w_dn, i, tm, tn)

        last = i == depth - 1
        outs = _ple(xf, g_ple[i], pf, w_pg, w_pi, i, g_final if last else g_mix[i + 1], last, tm_small)
        if last:
            return outs[0].reshape(b, s, d)
        xf, h = outs
```

```python
import functools
import math

import jax
import jax.numpy as jnp
from jax import lax
from jax.experimental import pallas as pl
from jax.experimental.pallas import tpu as pltpu

F32 = jnp.float32
BF16 = jnp.bfloat16

DIFF_HEADS = 8
DIFF_HEAD_DIM = 64
DIFF_WIDTH = DIFF_HEADS * 2 * DIFF_HEAD_DIM
PARTIAL_ROPE_DIM = DIFF_HEAD_DIM // 4
MLA_HEADS = 8
MLA_Q_RANK = 512
MLA_KV_RANK = 256
MLA_NOPE_DIM = 128
MLA_ROPE_DIM = 64
MLA_V_DIM = 128
MLA_QK_PAD = 256
MLA_HEADS_PER_STEP = 2
ROPE_THETA = 500000.0
NORM_EPS = 1e-6
SUBLN_EPS = 1e-5
LOG2E = math.log2(math.e)
LANES = 128
VMEM_LIMIT_BYTES = 56 * 1024 * 1024
KEY_CHUNK = 256
QKV_SUBTILE = 256

def _cparams(*sem):
    return pltpu.CompilerParams(dimension_semantics=sem, vmem_limit_bytes=VMEM_LIMIT_BYTES)


def _dot(a, b):
    return jnp.dot(a, b, preferred_element_type=F32)


def _sigmoid(x):
    return 1.0 / (1.0 + jnp.exp(-x))


def _rms(x, g, eps):
    return x * lax.rsqrt(jnp.mean(x * x, axis=-1, keepdims=True) + eps) * g


def _rope_chunk(z, c, s_lo, s_hi, half):
    return z * c + pltpu.roll(z, LANES - half, 1) * s_lo + pltpu.roll(z, half, 1) * s_hi


def _rmsnorm_kernel(x_ref, g_ref, o_ref):
    o_ref[...] = _rms(x_ref[...], g_ref[...], NORM_EPS).astype(o_ref.dtype)


def _rmsnorm(x, g, out_dtype, tm):
    m, d = x.shape
    return pl.pallas_call(
        _rmsnorm_kernel,
        grid=(m // tm,),
        in_specs=[pl.BlockSpec((tm, d), lambda i: (i, 0)), pl.BlockSpec((1, d), lambda i: (0, 0))],
        out_specs=pl.BlockSpec((tm, d), lambda i: (i, 0)),
        out_shape=jax.ShapeDtypeStruct((m, d), out_dtype),
        compiler_params=_cparams("parallel"),
        name="rmsnorm",
    )(x, g.reshape(1, d))


def _qkv_kernel(h_ref, w_ref, c_ref, slo_ref, shi_ref, o_ref, *, tn, q_scale):
    j = pl.program_id(1)
    scale = jnp.where(j == 0, q_scale, 1.0)
    rotary = j < 2
    c = jnp.where(rotary, c_ref[...], 1.0) * scale
    s_lo = jnp.where(rotary, slo_ref[...], 0.0) * scale
    s_hi = jnp.where(rotary, shi_ref[...], 0.0) * scale
    h = h_ref[...]
    for u in range(tn // QKV_SUBTILE):
        z = _dot(h, w_ref[:, u * QKV_SUBTILE:(u + 1) * QKV_SUBTILE])
        for t in range(QKV_SUBTILE // LANES):
            sl = slice(u * QKV_SUBTILE + t * LANES, u * QKV_SUBTILE + (t + 1) * LANES)
            zc = z[:, t * LANES:(t + 1) * LANES]
            o_ref[:, sl] = _rope_chunk(zc, c, s_lo, s_hi, PARTIAL_ROPE_DIM // 2).astype(o_ref.dtype)


def _qkv_proj(h, w, layer, tabs, tm):
    m, k = h.shape
    tn = DIFF_WIDTH
    tab_spec = pl.BlockSpec((tm, LANES), lambda i, j: (i, 0))
    return pl.pallas_call(
        functools.partial(_qkv_kernel, tn=tn, q_scale=DIFF_HEAD_DIM ** -0.5 * LOG2E),
        grid=(m // tm, 3),
        in_specs=[pl.BlockSpec((tm, k), lambda i, j: (i, 0)), pl.BlockSpec((None, k, tn), lambda i, j: (layer, 0, j)),
                  tab_spec, tab_spec, tab_spec],
        out_specs=pl.BlockSpec((tm, tn), lambda i, j: (i, j)),
        out_shape=jax.ShapeDtypeStruct((m, 3 * tn), BF16),
        compiler_params=_cparams("parallel", "arbitrary"),
        name="qkv_proj",
    )(h, w, *tabs)


def _latent_kernel(h_ref, wl_ref, gq_ref, wq_ref, gkv_ref, wk_ref, wv_ref, c_ref, slo_ref, shi_ref,
                   q_ref, k_ref, v_ref, *, scale):
    z = _dot(h_ref[...], wl_ref[...])
    cq = _rms(z[:, :MLA_Q_RANK], gq_ref[...], NORM_EPS).astype(BF16)
    ckv = _rms(z[:, MLA_Q_RANK:MLA_Q_RANK + MLA_KV_RANK], gkv_ref[...], NORM_EPS).astype(BF16)
    kr_off = MLA_Q_RANK + MLA_KV_RANK
    c, s_lo, s_hi = c_ref[...], slo_ref[...], shi_ref[...]
    half = MLA_ROPE_DIM // 2
    kr = _rope_chunk(z[:, kr_off:kr_off + LANES], c, s_lo, s_hi, half).astype(BF16)
    qf = _dot(cq, wq_ref[...]) * scale
    kn = _dot(ckv, wk_ref[...])
    for h in range(MLA_HEADS):
        o = h * MLA_QK_PAD
        q_ref[:, o:o + LANES] = qf[:, o:o + LANES].astype(BF16)
        q_ref[:, o + LANES:o + 2 * LANES] = _rope_chunk(qf[:, o + LANES:o + 2 * LANES], c, s_lo, s_hi, half).astype(BF16)
        k_ref[:, o:o + LANES] = kn[:, h * LANES:(h + 1) * LANES].astype(BF16)
        k_ref[:, o + LANES:o + 2 * LANES] = kr
    v_ref[...] = _dot(ckv, wv_ref[...]).astype(BF16)


def _latent_proj(h, w_main, layer, lat_tile, gq, wq, gkv, wk, wv, tabs, tm):
    m, k = h.shape
    full = lambda a: pl.BlockSpec(a.shape, lambda i: (0, 0))
    tab_spec = pl.BlockSpec((tm, LANES), lambda i: (i, 0))
    nq, nv = MLA_HEADS * MLA_QK_PAD, MLA_HEADS * MLA_V_DIM
    lat_w = MLA_Q_RANK + MLA_KV_RANK + 2 * LANES
    row = lambda n: pl.BlockSpec((tm, n), lambda i: (i, 0))
    return pl.pallas_call(
        functools.partial(_latent_kernel, scale=(MLA_NOPE_DIM + MLA_ROPE_DIM) ** -0.5 * LOG2E),
        grid=(m // tm,),
        in_specs=[row(k), pl.BlockSpec((None, k, lat_w), lambda i: (layer, 0, lat_tile)), full(gq), full(wq), full(gkv),
                  full(wk), full(wv), tab_spec, tab_spec, tab_spec],
        out_specs=[row(nq), row(nq), row(nv)],
        out_shape=[jax.ShapeDtypeStruct((m, nq), BF16), jax.ShapeDtypeStruct((m, nq), BF16),
                   jax.ShapeDtypeStruct((m, nv), BF16)],
        compiler_params=_cparams("parallel"),
        name="latent_proj",
    )(h, w_main, gq, wq, gkv, wk, wv, *tabs)


def _attend_pipelined(a_chains, b_chains, s_ref, m_ref, n_keys):
    n = len(a_chains)
    nt = (((1,), (1,)), ((), ()))
    m_prev = [m_ref[u] for u in range(n)]
    acc, l, m_new = [None] * n, [None] * n, [None] * n
    for c in range(n_keys // KEY_CHUNK):
        rows = slice(c * KEY_CHUNK, (c + 1) * KEY_CHUNK)
        for u in range(n):
            p = jnp.exp2(s_ref[u, rows, :] - m_prev[u])
            lc = jnp.sum(p, axis=0, keepdims=True)
            pv = _dot(b_chains[u](rows), p.astype(BF16))
            acc[u] = pv if c == 0 else acc[u] + pv
            l[u] = lc if c == 0 else l[u] + lc
        for u, (load_k, qe) in enumerate(a_chains):
            sc = lax.dot_general(load_k(rows), qe, nt, preferred_element_type=F32)
            s_ref[u, rows, :] = sc
            mc = jnp.max(sc, axis=0, keepdims=True)
            m_new[u] = mc if c == 0 else jnp.maximum(m_new[u], mc)
    for u in range(n):
        m_ref[u] = m_new[u]
    return [acc[u] / l[u] for u in range(n)]


def _attn_scratch(n_chains, n_keys, tq, dv):
    return [pltpu.VMEM((dv, n_keys), BF16), pltpu.VMEM((n_chains, n_keys, tq), F32),
            pltpu.VMEM((n_chains, 1, tq), F32)]


def _init_attn_scratch(t, tiles_per_head, v_ref, vt_ref, s_ref, m_ref):
    @pl.when(t == 0)
    def _():
        s_ref[...] = jnp.zeros_like(s_ref)
        m_ref[...] = jnp.zeros_like(m_ref)

    @pl.when(jnp.maximum(t - 1, 0) % tiles_per_head == 0)
    def _():
        vt_ref[...] = v_ref[...].astype(F32).T.astype(BF16)


def _flat_tile(u, n_heads, tiles_per_head):
    return u // (n_heads * tiles_per_head), (u // tiles_per_head) % n_heads, u % tiles_per_head


def _diff_attn_kernel(lam_ref, g_ref, q_ref, k_ref, v_ref, o_ref, vt_ref, s_ref, m_ref, *, lam_init, tiles_per_head):
    _init_attn_scratch(pl.program_id(0), tiles_per_head, v_ref, vt_ref, s_ref, m_ref)
    lp = lam_ref[...]
    lam = (jnp.exp(jnp.sum(lp[0:1] * lp[1:2], axis=-1, keepdims=True))
           - jnp.exp(jnp.sum(lp[2:3] * lp[3:4], axis=-1, keepdims=True)) + lam_init)
    q = q_ref[...]
    lane = lax.broadcasted_iota(jnp.int32, q.shape, 1)
    zero = jnp.zeros_like(q)
    load_k = lambda rows: k_ref[rows, :]
    load_vt = lambda cols: vt_ref[:, cols]
    o1, o2 = _attend_pipelined([(load_k, jnp.where(lane < DIFF_HEAD_DIM, q, zero)),
                                (load_k, jnp.where(lane >= DIFF_HEAD_DIM, q, zero))],
                               [load_vt, load_vt], s_ref, m_ref, k_ref.shape[0])
    o = (o1 - lam * o2).T
    o_ref[...] = (_rms(o, g_ref[...], SUBLN_EPS) * (1.0 - lam_init)).astype(o_ref.dtype)


def _diff_attention(qkv, lam_params, g_subln, lam_init, tq):
    b, s, _ = qkv.shape
    dv = 2 * DIFF_HEAD_DIM
    nh, nq = DIFF_HEADS, s // tq
    n_tiles = b * nh * nq
    tile_a = lambda t: _flat_tile(jnp.minimum(t, n_tiles - 1), nh, nq)
    tile_b = lambda t: _flat_tile(jnp.maximum(t - 1, 0), nh, nq)

    def q_map(t):
        bi, h, qi = tile_a(t)
        return bi, qi, h

    def k_map(t):
        bi, h, _ = tile_a(t)
        return bi, 0, nh + h

    def v_map(t):
        bi, h, _ = tile_b(t)
        return bi, 0, 2 * nh + h

    def o_map(t):
        bi, h, qi = tile_b(t)
        return bi, qi, h

    return pl.pallas_call(
        functools.partial(_diff_attn_kernel, lam_init=lam_init, tiles_per_head=nq),
        grid=(n_tiles + 1,),
        in_specs=[pl.BlockSpec(lam_params.shape, lambda t: (0, 0)),
                  pl.BlockSpec((1, dv), lambda t: (0, 0)),
                  pl.BlockSpec((None, tq, dv), q_map),
                  pl.BlockSpec((None, s, dv), k_map),
                  pl.BlockSpec((None, s, dv), v_map)],
        out_specs=pl.BlockSpec((None, tq, dv), o_map),
        out_shape=jax.ShapeDtypeStruct((b, s, nh * dv), BF16),
        scratch_shapes=_attn_scratch(2, s, tq, dv),
        compiler_params=_cparams("arbitrary"),
        name="diff_attention",
    )(lam_params, g_subln.reshape(1, dv), qkv, qkv, qkv)


def _mla_attn_kernel(q_ref, k_ref, v_ref, o_ref, vt_ref, s_ref, m_ref, *, tiles_per_head):
    _init_attn_scratch(pl.program_id(0), tiles_per_head, v_ref, vt_ref, s_ref, m_ref)
    a_chains, b_chains = [], []
    for h in range(MLA_HEADS_PER_STEP):
        kc = slice(h * MLA_QK_PAD, (h + 1) * MLA_QK_PAD)
        vr = slice(h * MLA_V_DIM, (h + 1) * MLA_V_DIM)
        a_chains.append((lambda rows, kc=kc: k_ref[rows, kc], q_ref[:, kc]))
        b_chains.append(lambda cols, vr=vr: vt_ref[vr, cols])
    outs = _attend_pipelined(a_chains, b_chains, s_ref, m_ref, k_ref.shape[0])
    for h, o in enumerate(outs):
        o_ref[:, h * MLA_V_DIM:(h + 1) * MLA_V_DIM] = o.T.astype(o_ref.dtype)


def _mla_attention(q, k, v, tq):
    b, s, _ = q.shape
    hp = MLA_HEADS_PER_STEP
    nh, dq, dv = MLA_HEADS // hp, hp * MLA_QK_PAD, hp * MLA_V_DIM
    nq = s // tq
    n_tiles = b * nh * nq
    tile_a = lambda t: _flat_tile(jnp.minimum(t, n_tiles - 1), nh, nq)
    tile_b = lambda t: _flat_tile(jnp.maximum(t - 1, 0), nh, nq)

    def q_map(t):
        bi, h, qi = tile_a(t)
        return bi, qi, h

    def k_map(t):
        bi, h, _ = tile_a(t)
        return bi, 0, h

    def v_map(t):
        bi, h, _ = tile_b(t)
        return bi, 0, h

    def o_map(t):
        bi, h, qi = tile_b(t)
        return bi, qi, h

    return pl.pallas_call(
        functools.partial(_mla_attn_kernel, tiles_per_head=nq),
        grid=(n_tiles + 1,),
        in_specs=[pl.BlockSpec((None, tq, dq), q_map), pl.BlockSpec((None, s, dq), k_map),
                  pl.BlockSpec((None, s, dv), v_map)],
        out_specs=pl.BlockSpec((None, tq, dv), o_map),
        out_shape=jax.ShapeDtypeStruct((b, s, nh * dv), BF16),
        scratch_shapes=_attn_scratch(hp, s, tq, dv),
        compiler_params=_cparams("arbitrary"),
        name="mla_attention",
    )(q, k, v)


def _mix_out_kernel(x_ref, h_ref, od_ref, om_ref, wg_ref, wbd_ref, wbm_ref, wo_ref, xo_ref, mg_ref, *, tn):
    h, od, om = h_ref[...], od_ref[...], om_ref[...]
    n = mg_ref.shape[1]
    for t in range(n // tn):
        cols = slice(t * tn, (t + 1) * tn)
        ga = _sigmoid(_dot(h, wg_ref[:, cols]))
        gb = _sigmoid(_dot(h, wg_ref[:, n + t * tn:n + (t + 1) * tn]))
        mg_ref[:, cols] = (ga * _dot(od, wbd_ref[:, cols]) + gb * _dot(om, wbm_ref[:, cols])).astype(BF16)
    xo_ref[...] = x_ref[...] + _dot(mg_ref[...], wo_ref[...])


def _mix_out(x, h, od, om, w_gates, wbd, wbm, wo, layer, tm, tn):
    m, d = x.shape
    n = wbd.shape[2]
    row = lambda a: pl.BlockSpec((tm, a.shape[1]), lambda i: (i, 0))
    resident = lambda a: pl.BlockSpec((None,) + a.shape[1:], lambda i: (layer, 0, 0), pipeline_mode=pl.Buffered(1))
    return pl.pallas_call(
        functools.partial(_mix_out_kernel, tn=tn),
        grid=(m // tm,),
        in_specs=[row(x), row(h), row(od), row(om), resident(w_gates), resident(wbd), resident(wbm), resident(wo)],
        out_specs=row(x),
        out_shape=jax.ShapeDtypeStruct((m, d), F32),
        scratch_shapes=[pltpu.VMEM((tm, n), BF16)],
        compiler_params=_cparams("parallel"),
        name="mix_out",
    )(x, h, od, om, w_gates, wbd, wbm, wo)


def _ffn_kernel(x_ref, g_ref, wg_ref, wu_ref, wd_ref, xo_ref, h_ref):
    @pl.when(pl.program_id(1) == 0)
    def _():
        x = x_ref[...]
        xo_ref[...] = x
        h_ref[...] = _rms(x, g_ref[...], NORM_EPS).astype(h_ref.dtype)

    h = h_ref[...]
    gate = _dot(h, wg_ref[...])
    act = (gate * _sigmoid(gate) * _dot(h, wu_ref[...])).astype(BF16)
    xo_ref[...] += _dot(act, wd_ref[...])


def _ffn(x, g, w_gate_up, w_down, layer, tm, tn):
    m, d = x.shape
    n = w_down.shape[1]
    row = pl.BlockSpec((tm, d), lambda i, j: (i, 0))
    return pl.pallas_call(
        _ffn_kernel,
        grid=(m // tm, n // tn),
        in_specs=[row, pl.BlockSpec((1, d), lambda i, j: (0, 0)),
                  pl.BlockSpec((None, d, tn), lambda i, j: (layer, 0, j)),
                  pl.BlockSpec((None, d, tn), lambda i, j: (layer, 0, j + n // tn)),
                  pl.BlockSpec((None, tn, d), lambda i, j: (layer, j, 0))],
        out_specs=row,
        out_shape=jax.ShapeDtypeStruct((m, d), F32),
        scratch_shapes=[pltpu.VMEM((tm, d), BF16)],
        compiler_params=_cparams("parallel", "arbitrary"),
        name="ffn",
    )(x, g.reshape(1, d), w_gate_up, w_gate_up, w_down)


def _ple_kernel(x_ref, g_ref, p_ref, wg_ref, wp_ref, gn_ref, *out_refs, last_layer):
    x = x_ref[...]
    gate = _sigmoid(_dot(_rms(x, g_ref[...], NORM_EPS).astype(BF16), wg_ref[...]))
    x_new = x + gate * _dot(p_ref[...].astype(BF16), wp_ref[...])
    normed = _rms(x_new, gn_ref[...], NORM_EPS)
    if last_layer:
        out_refs[0][...] = normed
    else:
        out_refs[0][...] = x_new
        out_refs[1][...] = normed.astype(BF16)


def _ple(x, g, p, wg, wp, layer, g_next, last_layer, tm):
    m, d = x.shape
    row = pl.BlockSpec((tm, d), lambda i: (i, 0))
    vec = pl.BlockSpec((1, d), lambda i: (0, 0))
    full = lambda a: pl.BlockSpec((None,) + a.shape[1:], lambda i: (layer, 0, 0))
    x_out = jax.ShapeDtypeStruct((m, d), F32)
    return pl.pallas_call(
        functools.partial(_ple_kernel, last_layer=last_layer),
        grid=(m // tm,),
        in_specs=[row, vec, pl.BlockSpec((None, tm, p.shape[2]), lambda i: (layer, i, 0)), full(wg), full(wp), vec],
        out_specs=[row] if last_layer else [row, row],
        out_shape=[x_out] if last_layer else [x_out, jax.ShapeDtypeStruct((m, d), BF16)],
        compiler_params=_cparams("parallel"),
        name="ple_gate",
    )(x, g.reshape(1, d), p, wg, wp, g_next.reshape(1, d))


def _rope_tables(positions, dim, period, pass_through):
    inv_freq = ROPE_THETA ** (-jnp.arange(0, dim, 2, dtype=F32) / dim)
    ang = positions.astype(F32).reshape(-1, 1) * inv_freq
    cos, sin = jnp.cos(ang), jnp.sin(ang)
    m, half = ang.shape
    pad = period - dim
    rest, zero = jnp.full((m, pad), pass_through, F32), jnp.zeros((m, pad), F32)
    zh = jnp.zeros((m, half), F32)
    reps = LANES // period
    c = jnp.tile(jnp.concatenate([cos, cos, rest], axis=1), (1, reps))
    s_lo = jnp.tile(jnp.concatenate([-sin, zh, zero], axis=1), (1, reps))
    s_hi = jnp.tile(jnp.concatenate([zh, sin, zero], axis=1), (1, reps))
    return c, s_lo, s_hi


def _tile(n, pref):
    t = min(n, pref)
    assert n % t == 0, (n, t)
    return t


def kernel(x, p, positions, g_mix, w_in, lambda_q1, lambda_k1, lambda_q2, lambda_k2, g_subln, g_q_latent, w_q_up, g_kv_latent, w_kv_up, w_branch_diff, w_branch_mla, w_out, g_ffn, w_gate_up, w_down, w_ple_in, g_ple, w_ple_gate, g_final):
    b, s, d = x.shape
    depth = w_in.shape[0]
    m = b * s
    tm = _tile(m, 1024)
    tm_small = _tile(m, 512)
    tm_mix = _tile(m, 256)
    tq = _tile(s, 512)
    tn = 512
    assert s % KEY_CHUNK == 0

    tabs_p = _rope_tables(positions, PARTIAL_ROPE_DIM, DIFF_HEAD_DIM, 1.0)
    tabs_m = _rope_tables(positions, MLA_ROPE_DIM, LANES, 0.0)

    xf = x.reshape(m, d)
    h = _rmsnorm(xf, g_mix[0], BF16, tm_small)
    lat0 = 3 * DIFF_WIDTH
    lat_w = MLA_Q_RANK + MLA_KV_RANK + 2 * LANES
    gates0 = lat0 + MLA_Q_RANK + MLA_KV_RANK + MLA_ROPE_DIM
    assert lat0 % lat_w == 0 and gates0 - lat0 <= lat_w
    qk_dim = MLA_NOPE_DIM + MLA_ROPE_DIM
    w_main = w_in[:, :, :lat0 + lat_w].astype(BF16)
    w_gates = w_in[:, :, gates0:].astype(BF16)
    w_bd, w_bm, w_o = w_branch_diff.astype(BF16), w_branch_mla.astype(BF16), w_out.astype(BF16)
    w_gu, w_dn = w_gate_up.astype(BF16), w_down.astype(BF16)
    w_pg, w_pi = w_ple_gate.astype(BF16), w_ple_in.astype(BF16)
    pf = p.reshape(depth, m, -1)
    for i in range(depth):
        lam_init = 0.8 - 0.6 * math.exp(-0.3 * i)
        w_q = jnp.pad(w_q_up[i].reshape(MLA_Q_RANK, MLA_HEADS, qk_dim),
                      ((0, 0), (0, 0), (0, MLA_QK_PAD - qk_dim))).reshape(MLA_Q_RANK, MLA_HEADS * MLA_QK_PAD).astype(BF16)
        w_kv = w_kv_up[i].reshape(MLA_KV_RANK, MLA_HEADS, MLA_NOPE_DIM + MLA_V_DIM)
        w_k = w_kv[:, :, :MLA_NOPE_DIM].reshape(MLA_KV_RANK, MLA_HEADS * MLA_NOPE_DIM).astype(BF16)
        w_v = w_kv[:, :, MLA_NOPE_DIM:].reshape(MLA_KV_RANK, MLA_HEADS * MLA_V_DIM).astype(BF16)
        lam_params = jnp.stack([lambda_q1[i], lambda_k1[i], lambda_q2[i], lambda_k2[i]]).astype(F32)

        qkv = _qkv_proj(h, w_main, i, tabs_p, tm)
        od = _diff_attention(qkv.reshape(b, s, 3 * DIFF_WIDTH), lam_params, g_subln[i], lam_init, tq)
        qm, km, vm = _latent_proj(h, w_main, i, lat0 // lat_w, g_q_latent[i].reshape(1, -1), w_q,
                                  g_kv_latent[i].reshape(1, -1), w_k, w_v, tabs_m, tm_small)
        om = _mla_attention(qm.reshape(b, s, -1), km.reshape(b, s, -1), vm.reshape(b, s, -1), tq)
        xf = _mix_out(xf, h, od.reshape(m, -1), om.reshape(m, -1), w_gates, w_bd, w_bm, w_o, i, tm_mix, tn)

        xf = _ffn(xf, g_ffn[i], w_gu, w_dn, i, tm, tn)

        last = i == depth - 1
        outs = _ple(xf, g_ple[i], pf, w_pg, w_pi, i, g_final if last else g_mix[i + 1], last, tm_small)
        if last:
            return outs[0].reshape(b, s, d)
        xf, h = outs
```

```python
import functools
import math

import jax
import jax.numpy as jnp
from jax import lax
from jax.experimental import pallas as pl
from jax.experimental.pallas import tpu as pltpu

F32 = jnp.float32
BF16 = jnp.bfloat16

DIFF_HEADS = 8
DIFF_HEAD_DIM = 64
DIFF_WIDTH = DIFF_HEADS * 2 * DIFF_HEAD_DIM
PARTIAL_ROPE_DIM = DIFF_HEAD_DIM // 4
MLA_HEADS = 8
MLA_Q_RANK = 512
MLA_KV_RANK = 256
MLA_NOPE_DIM = 128
MLA_ROPE_DIM = 64
MLA_V_DIM = 128
MLA_QK_PAD = 256
MLA_HEADS_PER_STEP = 2
ROPE_THETA = 500000.0
NORM_EPS = 1e-6
SUBLN_EPS = 1e-5
LOG2E = math.log2(math.e)
LANES = 128
VMEM_LIMIT_BYTES = 56 * 1024 * 1024
KEY_CHUNK = 256
QKV_SUBTILE = 256

def _cparams(*sem):
    return pltpu.CompilerParams(dimension_semantics=sem, vmem_limit_bytes=VMEM_LIMIT_BYTES)


def _dot(a, b):
    return jnp.dot(a, b, preferred_element_type=F32)


def _sigmoid(x):
    return 1.0 / (1.0 + jnp.exp(-x))


def _rms(x, g, eps):
    return x * lax.rsqrt(jnp.mean(x * x, axis=-1, keepdims=True) + eps) * g


def _rope_chunk(z, c, s_lo, s_hi, half):
    return z * c + pltpu.roll(z, LANES - half, 1) * s_lo + pltpu.roll(z, half, 1) * s_hi


def _rmsnorm_kernel(x_ref, g_ref, o_ref):
    o_ref[...] = _rms(x_ref[...], g_ref[...], NORM_EPS).astype(o_ref.dtype)


def _rmsnorm(x, g, out_dtype, tm):
    m, d = x.shape
    return pl.pallas_call(
        _rmsnorm_kernel,
        grid=(m // tm,),
        in_specs=[pl.BlockSpec((tm, d), lambda i: (i, 0)), pl.BlockSpec((1, d), lambda i: (0, 0))],
        out_specs=pl.BlockSpec((tm, d), lambda i: (i, 0)),
        out_shape=jax.ShapeDtypeStruct((m, d), out_dtype),
        compiler_params=_cparams("parallel"),
        name="rmsnorm",
    )(x, g.reshape(1, d))


LATENT_TILE = MLA_Q_RANK + MLA_KV_RANK + 2 * LANES


def _in_proj_kernel(h_ref, w_ref, gq_ref, wq_ref, gkv_ref, wk_ref, wv_ref, cp_ref, slop_ref, ship_ref,
                    cm_ref, slom_ref, shim_ref, qkv_ref, q_ref, k_ref, v_ref, *, q_scale, mla_scale):
    h = h_ref[...]
    plain = (cp_ref[...], slop_ref[...], ship_ref[...])
    scaled = tuple(t * q_scale for t in plain)
    for u in range(3 * DIFF_WIDTH // QKV_SUBTILE):
        z = _dot(h, w_ref[:, u * QKV_SUBTILE:(u + 1) * QKV_SUBTILE])
        part = u * QKV_SUBTILE // DIFF_WIDTH
        for t in range(QKV_SUBTILE // LANES):
            sl = slice(u * QKV_SUBTILE + t * LANES, u * QKV_SUBTILE + (t + 1) * LANES)
            zc = z[:, t * LANES:(t + 1) * LANES]
            if part < 2:
                zc = _rope_chunk(zc, *(scaled if part == 0 else plain), PARTIAL_ROPE_DIM // 2)
            qkv_ref[:, sl] = zc.astype(qkv_ref.dtype)

    lat0 = 3 * DIFF_WIDTH
    z = _dot(h, w_ref[:, lat0:lat0 + LATENT_TILE])
    cq = _rms(z[:, :MLA_Q_RANK], gq_ref[...], NORM_EPS).astype(BF16)
    ckv = _rms(z[:, MLA_Q_RANK:MLA_Q_RANK + MLA_KV_RANK], gkv_ref[...], NORM_EPS).astype(BF16)
    kr_off = MLA_Q_RANK + MLA_KV_RANK
    c, s_lo, s_hi = cm_ref[...], slom_ref[...], shim_ref[...]
    half = MLA_ROPE_DIM // 2
    kr = _rope_chunk(z[:, kr_off:kr_off + LANES], c, s_lo, s_hi, half).astype(BF16)
    qf = _dot(cq, wq_ref[...]) * mla_scale
    kn = _dot(ckv, wk_ref[...])
    for hd in range(MLA_HEADS):
        o = hd * MLA_QK_PAD
        q_ref[:, o:o + LANES] = qf[:, o:o + LANES].astype(BF16)
        q_ref[:, o + LANES:o + 2 * LANES] = _rope_chunk(qf[:, o + LANES:o + 2 * LANES], c, s_lo, s_hi, half).astype(BF16)
        k_ref[:, o:o + LANES] = kn[:, hd * LANES:(hd + 1) * LANES].astype(BF16)
        k_ref[:, o + LANES:o + 2 * LANES] = kr
    v_ref[...] = _dot(ckv, wv_ref[...]).astype(BF16)


def _in_proj(h, w_main, layer, gq, wq, gkv, wk, wv, tabs_p, tabs_m, tm):
    m, k = h.shape
    once = lambda a: pl.BlockSpec(a.shape, lambda i: (0,) * a.ndim, pipeline_mode=pl.Buffered(1))
    tab_spec = pl.BlockSpec((tm, LANES), lambda i: (i, 0))
    nqkv, nq, nv = 3 * DIFF_WIDTH, MLA_HEADS * MLA_QK_PAD, MLA_HEADS * MLA_V_DIM
    row = lambda n: pl.BlockSpec((tm, n), lambda i: (i, 0))
    return pl.pallas_call(
        functools.partial(_in_proj_kernel, q_scale=DIFF_HEAD_DIM ** -0.5 * LOG2E,
                          mla_scale=(MLA_NOPE_DIM + MLA_ROPE_DIM) ** -0.5 * LOG2E),
        grid=(m // tm,),
        in_specs=[row(k), pl.BlockSpec((None,) + w_main.shape[1:], lambda i: (layer, 0, 0), pipeline_mode=pl.Buffered(1)),
                  once(gq), once(wq), once(gkv), once(wk), once(wv)] + [tab_spec] * 6,
        out_specs=[row(nqkv), row(nq), row(nq), row(nv)],
        out_shape=[jax.ShapeDtypeStruct((m, nqkv), BF16), jax.ShapeDtypeStruct((m, nq), BF16),
                   jax.ShapeDtypeStruct((m, nq), BF16), jax.ShapeDtypeStruct((m, nv), BF16)],
        compiler_params=_cparams("parallel"),
        name="in_proj",
    )(h, w_main, gq, wq, gkv, wk, wv, *tabs_p, *tabs_m)


def _attend_pipelined(a_chains, b_chains, s_ref, m_ref, n_keys):
    n = len(a_chains)
    nt = (((1,), (1,)), ((), ()))
    m_prev = [m_ref[u] for u in range(n)]
    acc, l, m_new = [None] * n, [None] * n, [None] * n
    for c in range(n_keys // KEY_CHUNK):
        rows = slice(c * KEY_CHUNK, (c + 1) * KEY_CHUNK)
        for u in range(n):
            p = jnp.exp2(s_ref[u, rows, :] - m_prev[u])
            lc = jnp.sum(p, axis=0, keepdims=True)
            pv = _dot(b_chains[u](rows), p.astype(BF16))
            acc[u] = pv if c == 0 else acc[u] + pv
            l[u] = lc if c == 0 else l[u] + lc
        for u, (load_k, qe) in enumerate(a_chains):
            sc = lax.dot_general(load_k(rows), qe, nt, preferred_element_type=F32)
            s_ref[u, rows, :] = sc
            mc = jnp.max(sc, axis=0, keepdims=True)
            m_new[u] = mc if c == 0 else jnp.maximum(m_new[u], mc)
    for u in range(n):
        m_ref[u] = m_new[u]
    return [acc[u] / l[u] for u in range(n)]


def _attn_scratch(n_chains, n_keys, tq, dv):
    return [pltpu.VMEM((dv, n_keys), BF16), pltpu.VMEM((n_chains, n_keys, tq), F32),
            pltpu.VMEM((n_chains, 1, tq), F32)]


def _init_attn_scratch(t, tiles_per_head, v_ref, vt_ref, s_ref, m_ref):
    @pl.when(t == 0)
    def _():
        s_ref[...] = jnp.zeros_like(s_ref)
        m_ref[...] = jnp.zeros_like(m_ref)

    @pl.when(jnp.maximum(t - 1, 0) % tiles_per_head == 0)
    def _():
        vt_ref[...] = v_ref[...].astype(F32).T.astype(BF16)


def _flat_tile(u, n_heads, tiles_per_head):
    return u // (n_heads * tiles_per_head), (u // tiles_per_head) % n_heads, u % tiles_per_head


def _diff_attn_kernel(lam_ref, g_ref, q_ref, k_ref, v_ref, o_ref, vt_ref, s_ref, m_ref, *, lam_init, tiles_per_head):
    _init_attn_scratch(pl.program_id(0), tiles_per_head, v_ref, vt_ref, s_ref, m_ref)
    lp = lam_ref[...]
    lam = (jnp.exp(jnp.sum(lp[0:1] * lp[1:2], axis=-1, keepdims=True))
           - jnp.exp(jnp.sum(lp[2:3] * lp[3:4], axis=-1, keepdims=True)) + lam_init)
    q = q_ref[...]
    lane = lax.broadcasted_iota(jnp.int32, q.shape, 1)
    zero = jnp.zeros_like(q)
    load_k = lambda rows: k_ref[rows, :]
    load_vt = lambda cols: vt_ref[:, cols]
    o1, o2 = _attend_pipelined([(load_k, jnp.where(lane < DIFF_HEAD_DIM, q, zero)),
                                (load_k, jnp.where(lane >= DIFF_HEAD_DIM, q, zero))],
                               [load_vt, load_vt], s_ref, m_ref, k_ref.shape[0])
    o = (o1 - lam * o2).T
    o_ref[...] = (_rms(o, g_ref[...], SUBLN_EPS) * (1.0 - lam_init)).astype(o_ref.dtype)


def _diff_attention(qkv, lam_params, g_subln, lam_init, tq):
    b, s, _ = qkv.shape
    dv = 2 * DIFF_HEAD_DIM
    nh, nq = DIFF_HEADS, s // tq
    n_tiles = b * nh * nq
    tile_a = lambda t: _flat_tile(jnp.minimum(t, n_tiles - 1), nh, nq)
    tile_b = lambda t: _flat_tile(jnp.maximum(t - 1, 0), nh, nq)

    def q_map(t):
        bi, h, qi = tile_a(t)
        return bi, qi, h

    def k_map(t):
        bi, h, _ = tile_a(t)
        return bi, 0, nh + h

    def v_map(t):
        bi, h, _ = tile_b(t)
        return bi, 0, 2 * nh + h

    def o_map(t):
        bi, h, qi = tile_b(t)
        return bi, qi, h

    return pl.pallas_call(
        functools.partial(_diff_attn_kernel, lam_init=lam_init, tiles_per_head=nq),
        grid=(n_tiles + 1,),
        in_specs=[pl.BlockSpec(lam_params.shape, lambda t: (0, 0)),
                  pl.BlockSpec((1, dv), lambda t: (0, 0)),
                  pl.BlockSpec((None, tq, dv), q_map),
                  pl.BlockSpec((None, s, dv), k_map),
                  pl.BlockSpec((None, s, dv), v_map)],
        out_specs=pl.BlockSpec((None, tq, dv), o_map),
        out_shape=jax.ShapeDtypeStruct((b, s, nh * dv), BF16),
        scratch_shapes=_attn_scratch(2, s, tq, dv),
        compiler_params=_cparams("arbitrary"),
        name="diff_attention",
    )(lam_params, g_subln.reshape(1, dv), qkv, qkv, qkv)


def _mla_attn_kernel(q_ref, k_ref, v_ref, o_ref, vt_ref, s_ref, m_ref, *, tiles_per_head):
    _init_attn_scratch(pl.program_id(0), tiles_per_head, v_ref, vt_ref, s_ref, m_ref)
    a_chains, b_chains = [], []
    for h in range(MLA_HEADS_PER_STEP):
        kc = slice(h * MLA_QK_PAD, (h + 1) * MLA_QK_PAD)
        vr = slice(h * MLA_V_DIM, (h + 1) * MLA_V_DIM)
        a_chains.append((lambda rows, kc=kc: k_ref[rows, kc], q_ref[:, kc]))
        b_chains.append(lambda cols, vr=vr: vt_ref[vr, cols])
    outs = _attend_pipelined(a_chains, b_chains, s_ref, m_ref, k_ref.shape[0])
    for h, o in enumerate(outs):
        o_ref[:, h * MLA_V_DIM:(h + 1) * MLA_V_DIM] = o.T.astype(o_ref.dtype)


def _mla_attention(q, k, v, tq):
    b, s, _ = q.shape
    hp = MLA_HEADS_PER_STEP
    nh, dq, dv = MLA_HEADS // hp, hp * MLA_QK_PAD, hp * MLA_V_DIM
    nq = s // tq
    n_tiles = b * nh * nq
    tile_a = lambda t: _flat_tile(jnp.minimum(t, n_tiles - 1), nh, nq)
    tile_b = lambda t: _flat_tile(jnp.maximum(t - 1, 0), nh, nq)

    def q_map(t):
        bi, h, qi = tile_a(t)
        return bi, qi, h

    def k_map(t):
        bi, h, _ = tile_a(t)
        return bi, 0, h

    def v_map(t):
        bi, h, _ = tile_b(t)
        return bi, 0, h

    def o_map(t):
        bi, h, qi = tile_b(t)
        return bi, qi, h

    return pl.pallas_call(
        functools.partial(_mla_attn_kernel, tiles_per_head=nq),
        grid=(n_tiles + 1,),
        in_specs=[pl.BlockSpec((None, tq, dq), q_map), pl.BlockSpec((None, s, dq), k_map),
                  pl.BlockSpec((None, s, dv), v_map)],
        out_specs=pl.BlockSpec((None, tq, dv), o_map),
        out_shape=jax.ShapeDtypeStruct((b, s, nh * dv), BF16),
        scratch_shapes=_attn_scratch(hp, s, tq, dv),
        compiler_params=_cparams("arbitrary"),
        name="mla_attention",
    )(q, k, v)


def _mix_out_kernel(x_ref, h_ref, od_ref, om_ref, wg_ref, wbd_ref, wbm_ref, wo_ref, xo_ref, mg_ref, *, tn):
    h, od, om = h_ref[...], od_ref[...], om_ref[...]
    n = mg_ref.shape[1]
    for t in range(n // tn):
        cols = slice(t * tn, (t + 1) * tn)
        ga = _sigmoid(_dot(h, wg_ref[:, cols]))
        gb = _sigmoid(_dot(h, wg_ref[:, n + t * tn:n + (t + 1) * tn]))
        mg_ref[:, cols] = (ga * _dot(od, wbd_ref[:, cols]) + gb * _dot(om, wbm_ref[:, cols])).astype(BF16)
    xo_ref[...] = x_ref[...] + _dot(mg_ref[...], wo_ref[...])


def _mix_out(x, h, od, om, w_gates, wbd, wbm, wo, layer, tm, tn):
    m, d = x.shape
    n = wbd.shape[2]
    row = lambda a: pl.BlockSpec((tm, a.shape[1]), lambda i: (i, 0))
    resident = lambda a: pl.BlockSpec((None,) + a.shape[1:], lambda i: (layer, 0, 0), pipeline_mode=pl.Buffered(1))
    return pl.pallas_call(
        functools.partial(_mix_out_kernel, tn=tn),
        grid=(m // tm,),
        in_specs=[row(x), row(h), row(od), row(om), resident(w_gates), resident(wbd), resident(wbm), resident(wo)],
        out_specs=row(x),
        out_shape=jax.ShapeDtypeStruct((m, d), F32),
        scratch_shapes=[pltpu.VMEM((tm, n), BF16)],
        compiler_params=_cparams("parallel"),
        name="mix_out",
    )(x, h, od, om, w_gates, wbd, wbm, wo)


def _ffn_kernel(x_ref, g_ref, wg_ref, wu_ref, wd_ref, xo_ref, h_ref):
    @pl.when(pl.program_id(1) == 0)
    def _():
        x = x_ref[...]
        xo_ref[...] = x
        h_ref[...] = _rms(x, g_ref[...], NORM_EPS).astype(h_ref.dtype)

    h = h_ref[...]
    gate = _dot(h, wg_ref[...])
    act = (gate * _sigmoid(gate) * _dot(h, wu_ref[...])).astype(BF16)
    xo_ref[...] += _dot(act, wd_ref[...])


def _ffn(x, g, w_gate_up, w_down, layer, tm, tn):
    m, d = x.shape
    n = w_down.shape[1]
    row = pl.BlockSpec((tm, d), lambda i, j: (i, 0))
    return pl.pallas_call(
        _ffn_kernel,
        grid=(m // tm, n // tn),
        in_specs=[row, pl.BlockSpec((1, d), lambda i, j: (0, 0)),
                  pl.BlockSpec((None, d, tn), lambda i, j: (layer, 0, j)),
                  pl.BlockSpec((None, d, tn), lambda i, j: (layer, 0, j + n // tn)),
                  pl.BlockSpec((None, tn, d), lambda i, j: (layer, j, 0))],
        out_specs=row,
        out_shape=jax.ShapeDtypeStruct((m, d), F32),
        scratch_shapes=[pltpu.VMEM((tm, d), BF16)],
        compiler_params=_cparams("parallel", "arbitrary"),
        name="ffn",
    )(x, g.reshape(1, d), w_gate_up, w_gate_up, w_down)


def _ple_kernel(x_ref, g_ref, p_ref, wg_ref, wp_ref, gn_ref, *out_refs, last_layer):
    x = x_ref[...]
    gate = _sigmoid(_dot(_rms(x, g_ref[...], NORM_EPS).astype(BF16), wg_ref[...]))
    x_new = x + gate * _dot(p_ref[...].astype(BF16), wp_ref[...])
    normed = _rms(x_new, gn_ref[...], NORM_EPS)
    if last_layer:
        out_refs[0][...] = normed
    else:
        out_refs[0][...] = x_new
        out_refs[1][...] = normed.astype(BF16)


def _ple(x, g, p, wg, wp, layer, g_next, last_layer, tm):
    m, d = x.shape
    row = pl.BlockSpec((tm, d), lambda i: (i, 0))
    vec = pl.BlockSpec((1, d), lambda i: (0, 0))
    full = lambda a: pl.BlockSpec((None,) + a.shape[1:], lambda i: (layer, 0, 0))
    x_out = jax.ShapeDtypeStruct((m, d), F32)
    return pl.pallas_call(
        functools.partial(_ple_kernel, last_layer=last_layer),
        grid=(m // tm,),
        in_specs=[row, vec, pl.BlockSpec((None, tm, p.shape[2]), lambda i: (layer, i, 0)), full(wg), full(wp), vec],
        out_specs=[row] if last_layer else [row, row],
        out_shape=[x_out] if last_layer else [x_out, jax.ShapeDtypeStruct((m, d), BF16)],
        compiler_params=_cparams("parallel"),
        name="ple_gate",
    )(x, g.reshape(1, d), p, wg, wp, g_next.reshape(1, d))


def _rope_tables(positions, dim, period, pass_through):
    inv_freq = ROPE_THETA ** (-jnp.arange(0, dim, 2, dtype=F32) / dim)
    ang = positions.astype(F32).reshape(-1, 1) * inv_freq
    cos, sin = jnp.cos(ang), jnp.sin(ang)
    m, half = ang.shape
    pad = period - dim
    rest, zero = jnp.full((m, pad), pass_through, F32), jnp.zeros((m, pad), F32)
    zh = jnp.zeros((m, half), F32)
    reps = LANES // period
    c = jnp.tile(jnp.concatenate([cos, cos, rest], axis=1), (1, reps))
    s_lo = jnp.tile(jnp.concatenate([-sin, zh, zero], axis=1), (1, reps))
    s_hi = jnp.tile(jnp.concatenate([zh, sin, zero], axis=1), (1, reps))
    return c, s_lo, s_hi


def _tile(n, pref):
    t = min(n, pref)
    assert n % t == 0, (n, t)
    return t


def kernel(x, p, positions, g_mix, w_in, lambda_q1, lambda_k1, lambda_q2, lambda_k2, g_subln, g_q_latent, w_q_up, g_kv_latent, w_kv_up, w_branch_diff, w_branch_mla, w_out, g_ffn, w_gate_up, w_down, w_ple_in, g_ple, w_ple_gate, g_final):
    b, s, d = x.shape
    depth = w_in.shape[0]
    m = b * s
    tm = _tile(m, 1024)
    tm_small = _tile(m, 512)
    tm_mix = _tile(m, 256)
    tq = _tile(s, 512)
    tn = 512
    assert s % KEY_CHUNK == 0

    tabs_p = _rope_tables(positions, PARTIAL_ROPE_DIM, DIFF_HEAD_DIM, 1.0)
    tabs_m = _rope_tables(positions, MLA_ROPE_DIM, LANES, 0.0)

    xf = x.reshape(m, d)
    h = _rmsnorm(xf, g_mix[0], BF16, tm_small)
    lat0 = 3 * DIFF_WIDTH
    gates0 = lat0 + MLA_Q_RANK + MLA_KV_RANK + MLA_ROPE_DIM
    assert gates0 - lat0 <= LATENT_TILE
    qk_dim = MLA_NOPE_DIM + MLA_ROPE_DIM
    w_main = w_in[:, :, :lat0 + LATENT_TILE].astype(BF16)
    w_gates = w_in[:, :, gates0:].astype(BF16)
    w_bd, w_bm, w_o = w_branch_diff.astype(BF16), w_branch_mla.astype(BF16), w_out.astype(BF16)
    w_gu, w_dn = w_gate_up.astype(BF16), w_down.astype(BF16)
    w_pg, w_pi = w_ple_gate.astype(BF16), w_ple_in.astype(BF16)
    pf = p.reshape(depth, m, -1)
    for i in range(depth):
        lam_init = 0.8 - 0.6 * math.exp(-0.3 * i)
        w_q = jnp.pad(w_q_up[i].reshape(MLA_Q_RANK, MLA_HEADS, qk_dim),
                      ((0, 0), (0, 0), (0, MLA_QK_PAD - qk_dim))).reshape(MLA_Q_RANK, MLA_HEADS * MLA_QK_PAD).astype(BF16)
        w_kv = w_kv_up[i].reshape(MLA_KV_RANK, MLA_HEADS, MLA_NOPE_DIM + MLA_V_DIM)
        w_k = w_kv[:, :, :MLA_NOPE_DIM].reshape(MLA_KV_RANK, MLA_HEADS * MLA_NOPE_DIM).astype(BF16)
        w_v = w_kv[:, :, MLA_NOPE_DIM:].reshape(MLA_KV_RANK, MLA_HEADS * MLA_V_DIM).astype(BF16)
        lam_params = jnp.stack([lambda_q1[i], lambda_k1[i], lambda_q2[i], lambda_k2[i]]).astype(F32)

        qkv, qm, km, vm = _in_proj(h, w_main, i, g_q_latent[i].reshape(1, -1), w_q, g_kv_latent[i].reshape(1, -1),
                                   w_k, w_v, tabs_p, tabs_m, tm_small)
        od = _diff_attention(qkv.reshape(b, s, 3 * DIFF_WIDTH), lam_params, g_subln[i], lam_init, tq)
        om = _mla_attention(qm.reshape(b, s, -1), km.reshape(b, s, -1), vm.reshape(b, s, -1), tq)
        xf = _mix_out(xf, h, od.reshape(m, -1), om.reshape(m, -1), w_gates, w_bd, w_bm, w_o, i, tm_mix, tn)

        xf = _ffn(xf, g_ffn[i], w_gu, w_dn, i, tm, tn)

        last = i == depth - 1
        outs = _ple(xf, g_ple[i], pf, w_pg, w_pi, i, g_final if last else g_mix[i + 1], last, tm_small)
        if last:
            return outs[0].reshape(b, s, d)
        xf, h = outs
```

```python
import functools
import math

import jax
import jax.numpy as jnp
from jax import lax
from jax.experimental import pallas as pl
from jax.experimental.pallas import tpu as pltpu

F32 = jnp.float32
BF16 = jnp.bfloat16

DIFF_HEADS = 8
DIFF_HEAD_DIM = 64
DIFF_WIDTH = DIFF_HEADS * 2 * DIFF_HEAD_DIM
PARTIAL_ROPE_DIM = DIFF_HEAD_DIM // 4
MLA_HEADS = 8
MLA_Q_RANK = 512
MLA_KV_RANK = 256
MLA_NOPE_DIM = 128
MLA_ROPE_DIM = 64
MLA_V_DIM = 128
MLA_QK_PAD = 256
MLA_HEADS_PER_STEP = 2
ROPE_THETA = 500000.0
NORM_EPS = 1e-6
SUBLN_EPS = 1e-5
LOG2E = math.log2(math.e)
LANES = 128
VMEM_LIMIT_BYTES = 56 * 1024 * 1024
KEY_CHUNK = 256
QKV_SUBTILE = 256

def _cparams(*sem):
    return pltpu.CompilerParams(dimension_semantics=sem, vmem_limit_bytes=VMEM_LIMIT_BYTES)


def _dot(a, b):
    return jnp.dot(a, b, preferred_element_type=F32)


def _sigmoid(x):
    return 1.0 / (1.0 + jnp.exp(-x))


def _rms(x, g, eps):
    return x * lax.rsqrt(jnp.mean(x * x, axis=-1, keepdims=True) + eps) * g


def _rope_chunk(z, c, s_lo, s_hi, half):
    return z * c + pltpu.roll(z, LANES - half, 1) * s_lo + pltpu.roll(z, half, 1) * s_hi


LATENT_TILE = MLA_Q_RANK + MLA_KV_RANK + 2 * LANES


def _in_proj_kernel(x_ref, g_ref, w_ref, gq_ref, wq_ref, gkv_ref, wk_ref, wv_ref, cp_ref, slop_ref, ship_ref,
                    cm_ref, slom_ref, shim_ref, qkv_ref, q_ref, k_ref, v_ref, *, q_scale, mla_scale):
    h = _rms(x_ref[...], g_ref[...], NORM_EPS).astype(BF16)
    plain = (cp_ref[...], slop_ref[...], ship_ref[...])
    scaled = tuple(t * q_scale for t in plain)
    for u in range(3 * DIFF_WIDTH // QKV_SUBTILE):
        z = _dot(h, w_ref[:, u * QKV_SUBTILE:(u + 1) * QKV_SUBTILE])
        part = u * QKV_SUBTILE // DIFF_WIDTH
        for t in range(QKV_SUBTILE // LANES):
            sl = slice(u * QKV_SUBTILE + t * LANES, u * QKV_SUBTILE + (t + 1) * LANES)
            zc = z[:, t * LANES:(t + 1) * LANES]
            if part < 2:
                zc = _rope_chunk(zc, *(scaled if part == 0 else plain), PARTIAL_ROPE_DIM // 2)
            qkv_ref[:, sl] = zc.astype(qkv_ref.dtype)

    lat0 = 3 * DIFF_WIDTH
    z = _dot(h, w_ref[:, lat0:lat0 + LATENT_TILE])
    cq = _rms(z[:, :MLA_Q_RANK], gq_ref[...], NORM_EPS).astype(BF16)
    ckv = _rms(z[:, MLA_Q_RANK:MLA_Q_RANK + MLA_KV_RANK], gkv_ref[...], NORM_EPS).astype(BF16)
    kr_off = MLA_Q_RANK + MLA_KV_RANK
    c, s_lo, s_hi = cm_ref[...], slom_ref[...], shim_ref[...]
    half = MLA_ROPE_DIM // 2
    kr = _rope_chunk(z[:, kr_off:kr_off + LANES], c, s_lo, s_hi, half).astype(BF16)
    qf = _dot(cq, wq_ref[...]) * mla_scale
    kn = _dot(ckv, wk_ref[...])
    for hd in range(MLA_HEADS):
        o = hd * MLA_QK_PAD
        q_ref[:, o:o + LANES] = qf[:, o:o + LANES].astype(BF16)
        q_ref[:, o + LANES:o + 2 * LANES] = _rope_chunk(qf[:, o + LANES:o + 2 * LANES], c, s_lo, s_hi, half).astype(BF16)
        k_ref[:, o:o + LANES] = kn[:, hd * LANES:(hd + 1) * LANES].astype(BF16)
        k_ref[:, o + LANES:o + 2 * LANES] = kr
    v_ref[...] = _dot(ckv, wv_ref[...]).astype(BF16)


def _in_proj(x, g, w_main, layer, gq, wq, gkv, wk, wv, tabs_p, tabs_m, tm):
    m, k = x.shape
    g = g.reshape(1, k)
    once = lambda a: pl.BlockSpec(a.shape, lambda i: (0,) * a.ndim, pipeline_mode=pl.Buffered(1))
    tab_spec = pl.BlockSpec((tm, LANES), lambda i: (i, 0))
    nqkv, nq, nv = 3 * DIFF_WIDTH, MLA_HEADS * MLA_QK_PAD, MLA_HEADS * MLA_V_DIM
    row = lambda n: pl.BlockSpec((tm, n), lambda i: (i, 0))
    return pl.pallas_call(
        functools.partial(_in_proj_kernel, q_scale=DIFF_HEAD_DIM ** -0.5 * LOG2E,
                          mla_scale=(MLA_NOPE_DIM + MLA_ROPE_DIM) ** -0.5 * LOG2E),
        grid=(m // tm,),
        in_specs=[row(k), once(g),
                  pl.BlockSpec((None,) + w_main.shape[1:], lambda i: (layer, 0, 0), pipeline_mode=pl.Buffered(1)),
                  once(gq), once(wq), once(gkv), once(wk), once(wv)] + [tab_spec] * 6,
        out_specs=[row(nqkv), row(nq), row(nq), row(nv)],
        out_shape=[jax.ShapeDtypeStruct((m, nqkv), BF16), jax.ShapeDtypeStruct((m, nq), BF16),
                   jax.ShapeDtypeStruct((m, nq), BF16), jax.ShapeDtypeStruct((m, nv), BF16)],
        compiler_params=_cparams("parallel"),
        name="in_proj",
    )(x, g, w_main, gq, wq, gkv, wk, wv, *tabs_p, *tabs_m)


def _attend_pipelined(a_chains, b_chains, s_ref, m_ref, n_keys):
    n = len(a_chains)
    nt = (((1,), (1,)), ((), ()))
    m_prev = [m_ref[u] for u in range(n)]
    acc, l, m_new = [None] * n, [None] * n, [None] * n
    for c in range(n_keys // KEY_CHUNK):
        rows = slice(c * KEY_CHUNK, (c + 1) * KEY_CHUNK)
        for u in range(n):
            p = jnp.exp2(s_ref[u, rows, :] - m_prev[u])
            lc = jnp.sum(p, axis=0, keepdims=True)
            pv = _dot(b_chains[u](rows), p.astype(BF16))
            acc[u] = pv if c == 0 else acc[u] + pv
            l[u] = lc if c == 0 else l[u] + lc
        for u, (load_k, qe) in enumerate(a_chains):
            sc = lax.dot_general(load_k(rows), qe, nt, preferred_element_type=F32)
            s_ref[u, rows, :] = sc
            mc = jnp.max(sc, axis=0, keepdims=True)
            m_new[u] = mc if c == 0 else jnp.maximum(m_new[u], mc)
    for u in range(n):
        m_ref[u] = m_new[u]
    return [acc[u] / l[u] for u in range(n)]


def _attn_scratch(n_chains, n_keys, tq, dv):
    return [pltpu.VMEM((dv, n_keys), BF16), pltpu.VMEM((n_chains, n_keys, tq), F32),
            pltpu.VMEM((n_chains, 1, tq), F32)]


def _init_attn_scratch(t, tiles_per_head, v_ref, vt_ref, s_ref, m_ref):
    @pl.when(t == 0)
    def _():
        s_ref[...] = jnp.zeros_like(s_ref)
        m_ref[...] = jnp.zeros_like(m_ref)

    @pl.when(jnp.maximum(t - 1, 0) % tiles_per_head == 0)
    def _():
        vt_ref[...] = v_ref[...].astype(F32).T.astype(BF16)


def _flat_tile(u, n_heads, tiles_per_head):
    return u // (n_heads * tiles_per_head), (u // tiles_per_head) % n_heads, u % tiles_per_head


def _diff_attn_kernel(lam_ref, g_ref, q_ref, k_ref, v_ref, o_ref, vt_ref, s_ref, m_ref, *, lam_init, tiles_per_head):
    _init_attn_scratch(pl.program_id(0), tiles_per_head, v_ref, vt_ref, s_ref, m_ref)
    lp = lam_ref[...]
    lam = (jnp.exp(jnp.sum(lp[0:1] * lp[1:2], axis=-1, keepdims=True))
           - jnp.exp(jnp.sum(lp[2:3] * lp[3:4], axis=-1, keepdims=True)) + lam_init)
    q = q_ref[...]
    lane = lax.broadcasted_iota(jnp.int32, q.shape, 1)
    zero = jnp.zeros_like(q)
    load_k = lambda rows: k_ref[rows, :]
    load_vt = lambda cols: vt_ref[:, cols]
    o1, o2 = _attend_pipelined([(load_k, jnp.where(lane < DIFF_HEAD_DIM, q, zero)),
                                (load_k, jnp.where(lane >= DIFF_HEAD_DIM, q, zero))],
                               [load_vt, load_vt], s_ref, m_ref, k_ref.shape[0])
    o = (o1 - lam * o2).T
    o_ref[...] = (_rms(o, g_ref[...], SUBLN_EPS) * (1.0 - lam_init)).astype(o_ref.dtype)


def _diff_attention(qkv, lam_params, g_subln, lam_init, tq):
    b, s, _ = qkv.shape
    dv = 2 * DIFF_HEAD_DIM
    nh, nq = DIFF_HEADS, s // tq
    n_tiles = b * nh * nq
    tile_a = lambda t: _flat_tile(jnp.minimum(t, n_tiles - 1), nh, nq)
    tile_b = lambda t: _flat_tile(jnp.maximum(t - 1, 0), nh, nq)

    def q_map(t):
        bi, h, qi = tile_a(t)
        return bi, qi, h

    def k_map(t):
        bi, h, _ = tile_a(t)
        return bi, 0, nh + h

    def v_map(t):
        bi, h, _ = tile_b(t)
        return bi, 0, 2 * nh + h

    def o_map(t):
        bi, h, qi = tile_b(t)
        return bi, qi, h

    return pl.pallas_call(
        functools.partial(_diff_attn_kernel, lam_init=lam_init, tiles_per_head=nq),
        grid=(n_tiles + 1,),
        in_specs=[pl.BlockSpec(lam_params.shape, lambda t: (0, 0)),
                  pl.BlockSpec((1, dv), lambda t: (0, 0)),
                  pl.BlockSpec((None, tq, dv), q_map),
                  pl.BlockSpec((None, s, dv), k_map),
                  pl.BlockSpec((None, s, dv), v_map)],
        out_specs=pl.BlockSpec((None, tq, dv), o_map),
        out_shape=jax.ShapeDtypeStruct((b, s, nh * dv), BF16),
        scratch_shapes=_attn_scratch(2, s, tq, dv),
        compiler_params=_cparams("arbitrary"),
        name="diff_attention",
    )(lam_params, g_subln.reshape(1, dv), qkv, qkv, qkv)


def _mla_attn_kernel(q_ref, k_ref, v_ref, o_ref, vt_ref, s_ref, m_ref, *, tiles_per_head):
    _init_attn_scratch(pl.program_id(0), tiles_per_head, v_ref, vt_ref, s_ref, m_ref)
    a_chains, b_chains = [], []
    for h in range(MLA_HEADS_PER_STEP):
        kc = slice(h * MLA_QK_PAD, (h + 1) * MLA_QK_PAD)
        vr = slice(h * MLA_V_DIM, (h + 1) * MLA_V_DIM)
        a_chains.append((lambda rows, kc=kc: k_ref[rows, kc], q_ref[:, kc]))
        b_chains.append(lambda cols, vr=vr: vt_ref[vr, cols])
    outs = _attend_pipelined(a_chains, b_chains, s_ref, m_ref, k_ref.shape[0])
    for h, o in enumerate(outs):
        o_ref[:, h * MLA_V_DIM:(h + 1) * MLA_V_DIM] = o.T.astype(o_ref.dtype)


def _mla_attention(q, k, v, tq):
    b, s, _ = q.shape
    hp = MLA_HEADS_PER_STEP
    nh, dq, dv = MLA_HEADS // hp, hp * MLA_QK_PAD, hp * MLA_V_DIM
    nq = s // tq
    n_tiles = b * nh * nq
    tile_a = lambda t: _flat_tile(jnp.minimum(t, n_tiles - 1), nh, nq)
    tile_b = lambda t: _flat_tile(jnp.maximum(t - 1, 0), nh, nq)

    def q_map(t):
        bi, h, qi = tile_a(t)
        return bi, qi, h

    def k_map(t):
        bi, h, _ = tile_a(t)
        return bi, 0, h

    def v_map(t):
        bi, h, _ = tile_b(t)
        return bi, 0, h

    def o_map(t):
        bi, h, qi = tile_b(t)
        return bi, qi, h

    return pl.pallas_call(
        functools.partial(_mla_attn_kernel, tiles_per_head=nq),
        grid=(n_tiles + 1,),
        in_specs=[pl.BlockSpec((None, tq, dq), q_map), pl.BlockSpec((None, s, dq), k_map),
                  pl.BlockSpec((None, s, dv), v_map)],
        out_specs=pl.BlockSpec((None, tq, dv), o_map),
        out_shape=jax.ShapeDtypeStruct((b, s, nh * dv), BF16),
        scratch_shapes=_attn_scratch(hp, s, tq, dv),
        compiler_params=_cparams("arbitrary"),
        name="mla_attention",
    )(q, k, v)


def _mix_out_kernel(x_ref, g_ref, od_ref, om_ref, wg_ref, wbd_ref, wbm_ref, wo_ref, xo_ref, mg_ref, *, tn):
    od, om = od_ref[...], om_ref[...]
    h = _rms(x_ref[...], g_ref[...], NORM_EPS).astype(BF16)
    n = mg_ref.shape[1]
    for t in range(n // tn):
        cols = slice(t * tn, (t + 1) * tn)
        ga = _sigmoid(_dot(h, wg_ref[:, cols]))
        gb = _sigmoid(_dot(h, wg_ref[:, n + t * tn:n + (t + 1) * tn]))
        mg_ref[:, cols] = (ga * _dot(od, wbd_ref[:, cols]) + gb * _dot(om, wbm_ref[:, cols])).astype(BF16)
    xo_ref[...] = x_ref[...] + _dot(mg_ref[...], wo_ref[...])


def _mix_out(x, g, od, om, w_gates, wbd, wbm, wo, layer, tm, tn):
    m, d = x.shape
    n = wbd.shape[2]
    row = lambda a: pl.BlockSpec((tm, a.shape[1]), lambda i: (i, 0))
    resident = lambda a: pl.BlockSpec((None,) + a.shape[1:], lambda i: (layer, 0, 0), pipeline_mode=pl.Buffered(1))
    return pl.pallas_call(
        functools.partial(_mix_out_kernel, tn=tn),
        grid=(m // tm,),
        in_specs=[row(x), pl.BlockSpec((1, d), lambda i: (0, 0)), row(od), row(om), resident(w_gates), resident(wbd),
                  resident(wbm), resident(wo)],
        out_specs=row(x),
        out_shape=jax.ShapeDtypeStruct((m, d), F32),
        scratch_shapes=[pltpu.VMEM((tm, n), BF16)],
        compiler_params=_cparams("parallel"),
        name="mix_out",
    )(x, g.reshape(1, d), od, om, w_gates, wbd, wbm, wo)


def _ffn_kernel(x_ref, g_ref, wg_ref, wu_ref, wd_ref, xo_ref, h_ref):
    @pl.when(pl.program_id(1) == 0)
    def _():
        x = x_ref[...]
        xo_ref[...] = x
        h_ref[...] = _rms(x, g_ref[...], NORM_EPS).astype(h_ref.dtype)

    h = h_ref[...]
    gate = _dot(h, wg_ref[...])
    act = (gate * _sigmoid(gate) * _dot(h, wu_ref[...])).astype(BF16)
    xo_ref[...] += _dot(act, wd_ref[...])


def _ffn(x, g, w_gate_up, w_down, layer, tm, tn):
    m, d = x.shape
    n = w_down.shape[1]
    row = pl.BlockSpec((tm, d), lambda i, j: (i, 0))
    return pl.pallas_call(
        _ffn_kernel,
        grid=(m // tm, n // tn),
        in_specs=[row, pl.BlockSpec((1, d), lambda i, j: (0, 0)),
                  pl.BlockSpec((None, d, tn), lambda i, j: (layer, 0, j)),
                  pl.BlockSpec((None, d, tn), lambda i, j: (layer, 0, j + n // tn)),
                  pl.BlockSpec((None, tn, d), lambda i, j: (layer, j, 0))],
        out_specs=row,
        out_shape=jax.ShapeDtypeStruct((m, d), F32),
        scratch_shapes=[pltpu.VMEM((tm, d), BF16)],
        compiler_params=_cparams("parallel", "arbitrary"),
        name="ffn",
    )(x, g.reshape(1, d), w_gate_up, w_gate_up, w_down)


def _ple_kernel(x_ref, g_ref, p_ref, wg_ref, wp_ref, gf_ref, o_ref, *, final_norm):
    x = x_ref[...]
    gate = _sigmoid(_dot(_rms(x, g_ref[...], NORM_EPS).astype(BF16), wg_ref[...]))
    x_new = x + gate * _dot(p_ref[...].astype(BF16), wp_ref[...])
    o_ref[...] = _rms(x_new, gf_ref[...], NORM_EPS) if final_norm else x_new


def _ple(x, g, p, wg, wp, layer, g_final, final_norm, tm):
    m, d = x.shape
    row = pl.BlockSpec((tm, d), lambda i: (i, 0))
    vec = pl.BlockSpec((1, d), lambda i: (0, 0))
    full = lambda a: pl.BlockSpec((None,) + a.shape[1:], lambda i: (layer, 0, 0), pipeline_mode=pl.Buffered(1))
    return pl.pallas_call(
        functools.partial(_ple_kernel, final_norm=final_norm),
        grid=(m // tm,),
        in_specs=[row, vec, pl.BlockSpec((None, tm, p.shape[2]), lambda i: (layer, i, 0)), full(wg), full(wp), vec],
        out_specs=row,
        out_shape=jax.ShapeDtypeStruct((m, d), F32),
        compiler_params=_cparams("parallel"),
        name="ple_gate",
    )(x, g.reshape(1, d), p, wg, wp, g_final.reshape(1, d))


def _rope_tables(positions, dim, period, pass_through):
    inv_freq = ROPE_THETA ** (-jnp.arange(0, dim, 2, dtype=F32) / dim)
    ang = positions.astype(F32).reshape(-1, 1) * inv_freq
    cos, sin = jnp.cos(ang), jnp.sin(ang)
    m, half = ang.shape
    pad = period - dim
    rest, zero = jnp.full((m, pad), pass_through, F32), jnp.zeros((m, pad), F32)
    zh = jnp.zeros((m, half), F32)
    reps = LANES // period
    c = jnp.tile(jnp.concatenate([cos, cos, rest], axis=1), (1, reps))
    s_lo = jnp.tile(jnp.concatenate([-sin, zh, zero], axis=1), (1, reps))
    s_hi = jnp.tile(jnp.concatenate([zh, sin, zero], axis=1), (1, reps))
    return c, s_lo, s_hi


def _tile(n, pref):
    t = min(n, pref)
    assert n % t == 0, (n, t)
    return t


def kernel(x, p, positions, g_mix, w_in, lambda_q1, lambda_k1, lambda_q2, lambda_k2, g_subln, g_q_latent, w_q_up, g_kv_latent, w_kv_up, w_branch_diff, w_branch_mla, w_out, g_ffn, w_gate_up, w_down, w_ple_in, g_ple, w_ple_gate, g_final):
    b, s, d = x.shape
    depth = w_in.shape[0]
    m = b * s
    tm = _tile(m, 1024)
    tm_small = _tile(m, 512)
    tm_mix = _tile(m, 256)
    tq = _tile(s, 512)
    tn = 512
    assert s % KEY_CHUNK == 0

    tabs_p = _rope_tables(positions, PARTIAL_ROPE_DIM, DIFF_HEAD_DIM, 1.0)
    tabs_m = _rope_tables(positions, MLA_ROPE_DIM, LANES, 0.0)

    xf = x.reshape(m, d)
    lat0 = 3 * DIFF_WIDTH
    gates0 = lat0 + MLA_Q_RANK + MLA_KV_RANK + MLA_ROPE_DIM
    assert gates0 - lat0 <= LATENT_TILE
    qk_dim = MLA_NOPE_DIM + MLA_ROPE_DIM
    w_main = w_in[:, :, :lat0 + LATENT_TILE].astype(BF16)
    w_gates = w_in[:, :, gates0:].astype(BF16)
    w_bd, w_bm, w_o = w_branch_diff.astype(BF16), w_branch_mla.astype(BF16), w_out.astype(BF16)
    w_gu, w_dn = w_gate_up.astype(BF16), w_down.astype(BF16)
    w_pg, w_pi = w_ple_gate.astype(BF16), w_ple_in.astype(BF16)
    pf = p.reshape(depth, m, -1)
    for i in range(depth):
        lam_init = 0.8 - 0.6 * math.exp(-0.3 * i)
        w_q = jnp.pad(w_q_up[i].reshape(MLA_Q_RANK, MLA_HEADS, qk_dim),
                      ((0, 0), (0, 0), (0, MLA_QK_PAD - qk_dim))).reshape(MLA_Q_RANK, MLA_HEADS * MLA_QK_PAD).astype(BF16)
        w_kv = w_kv_up[i].reshape(MLA_KV_RANK, MLA_HEADS, MLA_NOPE_DIM + MLA_V_DIM)
        w_k = w_kv[:, :, :MLA_NOPE_DIM].reshape(MLA_KV_RANK, MLA_HEADS * MLA_NOPE_DIM).astype(BF16)
        w_v = w_kv[:, :, MLA_NOPE_DIM:].reshape(MLA_KV_RANK, MLA_HEADS * MLA_V_DIM).astype(BF16)
        lam_params = jnp.stack([lambda_q1[i], lambda_k1[i], lambda_q2[i], lambda_k2[i]]).astype(F32)

        qkv, qm, km, vm = _in_proj(xf, g_mix[i], w_main, i, g_q_latent[i].reshape(1, -1), w_q,
                                   g_kv_latent[i].reshape(1, -1), w_k, w_v, tabs_p, tabs_m, tm_small)
        od = _diff_attention(qkv.reshape(b, s, 3 * DIFF_WIDTH), lam_params, g_subln[i], lam_init, tq)
        om = _mla_attention(qm.reshape(b, s, -1), km.reshape(b, s, -1), vm.reshape(b, s, -1), tq)
        xf = _mix_out(xf, g_mix[i], od.reshape(m, -1), om.reshape(m, -1), w_gates, w_bd, w_bm, w_o, i, tm_mix, tn)

        xf = _ffn(xf, g_ffn[i], w_gu, w_dn, i, tm, tn)

        xf = _ple(xf, g_ple[i], pf, w_pg, w_pi, i, g_final, i == depth - 1, tm_small)
    return xf.reshape(b, s, d)
```

```python
import functools
import math

import jax
import jax.numpy as jnp
from jax import lax
from jax.experimental import pallas as pl
from jax.experimental.pallas import tpu as pltpu

F32 = jnp.float32
BF16 = jnp.bfloat16

DIFF_HEADS = 8
DIFF_HEAD_DIM = 64
DIFF_WIDTH = DIFF_HEADS * 2 * DIFF_HEAD_DIM
PARTIAL_ROPE_DIM = DIFF_HEAD_DIM // 4
MLA_HEADS = 8
MLA_Q_RANK = 512
MLA_KV_RANK = 256
MLA_NOPE_DIM = 128
MLA_ROPE_DIM = 64
MLA_V_DIM = 128
MLA_QK_PAD = 256
MLA_HEADS_PER_STEP = 2
ROPE_THETA = 500000.0
NORM_EPS = 1e-6
SUBLN_EPS = 1e-5
LOG2E = math.log2(math.e)
LANES = 128
VMEM_LIMIT_BYTES = 56 * 1024 * 1024
KEY_CHUNK = 256
QKV_SUBTILE = 256
def _cparams(*sem):
    return pltpu.CompilerParams(dimension_semantics=sem, vmem_limit_bytes=VMEM_LIMIT_BYTES)


def _dot(a, b):
    return jnp.dot(a, b, preferred_element_type=F32)


def _sigmoid(x):
    return 1.0 / (1.0 + jnp.exp(-x))


def _rms(x, g, eps):
    return x * lax.rsqrt(jnp.mean(x * x, axis=-1, keepdims=True) + eps) * g


def _transpose_bf16(a):
    return a.astype(F32).T.astype(BF16)


def _rope_chunk(z, c, s_lo, s_hi, half):
    return z * c + pltpu.roll(z, LANES - half, 1) * s_lo + pltpu.roll(z, half, 1) * s_hi


LATENT_TILE = MLA_Q_RANK + MLA_KV_RANK + 2 * LANES


def _in_proj_kernel(x_ref, g_ref, w_ref, gq_ref, wq_ref, gkv_ref, wk_ref, wv_ref, cp_ref, slop_ref, ship_ref,
                    cm_ref, slom_ref, shim_ref, qkv_ref, q_ref, k_ref, v_ref, *, q_scale, mla_scale):
    h = _rms(x_ref[...], g_ref[...], NORM_EPS).astype(BF16)
    plain = (cp_ref[...], slop_ref[...], ship_ref[...])
    scaled = tuple(t * q_scale for t in plain)
    for u in range(3 * DIFF_WIDTH // QKV_SUBTILE):
        z = _dot(h, w_ref[:, u * QKV_SUBTILE:(u + 1) * QKV_SUBTILE])
        part = u * QKV_SUBTILE // DIFF_WIDTH
        for t in range(QKV_SUBTILE // LANES):
            sl = slice(u * QKV_SUBTILE + t * LANES, u * QKV_SUBTILE + (t + 1) * LANES)
            zc = z[:, t * LANES:(t + 1) * LANES]
            if part < 2:
                zc = _rope_chunk(zc, *(scaled if part == 0 else plain), PARTIAL_ROPE_DIM // 2)
            qkv_ref[:, sl] = zc.astype(qkv_ref.dtype)

    lat0 = 3 * DIFF_WIDTH
    z = _dot(h, w_ref[:, lat0:lat0 + LATENT_TILE])
    cq = _rms(z[:, :MLA_Q_RANK], gq_ref[...], NORM_EPS).astype(BF16)
    ckv = _rms(z[:, MLA_Q_RANK:MLA_Q_RANK + MLA_KV_RANK], gkv_ref[...], NORM_EPS).astype(BF16)
    kr_off = MLA_Q_RANK + MLA_KV_RANK
    c, s_lo, s_hi = cm_ref[...], slom_ref[...], shim_ref[...]
    half = MLA_ROPE_DIM // 2
    kr = _rope_chunk(z[:, kr_off:kr_off + LANES], c, s_lo, s_hi, half).astype(BF16)
    qf = _dot(cq, wq_ref[...]) * mla_scale
    kn = _dot(ckv, wk_ref[...])
    for hd in range(MLA_HEADS):
        o = hd * MLA_QK_PAD
        q_ref[:, o:o + LANES] = qf[:, o:o + LANES].astype(BF16)
        q_ref[:, o + LANES:o + 2 * LANES] = _rope_chunk(qf[:, o + LANES:o + 2 * LANES], c, s_lo, s_hi, half).astype(BF16)
        k_ref[:, o:o + LANES] = kn[:, hd * LANES:(hd + 1) * LANES].astype(BF16)
        k_ref[:, o + LANES:o + 2 * LANES] = kr
    v_ref[...] = _dot(ckv, wv_ref[...]).astype(BF16)


def _in_proj(x, g, w_main, layer, gq, wq, gkv, wk, wv, tabs_p, tabs_m, tm):
    m, k = x.shape
    g = g.reshape(1, k)
    once = lambda a: pl.BlockSpec(a.shape, lambda i: (0,) * a.ndim, pipeline_mode=pl.Buffered(1))
    tab_spec = pl.BlockSpec((tm, LANES), lambda i: (i, 0))
    nqkv, nq, nv = 3 * DIFF_WIDTH, MLA_HEADS * MLA_QK_PAD, MLA_HEADS * MLA_V_DIM
    row = lambda n: pl.BlockSpec((tm, n), lambda i: (i, 0))
    return pl.pallas_call(
        functools.partial(_in_proj_kernel, q_scale=DIFF_HEAD_DIM ** -0.5 * LOG2E,
                          mla_scale=(MLA_NOPE_DIM + MLA_ROPE_DIM) ** -0.5 * LOG2E),
        grid=(m // tm,),
        in_specs=[row(k), once(g),
                  pl.BlockSpec((None,) + w_main.shape[1:], lambda i: (layer, 0, 0), pipeline_mode=pl.Buffered(1)),
                  once(gq), once(wq), once(gkv), once(wk), once(wv)] + [tab_spec] * 6,
        out_specs=[row(nqkv), row(nq), row(nq), row(nv)],
        out_shape=[jax.ShapeDtypeStruct((m, nqkv), BF16), jax.ShapeDtypeStruct((m, nq), BF16),
                   jax.ShapeDtypeStruct((m, nq), BF16), jax.ShapeDtypeStruct((m, nv), BF16)],
        compiler_params=_cparams("parallel"),
        name="in_proj",
    )(x, g, w_main, gq, wq, gkv, wk, wv, *tabs_p, *tabs_m)


def _attend_pipelined(a_chains, b_chains, s_ref, m_ref, n_keys):
    n = len(a_chains)
    m_prev = [m_ref[u] for u in range(n)]
    acc, l, m_new = [None] * n, [None] * n, [None] * n
    for c in range(n_keys // KEY_CHUNK):
        rows = slice(c * KEY_CHUNK, (c + 1) * KEY_CHUNK)
        for u in range(n):
            p = jnp.exp2(s_ref[u, rows, :] - m_prev[u])
            lc = jnp.sum(p, axis=0, keepdims=True)
            pv = _dot(b_chains[u](rows), p.astype(BF16))
            acc[u] = pv if c == 0 else acc[u] + pv
            l[u] = lc if c == 0 else l[u] + lc
        for u, (load_k, q_t) in enumerate(a_chains):
            sc = _dot(load_k(rows), q_t)
            s_ref[u, rows, :] = sc
            mc = jnp.max(sc, axis=0, keepdims=True)
            m_new[u] = mc if c == 0 else jnp.maximum(m_new[u], mc)
    for u in range(n):
        m_ref[u] = m_new[u]
    return [acc[u] / l[u] for u in range(n)]


def _attn_scratch(n_chains, n_keys, tq, dv):
    return [pltpu.VMEM((dv, n_keys), BF16), pltpu.VMEM((n_chains, n_keys, tq), F32),
            pltpu.VMEM((n_chains, 1, tq), F32)]


def _init_attn_scratch(t, tiles_per_head, v_ref, vt_ref, s_ref, m_ref):
    @pl.when(t == 0)
    def _():
        s_ref[...] = jnp.zeros_like(s_ref)
        m_ref[...] = jnp.zeros_like(m_ref)

    @pl.when(jnp.maximum(t - 1, 0) % tiles_per_head == 0)
    def _():
        vt_ref[...] = _transpose_bf16(v_ref[...])


def _flat_tile(u, n_heads, tiles_per_head):
    return u // (n_heads * tiles_per_head), (u // tiles_per_head) % n_heads, u % tiles_per_head


def _diff_attn_kernel(lam_ref, g_ref, q_ref, k_ref, v_ref, o_ref, vt_ref, s_ref, m_ref, *, lam_init, tiles_per_head):
    _init_attn_scratch(pl.program_id(0), tiles_per_head, v_ref, vt_ref, s_ref, m_ref)
    lp = lam_ref[...]
    lam = (jnp.exp(jnp.sum(lp[0:1] * lp[1:2], axis=-1, keepdims=True))
           - jnp.exp(jnp.sum(lp[2:3] * lp[3:4], axis=-1, keepdims=True)) + lam_init)
    q_t = _transpose_bf16(q_ref[...])
    row = lax.broadcasted_iota(jnp.int32, q_t.shape, 0)
    zero = jnp.zeros_like(q_t)
    load_k = lambda rows: k_ref[rows, :]
    load_vt = lambda cols: vt_ref[:, cols]
    o1, o2 = _attend_pipelined([(load_k, jnp.where(row < DIFF_HEAD_DIM, q_t, zero)),
                                (load_k, jnp.where(row >= DIFF_HEAD_DIM, q_t, zero))],
                               [load_vt, load_vt], s_ref, m_ref, k_ref.shape[0])
    o = (o1 - lam * o2).T
    o_ref[...] = (_rms(o, g_ref[...], SUBLN_EPS) * (1.0 - lam_init)).astype(o_ref.dtype)


def _diff_attention(qkv, lam_params, g_subln, lam_init, tq):
    b, s, _ = qkv.shape
    dv = 2 * DIFF_HEAD_DIM
    nh, nq = DIFF_HEADS, s // tq
    n_tiles = b * nh * nq
    tile_a = lambda t: _flat_tile(jnp.minimum(t, n_tiles - 1), nh, nq)
    tile_b = lambda t: _flat_tile(jnp.maximum(t - 1, 0), nh, nq)

    def q_map(t):
        bi, h, qi = tile_a(t)
        return bi, qi, h

    def k_map(t):
        bi, h, _ = tile_a(t)
        return bi, 0, nh + h

    def v_map(t):
        bi, h, _ = tile_b(t)
        return bi, 0, 2 * nh + h

    def o_map(t):
        bi, h, qi = tile_b(t)
        return bi, qi, h

    return pl.pallas_call(
        functools.partial(_diff_attn_kernel, lam_init=lam_init, tiles_per_head=nq),
        grid=(n_tiles + 1,),
        in_specs=[pl.BlockSpec(lam_params.shape, lambda t: (0, 0)),
                  pl.BlockSpec((1, dv), lambda t: (0, 0)),
                  pl.BlockSpec((None, tq, dv), q_map),
                  pl.BlockSpec((None, s, dv), k_map),
                  pl.BlockSpec((None, s, dv), v_map)],
        out_specs=pl.BlockSpec((None, tq, dv), o_map),
        out_shape=jax.ShapeDtypeStruct((b, s, nh * dv), BF16),
        scratch_shapes=_attn_scratch(2, s, tq, dv),
        compiler_params=_cparams("arbitrary"),
        name="diff_attention",
    )(lam_params, g_subln.reshape(1, dv), qkv, qkv, qkv)


def _mla_attn_kernel(q_ref, k_ref, v_ref, o_ref, vt_ref, s_ref, m_ref, *, tiles_per_head):
    _init_attn_scratch(pl.program_id(0), tiles_per_head, v_ref, vt_ref, s_ref, m_ref)
    a_chains, b_chains = [], []
    for h in range(MLA_HEADS_PER_STEP):
        kc = slice(h * MLA_QK_PAD, (h + 1) * MLA_QK_PAD)
        vr = slice(h * MLA_V_DIM, (h + 1) * MLA_V_DIM)
        a_chains.append((lambda rows, kc=kc: k_ref[rows, kc], _transpose_bf16(q_ref[:, kc])))
        b_chains.append(lambda cols, vr=vr: vt_ref[vr, cols])
    outs = _attend_pipelined(a_chains, b_chains, s_ref, m_ref, k_ref.shape[0])
    for h, o in enumerate(outs):
        o_ref[:, h * MLA_V_DIM:(h + 1) * MLA_V_DIM] = o.T.astype(o_ref.dtype)


def _mla_attention(q, k, v, tq):
    b, s, _ = q.shape
    hp = MLA_HEADS_PER_STEP
    nh, dq, dv = MLA_HEADS // hp, hp * MLA_QK_PAD, hp * MLA_V_DIM
    nq = s // tq
    n_tiles = b * nh * nq
    tile_a = lambda t: _flat_tile(jnp.minimum(t, n_tiles - 1), nh, nq)
    tile_b = lambda t: _flat_tile(jnp.maximum(t - 1, 0), nh, nq)

    def q_map(t):
        bi, h, qi = tile_a(t)
        return bi, qi, h

    def k_map(t):
        bi, h, _ = tile_a(t)
        return bi, 0, h

    def v_map(t):
        bi, h, _ = tile_b(t)
        return bi, 0, h

    def o_map(t):
        bi, h, qi = tile_b(t)
        return bi, qi, h

    return pl.pallas_call(
        functools.partial(_mla_attn_kernel, tiles_per_head=nq),
        grid=(n_tiles + 1,),
        in_specs=[pl.BlockSpec((None, tq, dq), q_map), pl.BlockSpec((None, s, dq), k_map),
                  pl.BlockSpec((None, s, dv), v_map)],
        out_specs=pl.BlockSpec((None, tq, dv), o_map),
        out_shape=jax.ShapeDtypeStruct((b, s, nh * dv), BF16),
        scratch_shapes=_attn_scratch(hp, s, tq, dv),
        compiler_params=_cparams("arbitrary"),
        name="mla_attention",
    )(q, k, v)


def _mix_out_kernel(x_ref, g_ref, od_ref, om_ref, wg_ref, wbd_ref, wbm_ref, wo_ref, xo_ref, mg_ref, *, tn):
    od, om = od_ref[...], om_ref[...]
    h = _rms(x_ref[...], g_ref[...], NORM_EPS).astype(BF16)
    n = mg_ref.shape[1]
    for t in range(n // tn):
        cols = slice(t * tn, (t + 1) * tn)
        ga = _sigmoid(_dot(h, wg_ref[:, cols]))
        gb = _sigmoid(_dot(h, wg_ref[:, n + t * tn:n + (t + 1) * tn]))
        mg_ref[:, cols] = (ga * _dot(od, wbd_ref[:, cols]) + gb * _dot(om, wbm_ref[:, cols])).astype(BF16)
    xo_ref[...] = x_ref[...] + _dot(mg_ref[...], wo_ref[...])


def _mix_out(x, g, od, om, w_gates, wbd, wbm, wo, layer, tm, tn):
    m, d = x.shape
    n = wbd.shape[2]
    row = lambda a: pl.BlockSpec((tm, a.shape[1]), lambda i: (i, 0))
    resident = lambda a: pl.BlockSpec((None,) + a.shape[1:], lambda i: (layer, 0, 0), pipeline_mode=pl.Buffered(1))
    return pl.pallas_call(
        functools.partial(_mix_out_kernel, tn=tn),
        grid=(m // tm,),
        in_specs=[row(x), pl.BlockSpec((1, d), lambda i: (0, 0)), row(od), row(om), resident(w_gates), resident(wbd),
                  resident(wbm), resident(wo)],
        out_specs=row(x),
        out_shape=jax.ShapeDtypeStruct((m, d), F32),
        scratch_shapes=[pltpu.VMEM((tm, n), BF16)],
        compiler_params=_cparams("parallel"),
        name="mix_out",
    )(x, g.reshape(1, d), od, om, w_gates, wbd, wbm, wo)


def _ffn_kernel(x_ref, g_ref, wg_ref, wu_ref, wd_ref, xo_ref, h_ref):
    @pl.when(pl.program_id(1) == 0)
    def _():
        x = x_ref[...]
        xo_ref[...] = x
        h_ref[...] = _rms(x, g_ref[...], NORM_EPS).astype(h_ref.dtype)

    h = h_ref[...]
    gate = _dot(h, wg_ref[...])
    act = (gate * _sigmoid(gate) * _dot(h, wu_ref[...])).astype(BF16)
    xo_ref[...] += _dot(act, wd_ref[...])


def _ffn(x, g, w_gate_up, w_down, layer, tm, tn):
    m, d = x.shape
    n = w_down.shape[1]
    row = pl.BlockSpec((tm, d), lambda i, j: (i, 0))
    return pl.pallas_call(
        _ffn_kernel,
        grid=(m // tm, n // tn),
        in_specs=[row, pl.BlockSpec((1, d), lambda i, j: (0, 0)),
                  pl.BlockSpec((None, d, tn), lambda i, j: (layer, 0, j)),
                  pl.BlockSpec((None, d, tn), lambda i, j: (layer, 0, j + n // tn)),
                  pl.BlockSpec((None, tn, d), lambda i, j: (layer, j, 0))],
        out_specs=row,
        out_shape=jax.ShapeDtypeStruct((m, d), F32),
        scratch_shapes=[pltpu.VMEM((tm, d), BF16)],
        compiler_params=_cparams("parallel", "arbitrary"),
        name="ffn",
    )(x, g.reshape(1, d), w_gate_up, w_gate_up, w_down)


def _ple_kernel(x_ref, g_ref, p_ref, wg_ref, wp_ref, gf_ref, o_ref, *, final_norm):
    x = x_ref[...]
    gate = _sigmoid(_dot(_rms(x, g_ref[...], NORM_EPS).astype(BF16), wg_ref[...]))
    x_new = x + gate * _dot(p_ref[...].astype(BF16), wp_ref[...])
    o_ref[...] = _rms(x_new, gf_ref[...], NORM_EPS) if final_norm else x_new


def _ple(x, g, p, wg, wp, layer, g_final, final_norm, tm):
    m, d = x.shape
    row = pl.BlockSpec((tm, d), lambda i: (i, 0))
    vec = pl.BlockSpec((1, d), lambda i: (0, 0))
    full = lambda a: pl.BlockSpec((None,) + a.shape[1:], lambda i: (layer, 0, 0), pipeline_mode=pl.Buffered(1))
    return pl.pallas_call(
        functools.partial(_ple_kernel, final_norm=final_norm),
        grid=(m // tm,),
        in_specs=[row, vec, pl.BlockSpec((None, tm, p.shape[2]), lambda i: (layer, i, 0)), full(wg), full(wp), vec],
        out_specs=row,
        out_shape=jax.ShapeDtypeStruct((m, d), F32),
        compiler_params=_cparams("parallel"),
        name="ple_gate",
    )(x, g.reshape(1, d), p, wg, wp, g_final.reshape(1, d))


def _rope_tables(positions, dim, period, pass_through):
    inv_freq = ROPE_THETA ** (-jnp.arange(0, dim, 2, dtype=F32) / dim)
    ang = positions.astype(F32).reshape(-1, 1) * inv_freq
    cos, sin = jnp.cos(ang), jnp.sin(ang)
    m, half = ang.shape
    pad = period - dim
    rest, zero = jnp.full((m, pad), pass_through, F32), jnp.zeros((m, pad), F32)
    zh = jnp.zeros((m, half), F32)
    reps = LANES // period
    c = jnp.tile(jnp.concatenate([cos, cos, rest], axis=1), (1, reps))
    s_lo = jnp.tile(jnp.concatenate([-sin, zh, zero], axis=1), (1, reps))
    s_hi = jnp.tile(jnp.concatenate([zh, sin, zero], axis=1), (1, reps))
    return c, s_lo, s_hi


def _tile(n, pref):
    t = min(n, pref)
    assert n % t == 0, (n, t)
    return t


def kernel(x, p, positions, g_mix, w_in, lambda_q1, lambda_k1, lambda_q2, lambda_k2, g_subln, g_q_latent, w_q_up, g_kv_latent, w_kv_up, w_branch_diff, w_branch_mla, w_out, g_ffn, w_gate_up, w_down, w_ple_in, g_ple, w_ple_gate, g_final):
    b, s, d = x.shape
    depth = w_in.shape[0]
    m = b * s
    tm = _tile(m, 1024)
    tm_small = _tile(m, 512)
    tm_mix = _tile(m, 256)
    tq = _tile(s, 512)
    tn = 512
    assert s % KEY_CHUNK == 0

    tabs_p = _rope_tables(positions, PARTIAL_ROPE_DIM, DIFF_HEAD_DIM, 1.0)
    tabs_m = _rope_tables(positions, MLA_ROPE_DIM, LANES, 0.0)

    xf = x.reshape(m, d)
    lat0 = 3 * DIFF_WIDTH
    gates0 = lat0 + MLA_Q_RANK + MLA_KV_RANK + MLA_ROPE_DIM
    assert gates0 - lat0 <= LATENT_TILE
    qk_dim = MLA_NOPE_DIM + MLA_ROPE_DIM
    w_main = w_in[:, :, :lat0 + LATENT_TILE].astype(BF16)
    w_gates = w_in[:, :, gates0:].astype(BF16)
    w_bd, w_bm, w_o = w_branch_diff.astype(BF16), w_branch_mla.astype(BF16), w_out.astype(BF16)
    w_gu, w_dn = w_gate_up.astype(BF16), w_down.astype(BF16)
    w_pg, w_pi = w_ple_gate.astype(BF16), w_ple_in.astype(BF16)
    pf = p.reshape(depth, m, -1)
    for i in range(depth):
        lam_init = 0.8 - 0.6 * math.exp(-0.3 * i)
        w_q = jnp.pad(w_q_up[i].reshape(MLA_Q_RANK, MLA_HEADS, qk_dim),
                      ((0, 0), (0, 0), (0, MLA_QK_PAD - qk_dim))).reshape(MLA_Q_RANK, MLA_HEADS * MLA_QK_PAD).astype(BF16)
        w_kv = w_kv_up[i].reshape(MLA_KV_RANK, MLA_HEADS, MLA_NOPE_DIM + MLA_V_DIM)
        w_k = w_kv[:, :, :MLA_NOPE_DIM].reshape(MLA_KV_RANK, MLA_HEADS * MLA_NOPE_DIM).astype(BF16)
        w_v = w_kv[:, :, MLA_NOPE_DIM:].reshape(MLA_KV_RANK, MLA_HEADS * MLA_V_DIM).astype(BF16)
        lam_params = jnp.stack([lambda_q1[i], lambda_k1[i], lambda_q2[i], lambda_k2[i]]).astype(F32)

        qkv, qm, km, vm = _in_proj(xf, g_mix[i], w_main, i, g_q_latent[i].reshape(1, -1), w_q,
                                   g_kv_latent[i].reshape(1, -1), w_k, w_v, tabs_p, tabs_m, tm_small)
        od = _diff_attention(qkv.reshape(b, s, 3 * DIFF_WIDTH), lam_params, g_subln[i], lam_init, tq)
        om = _mla_attention(qm.reshape(b, s, -1), km.reshape(b, s, -1), vm.reshape(b, s, -1), tq)
        xf = _mix_out(xf, g_mix[i], od.reshape(m, -1), om.reshape(m, -1), w_gates, w_bd, w_bm, w_o, i, tm_mix, tn)

        xf = _ffn(xf, g_ffn[i], w_gu, w_dn, i, tm, tn)

        xf = _ple(xf, g_ple[i], pf, w_pg, w_pi, i, g_final, i == depth - 1, tm_small)
    return xf.reshape(b, s, d)
```

```python
import functools
import math

import jax
import jax.numpy as jnp
from jax import lax
from jax.experimental import pallas as pl
from jax.experimental.pallas import tpu as pltpu

F32 = jnp.float32
BF16 = jnp.bfloat16

DIFF_HEADS = 8
DIFF_HEAD_DIM = 64
DIFF_WIDTH = DIFF_HEADS * 2 * DIFF_HEAD_DIM
PARTIAL_ROPE_DIM = DIFF_HEAD_DIM // 4
MLA_HEADS = 8
MLA_Q_RANK = 512
MLA_KV_RANK = 256
MLA_NOPE_DIM = 128
MLA_ROPE_DIM = 64
MLA_V_DIM = 128
MLA_QK_PAD = 256
MLA_HEADS_PER_STEP = 2
ROPE_THETA = 500000.0
NORM_EPS = 1e-6
SUBLN_EPS = 1e-5
LOG2E = math.log2(math.e)
LANES = 128
VMEM_LIMIT_BYTES = 56 * 1024 * 1024
KEY_CHUNK = 256
QKV_SUBTILE = 256


def _cparams(*sem):
    return pltpu.CompilerParams(dimension_semantics=sem, vmem_limit_bytes=VMEM_LIMIT_BYTES)


def _dot(a, b):
    return jnp.dot(a, b, preferred_element_type=F32)


def _sigmoid(x):
    return 1.0 / (1.0 + jnp.exp(-x))


def _rms(x, g, eps):
    return x * lax.rsqrt(jnp.mean(x * x, axis=-1, keepdims=True) + eps) * g


def _transpose_bf16(a):
    return a.astype(F32).T.astype(BF16)


def _rope_chunk(z, c, s_lo, s_hi, half):
    return z * c + pltpu.roll(z, LANES - half, 1) * s_lo + pltpu.roll(z, half, 1) * s_hi


LATENT_TILE = MLA_Q_RANK + MLA_KV_RANK + 2 * LANES


def _in_proj_kernel(x_ref, g_ref, w_ref, gq_ref, wq_ref, gkv_ref, wk_ref, wv_ref, cp_ref, slop_ref, ship_ref,
                    cm_ref, slom_ref, shim_ref, qkv_ref, q_ref, k_ref, v_ref, *, q_scale, mla_scale):
    h = _rms(x_ref[...], g_ref[...], NORM_EPS).astype(BF16)
    plain = (cp_ref[...], slop_ref[...], ship_ref[...])
    scaled = tuple(t * q_scale for t in plain)
    for u in range(3 * DIFF_WIDTH // QKV_SUBTILE):
        z = _dot(h, w_ref[:, u * QKV_SUBTILE:(u + 1) * QKV_SUBTILE])
        part = u * QKV_SUBTILE // DIFF_WIDTH
        for t in range(QKV_SUBTILE // LANES):
            sl = slice(u * QKV_SUBTILE + t * LANES, u * QKV_SUBTILE + (t + 1) * LANES)
            zc = z[:, t * LANES:(t + 1) * LANES]
            if part < 2:
                zc = _rope_chunk(zc, *(scaled if part == 0 else plain), PARTIAL_ROPE_DIM // 2)
            qkv_ref[:, sl] = zc.astype(qkv_ref.dtype)

    lat0 = 3 * DIFF_WIDTH
    z = _dot(h, w_ref[:, lat0:lat0 + LATENT_TILE])
    cq = _rms(z[:, :MLA_Q_RANK], gq_ref[...], NORM_EPS).astype(BF16)
    ckv = _rms(z[:, MLA_Q_RANK:MLA_Q_RANK + MLA_KV_RANK], gkv_ref[...], NORM_EPS).astype(BF16)
    kr_off = MLA_Q_RANK + MLA_KV_RANK
    c, s_lo, s_hi = cm_ref[...], slom_ref[...], shim_ref[...]
    half = MLA_ROPE_DIM // 2
    kr = _rope_chunk(z[:, kr_off:kr_off + LANES], c, s_lo, s_hi, half).astype(BF16)
    qf = _dot(cq, wq_ref[...]) * mla_scale
    kn = _dot(ckv, wk_ref[...])
    for hd in range(MLA_HEADS):
        o = hd * MLA_QK_PAD
        q_ref[:, o:o + LANES] = qf[:, o:o + LANES].astype(BF16)
        q_ref[:, o + LANES:o + 2 * LANES] = _rope_chunk(qf[:, o + LANES:o + 2 * LANES], c, s_lo, s_hi, half).astype(BF16)
        k_ref[:, o:o + LANES] = kn[:, hd * LANES:(hd + 1) * LANES].astype(BF16)
        k_ref[:, o + LANES:o + 2 * LANES] = kr
    v_ref[...] = _dot(ckv, wv_ref[...]).astype(BF16)


def _in_proj(x, g, w_main, layer, gq, wq, gkv, wk, wv, tabs_p, tabs_m, tm):
    m, k = x.shape
    g = g.reshape(1, k)
    once = lambda a: pl.BlockSpec(a.shape, lambda i: (0,) * a.ndim, pipeline_mode=pl.Buffered(1))
    tab_spec = pl.BlockSpec((tm, LANES), lambda i: (i, 0))
    nqkv, nq, nv = 3 * DIFF_WIDTH, MLA_HEADS * MLA_QK_PAD, MLA_HEADS * MLA_V_DIM
    row = lambda n: pl.BlockSpec((tm, n), lambda i: (i, 0))
    return pl.pallas_call(
        functools.partial(_in_proj_kernel, q_scale=DIFF_HEAD_DIM ** -0.5 * LOG2E,
                          mla_scale=(MLA_NOPE_DIM + MLA_ROPE_DIM) ** -0.5 * LOG2E),
        grid=(m // tm,),
        in_specs=[row(k), once(g),
                  pl.BlockSpec((None, k, 3 * DIFF_WIDTH + LATENT_TILE), lambda i: (layer, 0, 0),
                               pipeline_mode=pl.Buffered(1)),
                  once(gq), once(wq), once(gkv), once(wk), once(wv)] + [tab_spec] * 6,
        out_specs=[row(nqkv), row(nq), row(nq), row(nv)],
        out_shape=[jax.ShapeDtypeStruct((m, nqkv), BF16), jax.ShapeDtypeStruct((m, nq), BF16),
                   jax.ShapeDtypeStruct((m, nq), BF16), jax.ShapeDtypeStruct((m, nv), BF16)],
        compiler_params=_cparams("parallel"),
        name="in_proj",
    )(x, g, w_main, gq, wq, gkv, wk, wv, *tabs_p, *tabs_m)


def _attend_pipelined(a_chains, b_chains, s_ref, m_ref, n_keys):
    n = len(a_chains)
    m_prev = [m_ref[u] for u in range(n)]
    acc, l, m_new = [None] * n, [None] * n, [None] * n
    for c in range(n_keys // KEY_CHUNK):
        rows = slice(c * KEY_CHUNK, (c + 1) * KEY_CHUNK)
        for u in range(n):
            p = jnp.exp2(s_ref[u, rows, :] - m_prev[u])
            lc = jnp.sum(p, axis=0, keepdims=True)
            pv = _dot(b_chains[u](rows), p.astype(BF16))
            acc[u] = pv if c == 0 else acc[u] + pv
            l[u] = lc if c == 0 else l[u] + lc
        for u, (load_k, q_t) in enumerate(a_chains):
            sc = _dot(load_k(rows), q_t)
            s_ref[u, rows, :] = sc
            mc = jnp.max(sc, axis=0, keepdims=True)
            m_new[u] = mc if c == 0 else jnp.maximum(m_new[u], mc)
    for u in range(n):
        m_ref[u] = m_new[u]
    return [acc[u] / l[u] for u in range(n)]


def _attn_scratch(n_chains, n_keys, tq, dv):
    return [pltpu.VMEM((dv, n_keys), BF16), pltpu.VMEM((n_chains, n_keys, tq), F32),
            pltpu.VMEM((n_chains, 1, tq), F32)]


def _init_attn_scratch(t, tiles_per_head, v_ref, vt_ref, s_ref, m_ref):
    @pl.when(t == 0)
    def _():
        s_ref[...] = jnp.zeros_like(s_ref)
        m_ref[...] = jnp.zeros_like(m_ref)

    @pl.when(jnp.maximum(t - 1, 0) % tiles_per_head == 0)
    def _():
        vt_ref[...] = _transpose_bf16(v_ref[...])


def _flat_tile(u, n_heads, tiles_per_head):
    return u // (n_heads * tiles_per_head), (u // tiles_per_head) % n_heads, u % tiles_per_head


def _diff_attn_kernel(lam_ref, g_ref, q_ref, k_ref, v_ref, o_ref, vt_ref, s_ref, m_ref, *, lam_init, tiles_per_head):
    _init_attn_scratch(pl.program_id(0), tiles_per_head, v_ref, vt_ref, s_ref, m_ref)
    lp = lam_ref[...]
    lam = (jnp.exp(jnp.sum(lp[0:1] * lp[1:2], axis=-1, keepdims=True))
           - jnp.exp(jnp.sum(lp[2:3] * lp[3:4], axis=-1, keepdims=True)) + lam_init)
    q_t = _transpose_bf16(q_ref[...])
    row = lax.broadcasted_iota(jnp.int32, q_t.shape, 0)
    zero = jnp.zeros_like(q_t)
    load_k = lambda rows: k_ref[rows, :]
    load_vt = lambda cols: vt_ref[:, cols]
    o1, o2 = _attend_pipelined([(load_k, jnp.where(row < DIFF_HEAD_DIM, q_t, zero)),
                                (load_k, jnp.where(row >= DIFF_HEAD_DIM, q_t, zero))],
                               [load_vt, load_vt], s_ref, m_ref, k_ref.shape[0])
    o = (o1 - lam * o2).T
    o_ref[...] = (_rms(o, g_ref[...], SUBLN_EPS) * (1.0 - lam_init)).astype(o_ref.dtype)


def _diff_attention(qkv, lam_params, g_subln, lam_init, tq):
    b, s, _ = qkv.shape
    dv = 2 * DIFF_HEAD_DIM
    nh, nq = DIFF_HEADS, s // tq
    n_tiles = b * nh * nq
    tile_a = lambda t: _flat_tile(jnp.minimum(t, n_tiles - 1), nh, nq)
    tile_b = lambda t: _flat_tile(jnp.maximum(t - 1, 0), nh, nq)

    def q_map(t):
        bi, h, qi = tile_a(t)
        return bi, qi, h

    def k_map(t):
        bi, h, _ = tile_a(t)
        return bi, 0, nh + h

    def v_map(t):
        bi, h, _ = tile_b(t)
        return bi, 0, 2 * nh + h

    def o_map(t):
        bi, h, qi = tile_b(t)
        return bi, qi, h

    return pl.pallas_call(
        functools.partial(_diff_attn_kernel, lam_init=lam_init, tiles_per_head=nq),
        grid=(n_tiles + 1,),
        in_specs=[pl.BlockSpec(lam_params.shape, lambda t: (0, 0)),
                  pl.BlockSpec((1, dv), lambda t: (0, 0)),
                  pl.BlockSpec((None, tq, dv), q_map),
                  pl.BlockSpec((None, s, dv), k_map),
                  pl.BlockSpec((None, s, dv), v_map)],
        out_specs=pl.BlockSpec((None, tq, dv), o_map),
        out_shape=jax.ShapeDtypeStruct((b, s, nh * dv), BF16),
        scratch_shapes=_attn_scratch(2, s, tq, dv),
        compiler_params=_cparams("arbitrary"),
        name="diff_attention",
    )(lam_params, g_subln.reshape(1, dv), qkv, qkv, qkv)


def _mla_attn_kernel(q_ref, k_ref, v_ref, o_ref, vt_ref, s_ref, m_ref, *, tiles_per_head):
    _init_attn_scratch(pl.program_id(0), tiles_per_head, v_ref, vt_ref, s_ref, m_ref)
    a_chains, b_chains = [], []
    for h in range(MLA_HEADS_PER_STEP):
        kc = slice(h * MLA_QK_PAD, (h + 1) * MLA_QK_PAD)
        vr = slice(h * MLA_V_DIM, (h + 1) * MLA_V_DIM)
        a_chains.append((lambda rows, kc=kc: k_ref[rows, kc], _transpose_bf16(q_ref[:, kc])))
        b_chains.append(lambda cols, vr=vr: vt_ref[vr, cols])
    outs = _attend_pipelined(a_chains, b_chains, s_ref, m_ref, k_ref.shape[0])
    for h, o in enumerate(outs):
        o_ref[:, h * MLA_V_DIM:(h + 1) * MLA_V_DIM] = o.T.astype(o_ref.dtype)


def _mla_attention(q, k, v, tq):
    b, s, _ = q.shape
    hp = MLA_HEADS_PER_STEP
    nh, dq, dv = MLA_HEADS // hp, hp * MLA_QK_PAD, hp * MLA_V_DIM
    nq = s // tq
    n_tiles = b * nh * nq
    tile_a = lambda t: _flat_tile(jnp.minimum(t, n_tiles - 1), nh, nq)
    tile_b = lambda t: _flat_tile(jnp.maximum(t - 1, 0), nh, nq)

    def q_map(t):
        bi, h, qi = tile_a(t)
        return bi, qi, h

    def k_map(t):
        bi, h, _ = tile_a(t)
        return bi, 0, h

    def v_map(t):
        bi, h, _ = tile_b(t)
        return bi, 0, h

    def o_map(t):
        bi, h, qi = tile_b(t)
        return bi, qi, h

    return pl.pallas_call(
        functools.partial(_mla_attn_kernel, tiles_per_head=nq),
        grid=(n_tiles + 1,),
        in_specs=[pl.BlockSpec((None, tq, dq), q_map), pl.BlockSpec((None, s, dq), k_map),
                  pl.BlockSpec((None, s, dv), v_map)],
        out_specs=pl.BlockSpec((None, tq, dv), o_map),
        out_shape=jax.ShapeDtypeStruct((b, s, nh * dv), BF16),
        scratch_shapes=_attn_scratch(hp, s, tq, dv),
        compiler_params=_cparams("arbitrary"),
        name="mla_attention",
    )(q, k, v)


def _mix_out_kernel(x_ref, g_ref, od_ref, om_ref, wg_ref, wbd_ref, wbm_ref, wo_ref, xo_ref, mg_ref, *, tn):
    od, om = od_ref[...], om_ref[...]
    h = _rms(x_ref[...], g_ref[...], NORM_EPS).astype(BF16)
    n = mg_ref.shape[1]
    for t in range(n // tn):
        cols = slice(t * tn, (t + 1) * tn)
        ga = _sigmoid(_dot(h, wg_ref[:, cols]))
        gb = _sigmoid(_dot(h, wg_ref[:, n + t * tn:n + (t + 1) * tn]))
        mg_ref[:, cols] = (ga * _dot(od, wbd_ref[:, cols]) + gb * _dot(om, wbm_ref[:, cols])).astype(BF16)
    xo_ref[...] = x_ref[...] + _dot(mg_ref[...], wo_ref[...])


def _mix_out(x, g, od, om, w_gates, wbd, wbm, wo, layer, tm, tn):
    m, d = x.shape
    n = wbd.shape[2]
    row = lambda a: pl.BlockSpec((tm, a.shape[1]), lambda i: (i, 0))
    resident = lambda a: pl.BlockSpec((None,) + a.shape[1:], lambda i: (layer, 0, 0), pipeline_mode=pl.Buffered(1))
    return pl.pallas_call(
        functools.partial(_mix_out_kernel, tn=tn),
        grid=(m // tm,),
        in_specs=[row(x), pl.BlockSpec((1, d), lambda i: (0, 0)), row(od), row(om), resident(w_gates), resident(wbd),
                  resident(wbm), resident(wo)],
        out_specs=row(x),
        out_shape=jax.ShapeDtypeStruct((m, d), F32),
        scratch_shapes=[pltpu.VMEM((tm, n), BF16)],
        compiler_params=_cparams("parallel"),
        name="mix_out",
    )(x, g.reshape(1, d), od, om, w_gates, wbd, wbm, wo)


def _ffn_kernel(x_ref, g_ref, wg_ref, wu_ref, wd_ref, xo_ref, h_ref):
    def down_proj(h):
        gate = _dot(h, wg_ref[...])
        act = (gate * _sigmoid(gate) * _dot(h, wu_ref[...])).astype(BF16)
        return _dot(act, wd_ref[...])

    @pl.when(pl.program_id(1) == 0)
    def _():
        x = x_ref[...]
        h = _rms(x, g_ref[...], NORM_EPS).astype(h_ref.dtype)
        h_ref[...] = h
        xo_ref[...] = x + down_proj(h)

    @pl.when(pl.program_id(1) > 0)
    def _():
        xo_ref[...] += down_proj(h_ref[...])


def _ffn(x, g, w_gate_up, w_down, layer, tm, tn):
    m, d = x.shape
    n = w_down.shape[1]
    row = pl.BlockSpec((tm, d), lambda i, j: (i, 0))
    return pl.pallas_call(
        _ffn_kernel,
        grid=(m // tm, n // tn),
        in_specs=[row, pl.BlockSpec((1, d), lambda i, j: (0, 0)),
                  pl.BlockSpec((None, d, tn), lambda i, j: (layer, 0, j)),
                  pl.BlockSpec((None, d, tn), lambda i, j: (layer, 0, j + n // tn)),
                  pl.BlockSpec((None, tn, d), lambda i, j: (layer, j, 0))],
        out_specs=row,
        out_shape=jax.ShapeDtypeStruct((m, d), F32),
        scratch_shapes=[pltpu.VMEM((tm, d), BF16)],
        compiler_params=_cparams("parallel", "arbitrary"),
        name="ffn",
    )(x, g.reshape(1, d), w_gate_up, w_gate_up, w_down)


def _ple_kernel(x_ref, g_ref, p_ref, wg_ref, wp_ref, gf_ref, o_ref, *, final_norm):
    x = x_ref[...]
    gate = _sigmoid(_dot(_rms(x, g_ref[...], NORM_EPS).astype(BF16), wg_ref[...]))
    x_new = x + gate * _dot(p_ref[...].astype(BF16), wp_ref[...])
    o_ref[...] = _rms(x_new, gf_ref[...], NORM_EPS) if final_norm else x_new


def _ple(x, g, p, wg, wp, layer, g_final, final_norm, tm):
    m, d = x.shape
    row = pl.BlockSpec((tm, d), lambda i: (i, 0))
    vec = pl.BlockSpec((1, d), lambda i: (0, 0))
    full = lambda a: pl.BlockSpec((None,) + a.shape[1:], lambda i: (layer, 0, 0), pipeline_mode=pl.Buffered(1))
    return pl.pallas_call(
        functools.partial(_ple_kernel, final_norm=final_norm),
        grid=(m // tm,),
        in_specs=[row, vec, pl.BlockSpec((None, tm, p.shape[2]), lambda i: (layer, i, 0)), full(wg), full(wp), vec],
        out_specs=row,
        out_shape=jax.ShapeDtypeStruct((m, d), F32),
        compiler_params=_cparams("parallel"),
        name="ple_gate",
    )(x, g.reshape(1, d), p, wg, wp, g_final.reshape(1, d))


def _rope_tables(positions, dim, period, pass_through):
    inv_freq = ROPE_THETA ** (-jnp.arange(0, dim, 2, dtype=F32) / dim)
    ang = positions.astype(F32).reshape(-1, 1) * inv_freq
    cos, sin = jnp.cos(ang), jnp.sin(ang)
    m, half = ang.shape
    pad = period - dim
    rest, zero = jnp.full((m, pad), pass_through, F32), jnp.zeros((m, pad), F32)
    zh = jnp.zeros((m, half), F32)
    reps = LANES // period
    c = jnp.tile(jnp.concatenate([cos, cos, rest], axis=1), (1, reps))
    s_lo = jnp.tile(jnp.concatenate([-sin, zh, zero], axis=1), (1, reps))
    s_hi = jnp.tile(jnp.concatenate([zh, sin, zero], axis=1), (1, reps))
    return c, s_lo, s_hi


def _tile(n, pref):
    t = min(n, pref)
    assert n % t == 0, (n, t)
    return t


def kernel(x, p, positions, g_mix, w_in, lambda_q1, lambda_k1, lambda_q2, lambda_k2, g_subln, g_q_latent, w_q_up, g_kv_latent, w_kv_up, w_branch_diff, w_branch_mla, w_out, g_ffn, w_gate_up, w_down, w_ple_in, g_ple, w_ple_gate, g_final):
    b, s, d = x.shape
    depth = w_in.shape[0]
    m = b * s
    tm = _tile(m, 1024)
    tm_small = _tile(m, 512)
    tm_mix = _tile(m, 256)
    tq = _tile(s, 512)
    tn = 512
    assert s % KEY_CHUNK == 0

    tabs_p = _rope_tables(positions, PARTIAL_ROPE_DIM, DIFF_HEAD_DIM, 1.0)
    tabs_m = _rope_tables(positions, MLA_ROPE_DIM, LANES, 0.0)

    xf = x.reshape(m, d)
    lat0 = 3 * DIFF_WIDTH
    gates0 = lat0 + MLA_Q_RANK + MLA_KV_RANK + MLA_ROPE_DIM
    assert gates0 - lat0 <= LATENT_TILE
    qk_dim = MLA_NOPE_DIM + MLA_ROPE_DIM
    w_main = w_in.astype(BF16)
    w_gates = w_main[:, :, gates0:]
    w_bd, w_bm, w_o = w_branch_diff.astype(BF16), w_branch_mla.astype(BF16), w_out.astype(BF16)
    w_gu, w_dn = w_gate_up.astype(BF16), w_down.astype(BF16)
    w_pg, w_pi = w_ple_gate.astype(BF16), w_ple_in.astype(BF16)
    pf = p.reshape(depth, m, -1)
    for i in range(depth):
        lam_init = 0.8 - 0.6 * math.exp(-0.3 * i)
        w_q = jnp.pad(w_q_up[i].reshape(MLA_Q_RANK, MLA_HEADS, qk_dim),
                      ((0, 0), (0, 0), (0, MLA_QK_PAD - qk_dim))).reshape(MLA_Q_RANK, MLA_HEADS * MLA_QK_PAD).astype(BF16)
        w_kv = w_kv_up[i].reshape(MLA_KV_RANK, MLA_HEADS, MLA_NOPE_DIM + MLA_V_DIM)
        w_k = w_kv[:, :, :MLA_NOPE_DIM].reshape(MLA_KV_RANK, MLA_HEADS * MLA_NOPE_DIM).astype(BF16)
        w_v = w_kv[:, :, MLA_NOPE_DIM:].reshape(MLA_KV_RANK, MLA_HEADS * MLA_V_DIM).astype(BF16)
        lam_params = jnp.stack([lambda_q1[i], lambda_k1[i], lambda_q2[i], lambda_k2[i]]).astype(F32)

        qkv, qm, km, vm = _in_proj(xf, g_mix[i], w_main, i, g_q_latent[i].reshape(1, -1), w_q,
                                   g_kv_latent[i].reshape(1, -1), w_k, w_v, tabs_p, tabs_m, tm_small)
        od = _diff_attention(qkv.reshape(b, s, 3 * DIFF_WIDTH), lam_params, g_subln[i], lam_init, tq)
        om = _mla_attention(qm.reshape(b, s, -1), km.reshape(b, s, -1), vm.reshape(b, s, -1), tq)
        xf = _mix_out(xf, g_mix[i], od.reshape(m, -1), om.reshape(m, -1), w_gates, w_bd, w_bm, w_o, i, tm_mix, tn)

        xf = _ffn(xf, g_ffn[i], w_gu, w_dn, i, tm, tn)

        xf = _ple(xf, g_ple[i], pf, w_pg, w_pi, i, g_final, i == depth - 1, tm_small)
    return xf.reshape(b, s, d)
```

```python
import functools
import math

import jax
import jax.numpy as jnp
from jax import lax
from jax.experimental import pallas as pl
from jax.experimental.pallas import tpu as pltpu

F32 = jnp.float32
BF16 = jnp.bfloat16

DIFF_HEADS = 8
DIFF_HEAD_DIM = 64
DIFF_WIDTH = DIFF_HEADS * 2 * DIFF_HEAD_DIM
PARTIAL_ROPE_DIM = DIFF_HEAD_DIM // 4
MLA_HEADS = 8
MLA_Q_RANK = 512
MLA_KV_RANK = 256
MLA_NOPE_DIM = 128
MLA_ROPE_DIM = 64
MLA_V_DIM = 128
MLA_QK_PAD = 256
MLA_HEADS_PER_STEP = 2
ROPE_THETA = 500000.0
NORM_EPS = 1e-6
SUBLN_EPS = 1e-5
LOG2E = math.log2(math.e)
LANES = 128
VMEM_LIMIT_BYTES = 56 * 1024 * 1024
KEY_CHUNK = 256
QKV_SUBTILE = 256

def _cparams(*sem):
    return pltpu.CompilerParams(dimension_semantics=sem, vmem_limit_bytes=VMEM_LIMIT_BYTES)


def _dot(a, b):
    return jnp.dot(a, b, preferred_element_type=F32)


def _sigmoid(x):
    return 1.0 / (1.0 + jnp.exp(-x))


def _rms(x, g, eps):
    return x * lax.rsqrt(jnp.mean(x * x, axis=-1, keepdims=True) + eps) * g


def _transpose_bf16(a):
    return a.astype(F32).T.astype(BF16)


def _rope_chunk(z, c, s_lo, s_hi, half):
    return z * c + pltpu.roll(z, LANES - half, 1) * s_lo + pltpu.roll(z, half, 1) * s_hi


LATENT_TILE = MLA_Q_RANK + MLA_KV_RANK + 2 * LANES


def _in_proj_kernel(x_ref, g_ref, w_ref, gq_ref, wq_ref, gkv_ref, wk_ref, wv_ref, cp_ref, slop_ref, ship_ref,
                    cm_ref, slom_ref, shim_ref, qkv_ref, q_ref, k_ref, v_ref, *, q_scale, mla_scale):
    h = _rms(x_ref[...], g_ref[...], NORM_EPS).astype(BF16)
    plain = (cp_ref[...], slop_ref[...], ship_ref[...])
    scaled = tuple(t * q_scale for t in plain)
    for u in range(3 * DIFF_WIDTH // QKV_SUBTILE):
        z = _dot(h, w_ref[:, u * QKV_SUBTILE:(u + 1) * QKV_SUBTILE])
        part = u * QKV_SUBTILE // DIFF_WIDTH
        for t in range(QKV_SUBTILE // LANES):
            sl = slice(u * QKV_SUBTILE + t * LANES, u * QKV_SUBTILE + (t + 1) * LANES)
            zc = z[:, t * LANES:(t + 1) * LANES]
            if part < 2:
                zc = _rope_chunk(zc, *(scaled if part == 0 else plain), PARTIAL_ROPE_DIM // 2)
            qkv_ref[:, sl] = zc.astype(qkv_ref.dtype)

    lat0 = 3 * DIFF_WIDTH
    z = _dot(h, w_ref[:, lat0:lat0 + LATENT_TILE])
    cq = _rms(z[:, :MLA_Q_RANK], gq_ref[...], NORM_EPS).astype(BF16)
    ckv = _rms(z[:, MLA_Q_RANK:MLA_Q_RANK + MLA_KV_RANK], gkv_ref[...], NORM_EPS).astype(BF16)
    kr_off = MLA_Q_RANK + MLA_KV_RANK
    c, s_lo, s_hi = cm_ref[...], slom_ref[...], shim_ref[...]
    half = MLA_ROPE_DIM // 2
    kr = _rope_chunk(z[:, kr_off:kr_off + LANES], c, s_lo, s_hi, half).astype(BF16)
    qf = _dot(cq, wq_ref[...]) * mla_scale
    kn = _dot(ckv, wk_ref[...])
    for hd in range(MLA_HEADS):
        o = hd * MLA_QK_PAD
        q_ref[:, o:o + LANES] = qf[:, o:o + LANES].astype(BF16)
        q_ref[:, o + LANES:o + 2 * LANES] = _rope_chunk(qf[:, o + LANES:o + 2 * LANES], c, s_lo, s_hi, half).astype(BF16)
        k_ref[:, o:o + LANES] = kn[:, hd * LANES:(hd + 1) * LANES].astype(BF16)
        k_ref[:, o + LANES:o + 2 * LANES] = kr
    v_ref[...] = _dot(ckv, wv_ref[...]).astype(BF16)


def _in_proj(x, g, w_main, layer, gq, wq, gkv, wk, wv, tabs_p, tabs_m, tm):
    m, k = x.shape
    g = g.reshape(1, k)
    once = lambda a: pl.BlockSpec(a.shape, lambda i: (0,) * a.ndim, pipeline_mode=pl.Buffered(1))
    tab_spec = pl.BlockSpec((tm, LANES), lambda i: (i, 0))
    nqkv, nq, nv = 3 * DIFF_WIDTH, MLA_HEADS * MLA_QK_PAD, MLA_HEADS * MLA_V_DIM
    row = lambda n: pl.BlockSpec((tm, n), lambda i: (i, 0))
    return pl.pallas_call(
        functools.partial(_in_proj_kernel, q_scale=DIFF_HEAD_DIM ** -0.5 * LOG2E,
                          mla_scale=(MLA_NOPE_DIM + MLA_ROPE_DIM) ** -0.5 * LOG2E),
        grid=(m // tm,),
        in_specs=[row(k), once(g),
                  pl.BlockSpec((None, k, 3 * DIFF_WIDTH + LATENT_TILE), lambda i: (layer, 0, 0),
                               pipeline_mode=pl.Buffered(1)),
                  once(gq), once(wq), once(gkv), once(wk), once(wv)] + [tab_spec] * 6,
        out_specs=[row(nqkv), row(nq), row(nq), row(nv)],
        out_shape=[jax.ShapeDtypeStruct((m, nqkv), BF16), jax.ShapeDtypeStruct((m, nq), BF16),
                   jax.ShapeDtypeStruct((m, nq), BF16), jax.ShapeDtypeStruct((m, nv), BF16)],
        compiler_params=_cparams("parallel"),
        name="in_proj",
    )(x, g, w_main, gq, wq, gkv, wk, wv, *tabs_p, *tabs_m)


def _attend_pipelined(a_chains, b_chains, s_ref, m_ref, n_keys):
    n = len(a_chains)
    m_prev = [m_ref[u] for u in range(n)]
    acc, l, m_new = [None] * n, [None] * n, [None] * n
    for c in range(n_keys // KEY_CHUNK):
        rows = slice(c * KEY_CHUNK, (c + 1) * KEY_CHUNK)
        for u in range(n):
            p = jnp.exp2(s_ref[u, rows, :] - m_prev[u])
            lc = jnp.sum(p, axis=0, keepdims=True)
            pv = _dot(b_chains[u](rows), p.astype(BF16))
            acc[u] = pv if c == 0 else acc[u] + pv
            l[u] = lc if c == 0 else l[u] + lc
        for u, (load_k, q_t) in enumerate(a_chains):
            sc = _dot(load_k(rows), q_t)
            s_ref[u, rows, :] = sc
            mc = jnp.max(sc, axis=0, keepdims=True)
            m_new[u] = mc if c == 0 else jnp.maximum(m_new[u], mc)
    for u in range(n):
        m_ref[u] = m_new[u]
    return [acc[u] / l[u] for u in range(n)]


def _attn_scratch(n_chains, n_keys, tq, dv):
    return [pltpu.VMEM((dv, n_keys), BF16), pltpu.VMEM((n_chains, n_keys, tq), F32),
            pltpu.VMEM((n_chains, 1, tq), F32)]


def _init_attn_scratch(t, tiles_per_head, v_ref, vt_ref, s_ref, m_ref):
    @pl.when(t == 0)
    def _():
        s_ref[...] = jnp.zeros_like(s_ref)
        m_ref[...] = jnp.zeros_like(m_ref)

    @pl.when(jnp.maximum(t - 1, 0) % tiles_per_head == 0)
    def _():
        vt_ref[...] = _transpose_bf16(v_ref[...])


def _flat_tile(u, n_heads, tiles_per_head):
    return u // (n_heads * tiles_per_head), (u // tiles_per_head) % n_heads, u % tiles_per_head


def _diff_attn_kernel(lam_ref, g_ref, q_ref, k_ref, v_ref, o_ref, vt_ref, s_ref, m_ref, *, lam_init, tiles_per_head):
    _init_attn_scratch(pl.program_id(0), tiles_per_head, v_ref, vt_ref, s_ref, m_ref)
    lp = lam_ref[...]
    lam = (jnp.exp(jnp.sum(lp[0:1] * lp[1:2], axis=-1, keepdims=True))
           - jnp.exp(jnp.sum(lp[2:3] * lp[3:4], axis=-1, keepdims=True)) + lam_init)
    q_t = _transpose_bf16(q_ref[...])
    row = lax.broadcasted_iota(jnp.int32, q_t.shape, 0)
    zero = jnp.zeros_like(q_t)
    load_k = lambda rows: k_ref[rows, :]
    load_vt = lambda cols: vt_ref[:, cols]
    o1, o2 = _attend_pipelined([(load_k, jnp.where(row < DIFF_HEAD_DIM, q_t, zero)),
                                (load_k, jnp.where(row >= DIFF_HEAD_DIM, q_t, zero))],
                               [load_vt, load_vt], s_ref, m_ref, k_ref.shape[0])
    o = (o1 - lam * o2).T
    o_ref[...] = (_rms(o, g_ref[...], SUBLN_EPS) * (1.0 - lam_init)).astype(o_ref.dtype)


def _diff_attention(qkv, lam_params, g_subln, lam_init, tq):
    b, s, _ = qkv.shape
    dv = 2 * DIFF_HEAD_DIM
    nh, nq = DIFF_HEADS, s // tq
    n_tiles = b * nh * nq
    tile_a = lambda t: _flat_tile(jnp.minimum(t, n_tiles - 1), nh, nq)
    tile_b = lambda t: _flat_tile(jnp.maximum(t - 1, 0), nh, nq)

    def q_map(t):
        bi, h, qi = tile_a(t)
        return bi, qi, h

    def k_map(t):
        bi, h, _ = tile_a(t)
        return bi, 0, nh + h

    def v_map(t):
        bi, h, _ = tile_b(t)
        return bi, 0, 2 * nh + h

    def o_map(t):
        bi, h, qi = tile_b(t)
        return bi, qi, h

    return pl.pallas_call(
        functools.partial(_diff_attn_kernel, lam_init=lam_init, tiles_per_head=nq),
        grid=(n_tiles + 1,),
        in_specs=[pl.BlockSpec(lam_params.shape, lambda t: (0, 0)),
                  pl.BlockSpec((1, dv), lambda t: (0, 0)),
                  pl.BlockSpec((None, tq, dv), q_map),
                  pl.BlockSpec((None, s, dv), k_map),
                  pl.BlockSpec((None, s, dv), v_map)],
        out_specs=pl.BlockSpec((None, tq, dv), o_map),
        out_shape=jax.ShapeDtypeStruct((b, s, nh * dv), BF16),
        scratch_shapes=_attn_scratch(2, s, tq, dv),
        compiler_params=_cparams("arbitrary"),
        name="diff_attention",
    )(lam_params, g_subln.reshape(1, dv), qkv, qkv, qkv)


def _mla_attn_kernel(q_ref, k_ref, v_ref, o_ref, vt_ref, s_ref, m_ref, *, tiles_per_head):
    _init_attn_scratch(pl.program_id(0), tiles_per_head, v_ref, vt_ref, s_ref, m_ref)
    a_chains, b_chains = [], []
    for h in range(MLA_HEADS_PER_STEP):
        kc = slice(h * MLA_QK_PAD, (h + 1) * MLA_QK_PAD)
        vr = slice(h * MLA_V_DIM, (h + 1) * MLA_V_DIM)
        a_chains.append((lambda rows, kc=kc: k_ref[rows, kc], _transpose_bf16(q_ref[:, kc])))
        b_chains.append(lambda cols, vr=vr: vt_ref[vr, cols])
    outs = _attend_pipelined(a_chains, b_chains, s_ref, m_ref, k_ref.shape[0])
    for h, o in enumerate(outs):
        o_ref[:, h * MLA_V_DIM:(h + 1) * MLA_V_DIM] = o.T.astype(o_ref.dtype)


def _mla_attention(q, k, v, tq):
    b, s, _ = q.shape
    hp = MLA_HEADS_PER_STEP
    nh, dq, dv = MLA_HEADS // hp, hp * MLA_QK_PAD, hp * MLA_V_DIM
    nq = s // tq
    n_tiles = b * nh * nq
    tile_a = lambda t: _flat_tile(jnp.minimum(t, n_tiles - 1), nh, nq)
    tile_b = lambda t: _flat_tile(jnp.maximum(t - 1, 0), nh, nq)

    def q_map(t):
        bi, h, qi = tile_a(t)
        return bi, qi, h

    def k_map(t):
        bi, h, _ = tile_a(t)
        return bi, 0, h

    def v_map(t):
        bi, h, _ = tile_b(t)
        return bi, 0, h

    def o_map(t):
        bi, h, qi = tile_b(t)
        return bi, qi, h

    return pl.pallas_call(
        functools.partial(_mla_attn_kernel, tiles_per_head=nq),
        grid=(n_tiles + 1,),
        in_specs=[pl.BlockSpec((None, tq, dq), q_map), pl.BlockSpec((None, s, dq), k_map),
                  pl.BlockSpec((None, s, dv), v_map)],
        out_specs=pl.BlockSpec((None, tq, dv), o_map),
        out_shape=jax.ShapeDtypeStruct((b, s, nh * dv), BF16),
        scratch_shapes=_attn_scratch(hp, s, tq, dv),
        compiler_params=_cparams("arbitrary"),
        name="mla_attention",
    )(q, k, v)


def _mix_out_kernel(x_ref, g_ref, od_ref, om_ref, wg_ref, wbd_ref, wbm_ref, wo_ref, xo_ref, mg_ref, *, tn):
    od, om = od_ref[...], om_ref[...]
    h = _rms(x_ref[...], g_ref[...], NORM_EPS).astype(BF16)
    n = mg_ref.shape[1]
    for t in range(n // tn):
        cols = slice(t * tn, (t + 1) * tn)
        ga = _sigmoid(_dot(h, wg_ref[:, cols]))
        gb = _sigmoid(_dot(h, wg_ref[:, n + t * tn:n + (t + 1) * tn]))
        mg_ref[:, cols] = (ga * _dot(od, wbd_ref[:, cols]) + gb * _dot(om, wbm_ref[:, cols])).astype(BF16)
    xo_ref[...] = x_ref[...] + _dot(mg_ref[...], wo_ref[...])


def _mix_out(x, g, od, om, w_gates, wbd, wbm, wo, layer, tm, tn):
    m, d = x.shape
    n = wbd.shape[2]
    row = lambda a: pl.BlockSpec((tm, a.shape[1]), lambda i: (i, 0))
    resident = lambda a: pl.BlockSpec((None,) + a.shape[1:], lambda i: (layer, 0, 0), pipeline_mode=pl.Buffered(1))
    return pl.pallas_call(
        functools.partial(_mix_out_kernel, tn=tn),
        grid=(m // tm,),
        in_specs=[row(x), pl.BlockSpec((1, d), lambda i: (0, 0)), row(od), row(om), resident(w_gates), resident(wbd),
                  resident(wbm), resident(wo)],
        out_specs=row(x),
        out_shape=jax.ShapeDtypeStruct((m, d), F32),
        scratch_shapes=[pltpu.VMEM((tm, n), BF16)],
        compiler_params=_cparams("parallel"),
        name="mix_out",
    )(x, g.reshape(1, d), od, om, w_gates, wbd, wbm, wo)


def _ffn_kernel(x_ref, g_ref, wg_ref, wu_ref, wd_ref, xo_ref, h_ref):
    def down_proj(h):
        gate = _dot(h, wg_ref[...])
        act = (gate * _sigmoid(gate) * _dot(h, wu_ref[...])).astype(BF16)
        return _dot(act, wd_ref[...])

    @pl.when(pl.program_id(1) == 0)
    def _():
        x = x_ref[...]
        h = _rms(x, g_ref[...], NORM_EPS).astype(h_ref.dtype)
        h_ref[...] = h
        xo_ref[...] = x + down_proj(h)

    @pl.when(pl.program_id(1) > 0)
    def _():
        xo_ref[...] += down_proj(h_ref[...])


def _ffn(x, g, w_gate_up, w_down, layer, tm, tn):
    m, d = x.shape
    n = w_down.shape[1]
    row = pl.BlockSpec((tm, d), lambda i, j: (i, 0))
    return pl.pallas_call(
        _ffn_kernel,
        grid=(m // tm, n // tn),
        in_specs=[row, pl.BlockSpec((1, d), lambda i, j: (0, 0)),
                  pl.BlockSpec((None, d, tn), lambda i, j: (layer, 0, j)),
                  pl.BlockSpec((None, d, tn), lambda i, j: (layer, 0, j + n // tn)),
                  pl.BlockSpec((None, tn, d), lambda i, j: (layer, j, 0))],
        out_specs=row,
        out_shape=jax.ShapeDtypeStruct((m, d), F32),
        scratch_shapes=[pltpu.VMEM((tm, d), BF16)],
        compiler_params=_cparams("parallel", "arbitrary"),
        name="ffn",
    )(x, g.reshape(1, d), w_gate_up, w_gate_up, w_down)


def _ple_kernel(x_ref, g_ref, p_ref, wg_ref, wp_ref, gf_ref, o_ref, *, final_norm):
    x = x_ref[...]
    gate = _sigmoid(_dot(_rms(x, g_ref[...], NORM_EPS).astype(BF16), wg_ref[...]))
    x_new = x + gate * _dot(p_ref[...].astype(BF16), wp_ref[...])
    o_ref[...] = _rms(x_new, gf_ref[...], NORM_EPS) if final_norm else x_new


def _ple(x, g, p, wg, wp, layer, g_final, final_norm, tm):
    m, d = x.shape
    row = pl.BlockSpec((tm, d), lambda i: (i, 0))
    vec = pl.BlockSpec((1, d), lambda i: (0, 0))
    full = lambda a: pl.BlockSpec((None,) + a.shape[1:], lambda i: (layer, 0, 0), pipeline_mode=pl.Buffered(1))
    return pl.pallas_call(
        functools.partial(_ple_kernel, final_norm=final_norm),
        grid=(m // tm,),
        in_specs=[row, vec, pl.BlockSpec((None, tm, p.shape[2]), lambda i: (layer, i, 0)), full(wg), full(wp), vec],
        out_specs=row,
        out_shape=jax.ShapeDtypeStruct((m, d), F32),
        compiler_params=_cparams("parallel"),
        name="ple_gate",
    )(x, g.reshape(1, d), p, wg, wp, g_final.reshape(1, d))


def _rope_tables(positions, dim, period, pass_through):
    inv_freq = ROPE_THETA ** (-jnp.arange(0, dim, 2, dtype=F32) / dim)
    ang = inv_freq[:, None] * positions.astype(F32).reshape(1, -1)
    cos, sin = jnp.cos(ang), jnp.sin(ang)
    half, m = ang.shape
    pad = period - dim
    rest, zero = jnp.full((pad, m), pass_through, F32), jnp.zeros((pad, m), F32)
    zh = jnp.zeros((half, m), F32)
    reps = LANES // period
    c = jnp.tile(jnp.concatenate([cos, cos, rest], axis=0), (reps, 1))
    s_lo = jnp.tile(jnp.concatenate([-sin, zh, zero], axis=0), (reps, 1))
    s_hi = jnp.tile(jnp.concatenate([zh, sin, zero], axis=0), (reps, 1))
    return c.T, s_lo.T, s_hi.T


def _tile(n, pref):
    t = min(n, pref)
    assert n % t == 0, (n, t)
    return t


def kernel(x, p, positions, g_mix, w_in, lambda_q1, lambda_k1, lambda_q2, lambda_k2, g_subln, g_q_latent, w_q_up, g_kv_latent, w_kv_up, w_branch_diff, w_branch_mla, w_out, g_ffn, w_gate_up, w_down, w_ple_in, g_ple, w_ple_gate, g_final):
    b, s, d = x.shape
    depth = w_in.shape[0]
    m = b * s
    tm = _tile(m, 1024)
    tm_small = _tile(m, 512)
    tm_mix = _tile(m, 256)
    tq = _tile(s, 512)
    tn = 512
    assert s % KEY_CHUNK == 0

    tabs_p = _rope_tables(positions, PARTIAL_ROPE_DIM, DIFF_HEAD_DIM, 1.0)
    tabs_m = _rope_tables(positions, MLA_ROPE_DIM, LANES, 0.0)

    xf = x.reshape(m, d)
    lat0 = 3 * DIFF_WIDTH
    gates0 = lat0 + MLA_Q_RANK + MLA_KV_RANK + MLA_ROPE_DIM
    assert gates0 - lat0 <= LATENT_TILE
    qk_dim = MLA_NOPE_DIM + MLA_ROPE_DIM
    w_main = w_in.astype(BF16)
    w_gates = w_main[:, :, gates0:]
    w_bd, w_bm, w_o = w_branch_diff.astype(BF16), w_branch_mla.astype(BF16), w_out.astype(BF16)
    w_gu, w_dn = w_gate_up.astype(BF16), w_down.astype(BF16)
    w_pg, w_pi = w_ple_gate.astype(BF16), w_ple_in.astype(BF16)
    pf = p.reshape(depth, m, -1)
    for i in range(depth):
        lam_init = 0.8 - 0.6 * math.exp(-0.3 * i)
        w_q = jnp.pad(w_q_up[i].reshape(MLA_Q_RANK, MLA_HEADS, qk_dim),
                      ((0, 0), (0, 0), (0, MLA_QK_PAD - qk_dim))).reshape(MLA_Q_RANK, MLA_HEADS * MLA_QK_PAD).astype(BF16)
        w_kv = w_kv_up[i].reshape(MLA_KV_RANK, MLA_HEADS, MLA_NOPE_DIM + MLA_V_DIM)
        w_k = w_kv[:, :, :MLA_NOPE_DIM].reshape(MLA_KV_RANK, MLA_HEADS * MLA_NOPE_DIM).astype(BF16)
        w_v = w_kv[:, :, MLA_NOPE_DIM:].reshape(MLA_KV_RANK, MLA_HEADS * MLA_V_DIM).astype(BF16)
        lam_params = jnp.stack([lambda_q1[i], lambda_k1[i], lambda_q2[i], lambda_k2[i]]).astype(F32)

        qkv, qm, km, vm = _in_proj(xf, g_mix[i], w_main, i, g_q_latent[i].reshape(1, -1), w_q,
                                   g_kv_latent[i].reshape(1, -1), w_k, w_v, tabs_p, tabs_m, tm_small)
        od = _diff_attention(qkv.reshape(b, s, 3 * DIFF_WIDTH), lam_params, g_subln[i], lam_init, tq)
        om = _mla_attention(qm.reshape(b, s, -1), km.reshape(b, s, -1), vm.reshape(b, s, -1), tq)
        xf = _mix_out(xf, g_mix[i], od.reshape(m, -1), om.reshape(m, -1), w_gates, w_bd, w_bm, w_o, i, tm_mix, tn)

        xf = _ffn(xf, g_ffn[i], w_gu, w_dn, i, tm, tn)

        xf = _ple(xf, g_ple[i], pf, w_pg, w_pi, i, g_final, i == depth - 1, tm_small)
    return xf.reshape(b, s, d)
```

```python
import functools
import math

import jax
import jax.numpy as jnp
from jax import lax
from jax.experimental import pallas as pl
from jax.experimental.pallas import tpu as pltpu

F32 = jnp.float32
BF16 = jnp.bfloat16

DIFF_HEADS = 8
DIFF_HEAD_DIM = 64
DIFF_WIDTH = DIFF_HEADS * 2 * DIFF_HEAD_DIM
PARTIAL_ROPE_DIM = DIFF_HEAD_DIM // 4
MLA_HEADS = 8
MLA_Q_RANK = 512
MLA_KV_RANK = 256
MLA_NOPE_DIM = 128
MLA_ROPE_DIM = 64
MLA_V_DIM = 128
MLA_QK_PAD = 256
MLA_HEADS_PER_STEP = 2
ROPE_THETA = 500000.0
NORM_EPS = 1e-6
SUBLN_EPS = 1e-5
LOG2E = math.log2(math.e)
LANES = 128
VMEM_LIMIT_BYTES = 56 * 1024 * 1024
KEY_CHUNK = 256
QKV_SUBTILE = 256

def _cparams(*sem):
    return pltpu.CompilerParams(dimension_semantics=sem, vmem_limit_bytes=VMEM_LIMIT_BYTES)


def _dot(a, b):
    return jnp.dot(a, b, preferred_element_type=F32)


def _sigmoid(x):
    return 1.0 / (1.0 + jnp.exp(-x))


def _rms(x, g, eps):
    return x * lax.rsqrt(jnp.mean(x * x, axis=-1, keepdims=True) + eps) * g


def _transpose_bf16(a):
    return a.astype(F32).T.astype(BF16)


def _rope_chunk(z, c, s_lo, s_hi, half):
    return z * c + pltpu.roll(z, LANES - half, 1) * s_lo + pltpu.roll(z, half, 1) * s_hi


LATENT_TILE = MLA_Q_RANK + MLA_KV_RANK + 2 * LANES


def _in_proj_kernel(x_ref, g_ref, w_ref, gq_ref, wq_ref, gkv_ref, wk_ref, wv_ref, cp_ref, slop_ref, ship_ref,
                    cm_ref, slom_ref, shim_ref, qkv_ref, q_ref, k_ref, v_ref, *, q_scale, mla_scale):
    h = _rms(x_ref[...], g_ref[...], NORM_EPS).astype(BF16)
    plain = (cp_ref[...], slop_ref[...], ship_ref[...])
    scaled = tuple(t * q_scale for t in plain)
    for u in range(3 * DIFF_WIDTH // QKV_SUBTILE):
        z = _dot(h, w_ref[:, u * QKV_SUBTILE:(u + 1) * QKV_SUBTILE])
        part = u * QKV_SUBTILE // DIFF_WIDTH
        for t in range(QKV_SUBTILE // LANES):
            sl = slice(u * QKV_SUBTILE + t * LANES, u * QKV_SUBTILE + (t + 1) * LANES)
            zc = z[:, t * LANES:(t + 1) * LANES]
            if part < 2:
                zc = _rope_chunk(zc, *(scaled if part == 0 else plain), PARTIAL_ROPE_DIM // 2)
            qkv_ref[:, sl] = zc.astype(qkv_ref.dtype)

    lat0 = 3 * DIFF_WIDTH
    z = _dot(h, w_ref[:, lat0:lat0 + LATENT_TILE])
    cq = _rms(z[:, :MLA_Q_RANK], gq_ref[...], NORM_EPS).astype(BF16)
    ckv = _rms(z[:, MLA_Q_RANK:MLA_Q_RANK + MLA_KV_RANK], gkv_ref[...], NORM_EPS).astype(BF16)
    kr_off = MLA_Q_RANK + MLA_KV_RANK
    c, s_lo, s_hi = cm_ref[...], slom_ref[...], shim_ref[...]
    half = MLA_ROPE_DIM // 2
    kr = _rope_chunk(z[:, kr_off:kr_off + LANES], c, s_lo, s_hi, half).astype(BF16)
    qf = _dot(cq, wq_ref[...]) * mla_scale
    kn = _dot(ckv, wk_ref[...])
    for hd in range(MLA_HEADS):
        o = hd * MLA_QK_PAD
        q_ref[:, o:o + LANES] = qf[:, o:o + LANES].astype(BF16)
        q_ref[:, o + LANES:o + 2 * LANES] = _rope_chunk(qf[:, o + LANES:o + 2 * LANES], c, s_lo, s_hi, half).astype(BF16)
        k_ref[:, o:o + LANES] = kn[:, hd * LANES:(hd + 1) * LANES].astype(BF16)
        k_ref[:, o + LANES:o + 2 * LANES] = kr
    v_ref[...] = _dot(ckv, wv_ref[...]).astype(BF16)


def _in_proj(x, g, w_main, layer, gq, wq, gkv, wk, wv, tabs_p, tabs_m, tm):
    m, k = x.shape
    g = g.reshape(1, k)
    once = lambda a: pl.BlockSpec(a.shape, lambda i: (0,) * a.ndim, pipeline_mode=pl.Buffered(1))
    tab_spec = pl.BlockSpec((tm, LANES), lambda i: (i, 0))
    nqkv, nq, nv = 3 * DIFF_WIDTH, MLA_HEADS * MLA_QK_PAD, MLA_HEADS * MLA_V_DIM
    row = lambda n: pl.BlockSpec((tm, n), lambda i: (i, 0))
    return pl.pallas_call(
        functools.partial(_in_proj_kernel, q_scale=DIFF_HEAD_DIM ** -0.5 * LOG2E,
                          mla_scale=(MLA_NOPE_DIM + MLA_ROPE_DIM) ** -0.5 * LOG2E),
        grid=(m // tm,),
        in_specs=[row(k), once(g),
                  pl.BlockSpec((None, k, 3 * DIFF_WIDTH + LATENT_TILE), lambda i: (layer, 0, 0),
                               pipeline_mode=pl.Buffered(1)),
                  once(gq), once(wq), once(gkv), once(wk), once(wv)] + [tab_spec] * 6,
        out_specs=[row(nqkv), row(nq), row(nq), row(nv)],
        out_shape=[jax.ShapeDtypeStruct((m, nqkv), BF16), jax.ShapeDtypeStruct((m, nq), BF16),
                   jax.ShapeDtypeStruct((m, nq), BF16), jax.ShapeDtypeStruct((m, nv), BF16)],
        compiler_params=_cparams("parallel"),
        name="in_proj",
    )(x, g, w_main, gq, wq, gkv, wk, wv, *tabs_p, *tabs_m)


def _attend_pipelined(a_chains, b_chains, s_ref, m_ref, n_keys):
    n = len(a_chains)
    m_prev = [m_ref[u] for u in range(n)]
    acc, l, m_new = [None] * n, [None] * n, [None] * n
    for c in range(n_keys // KEY_CHUNK):
        rows = slice(c * KEY_CHUNK, (c + 1) * KEY_CHUNK)
        for u in range(n):
            p = jnp.exp2(s_ref[u, rows, :] - m_prev[u])
            lc = jnp.sum(p, axis=0, keepdims=True)
            pv = _dot(b_chains[u](rows), p.astype(BF16))
            acc[u] = pv if c == 0 else acc[u] + pv
            l[u] = lc if c == 0 else l[u] + lc
        for u, (load_k, q_t) in enumerate(a_chains):
            sc = _dot(load_k(rows), q_t)
            s_ref[u, rows, :] = sc
            mc = jnp.max(sc, axis=0, keepdims=True)
            m_new[u] = mc if c == 0 else jnp.maximum(m_new[u], mc)
    for u in range(n):
        m_ref[u] = m_new[u]
    return [acc[u] / l[u] for u in range(n)]


def _attn_scratch(n_chains, n_keys, tq, dv):
    return [pltpu.VMEM((dv, n_keys), BF16), pltpu.VMEM((n_chains, n_keys, tq), F32),
            pltpu.VMEM((n_chains, 1, tq), F32)]


def _init_attn_scratch(t, tiles_per_head, v_ref, vt_ref, s_ref, m_ref):
    @pl.when(t == 0)
    def _():
        s_ref[...] = jnp.zeros_like(s_ref)
        m_ref[...] = jnp.zeros_like(m_ref)

    @pl.when(_divmod_nonneg(jnp.maximum(t - 1, 0), tiles_per_head)[1] == 0)
    def _():
        vt_ref[...] = _transpose_bf16(v_ref[...])


def _divmod_nonneg(u, n):
    if n & (n - 1) == 0:
        return lax.shift_right_logical(u, n.bit_length() - 1), u & (n - 1)
    return u // n, u % n


def _flat_tile(u, n_heads, tiles_per_head):
    head_flat, qi = _divmod_nonneg(u, tiles_per_head)
    bi, h = _divmod_nonneg(head_flat, n_heads)
    return bi, h, qi


def _diff_attn_kernel(lam_ref, g_ref, q_ref, k_ref, v_ref, o_ref, vt_ref, s_ref, m_ref, *, lam_init, tiles_per_head):
    _init_attn_scratch(pl.program_id(0), tiles_per_head, v_ref, vt_ref, s_ref, m_ref)
    lp = lam_ref[...]
    lam = (jnp.exp(jnp.sum(lp[0:1] * lp[1:2], axis=-1, keepdims=True))
           - jnp.exp(jnp.sum(lp[2:3] * lp[3:4], axis=-1, keepdims=True)) + lam_init)
    q_t = _transpose_bf16(q_ref[...])
    row = lax.broadcasted_iota(jnp.int32, q_t.shape, 0)
    zero = jnp.zeros_like(q_t)
    load_k = lambda rows: k_ref[rows, :]
    load_vt = lambda cols: vt_ref[:, cols]
    o1, o2 = _attend_pipelined([(load_k, jnp.where(row < DIFF_HEAD_DIM, q_t, zero)),
                                (load_k, jnp.where(row >= DIFF_HEAD_DIM, q_t, zero))],
                               [load_vt, load_vt], s_ref, m_ref, k_ref.shape[0])
    o = (o1 - lam * o2).T
    o_ref[...] = (_rms(o, g_ref[...], SUBLN_EPS) * (1.0 - lam_init)).astype(o_ref.dtype)


def _diff_attention(qkv, lam_params, g_subln, lam_init, tq):
    b, s, _ = qkv.shape
    dv = 2 * DIFF_HEAD_DIM
    nh, nq = DIFF_HEADS, s // tq
    n_tiles = b * nh * nq
    tile_a = lambda t: _flat_tile(jnp.minimum(t, n_tiles - 1), nh, nq)
    tile_b = lambda t: _flat_tile(jnp.maximum(t - 1, 0), nh, nq)

    def q_map(t):
        bi, h, qi = tile_a(t)
        return bi, qi, h

    def k_map(t):
        bi, h, _ = tile_a(t)
        return bi, 0, nh + h

    def v_map(t):
        bi, h, _ = tile_b(t)
        return bi, 0, 2 * nh + h

    def o_map(t):
        bi, h, qi = tile_b(t)
        return bi, qi, h

    return pl.pallas_call(
        functools.partial(_diff_attn_kernel, lam_init=lam_init, tiles_per_head=nq),
        grid=(n_tiles + 1,),
        in_specs=[pl.BlockSpec(lam_params.shape, lambda t: (0, 0)),
                  pl.BlockSpec((1, dv), lambda t: (0, 0)),
                  pl.BlockSpec((None, tq, dv), q_map),
                  pl.BlockSpec((None, s, dv), k_map),
                  pl.BlockSpec((None, s, dv), v_map)],
        out_specs=pl.BlockSpec((None, tq, dv), o_map),
        out_shape=jax.ShapeDtypeStruct((b, s, nh * dv), BF16),
        scratch_shapes=_attn_scratch(2, s, tq, dv),
        compiler_params=_cparams("arbitrary"),
        name="diff_attention",
    )(lam_params, g_subln.reshape(1, dv), qkv, qkv, qkv)


def _mla_attn_kernel(q_ref, k_ref, v_ref, o_ref, vt_ref, s_ref, m_ref, *, tiles_per_head):
    _init_attn_scratch(pl.program_id(0), tiles_per_head, v_ref, vt_ref, s_ref, m_ref)
    a_chains, b_chains = [], []
    for h in range(MLA_HEADS_PER_STEP):
        kc = slice(h * MLA_QK_PAD, (h + 1) * MLA_QK_PAD)
        vr = slice(h * MLA_V_DIM, (h + 1) * MLA_V_DIM)
        a_chains.append((lambda rows, kc=kc: k_ref[rows, kc], _transpose_bf16(q_ref[:, kc])))
        b_chains.append(lambda cols, vr=vr: vt_ref[vr, cols])
    outs = _attend_pipelined(a_chains, b_chains, s_ref, m_ref, k_ref.shape[0])
    for h, o in enumerate(outs):
        o_ref[:, h * MLA_V_DIM:(h + 1) * MLA_V_DIM] = o.T.astype(o_ref.dtype)


def _mla_attention(q, k, v, tq):
    b, s, _ = q.shape
    hp = MLA_HEADS_PER_STEP
    nh, dq, dv = MLA_HEADS // hp, hp * MLA_QK_PAD, hp * MLA_V_DIM
    nq = s // tq
    n_tiles = b * nh * nq
    tile_a = lambda t: _flat_tile(jnp.minimum(t, n_tiles - 1), nh, nq)
    tile_b = lambda t: _flat_tile(jnp.maximum(t - 1, 0), nh, nq)

    def q_map(t):
        bi, h, qi = tile_a(t)
        return bi, qi, h

    def k_map(t):
        bi, h, _ = tile_a(t)
        return bi, 0, h

    def v_map(t):
        bi, h, _ = tile_b(t)
        return bi, 0, h

    def o_map(t):
        bi, h, qi = tile_b(t)
        return bi, qi, h

    return pl.pallas_call(
        functools.partial(_mla_attn_kernel, tiles_per_head=nq),
        grid=(n_tiles + 1,),
        in_specs=[pl.BlockSpec((None, tq, dq), q_map), pl.BlockSpec((None, s, dq), k_map),
                  pl.BlockSpec((None, s, dv), v_map)],
        out_specs=pl.BlockSpec((None, tq, dv), o_map),
        out_shape=jax.ShapeDtypeStruct((b, s, nh * dv), BF16),
        scratch_shapes=_attn_scratch(hp, s, tq, dv),
        compiler_params=_cparams("arbitrary"),
        name="mla_attention",
    )(q, k, v)


def _mix_out_kernel(x_ref, g_ref, od_ref, om_ref, wg_ref, wbd_ref, wbm_ref, wo_ref, xo_ref, mg_ref, *, tn):
    od, om = od_ref[...], om_ref[...]
    h = _rms(x_ref[...], g_ref[...], NORM_EPS).astype(BF16)
    n = mg_ref.shape[1]
    for t in range(n // tn):
        cols = slice(t * tn, (t + 1) * tn)
        ga = _sigmoid(_dot(h, wg_ref[:, cols]))
        gb = _sigmoid(_dot(h, wg_ref[:, n + t * tn:n + (t + 1) * tn]))
        mg_ref[:, cols] = (ga * _dot(od, wbd_ref[:, cols]) + gb * _dot(om, wbm_ref[:, cols])).astype(BF16)
    xo_ref[...] = x_ref[...] + _dot(mg_ref[...], wo_ref[...])


def _mix_out(x, g, od, om, w_gates, wbd, wbm, wo, layer, tm, tn):
    m, d = x.shape
    n = wbd.shape[2]
    row = lambda a: pl.BlockSpec((tm, a.shape[1]), lambda i: (i, 0))
    resident = lambda a: pl.BlockSpec((None,) + a.shape[1:], lambda i: (layer, 0, 0), pipeline_mode=pl.Buffered(1))
    return pl.pallas_call(
        functools.partial(_mix_out_kernel, tn=tn),
        grid=(m // tm,),
        in_specs=[row(x), pl.BlockSpec((1, d), lambda i: (0, 0)), row(od), row(om), resident(w_gates), resident(wbd),
                  resident(wbm), resident(wo)],
        out_specs=row(x),
        out_shape=jax.ShapeDtypeStruct((m, d), F32),
        scratch_shapes=[pltpu.VMEM((tm, n), BF16)],
        compiler_params=_cparams("parallel"),
        name="mix_out",
    )(x, g.reshape(1, d), od, om, w_gates, wbd, wbm, wo)


def _ffn_kernel(x_ref, g_ref, wg_ref, wu_ref, wd_ref, xo_ref, h_ref):
    def down_proj(h):
        gate = _dot(h, wg_ref[...])
        act = (gate * _sigmoid(gate) * _dot(h, wu_ref[...])).astype(BF16)
        return _dot(act, wd_ref[...])

    @pl.when(pl.program_id(1) == 0)
    def _():
        x = x_ref[...]
        h = _rms(x, g_ref[...], NORM_EPS).astype(h_ref.dtype)
        h_ref[...] = h
        xo_ref[...] = x + down_proj(h)

    @pl.when(pl.program_id(1) > 0)
    def _():
        xo_ref[...] += down_proj(h_ref[...])


def _ffn(x, g, w_gate_up, w_down, layer, tm, tn):
    m, d = x.shape
    n = w_down.shape[1]
    row = pl.BlockSpec((tm, d), lambda i, j: (i, 0))
    return pl.pallas_call(
        _ffn_kernel,
        grid=(m // tm, n // tn),
        in_specs=[row, pl.BlockSpec((1, d), lambda i, j: (0, 0)),
                  pl.BlockSpec((None, d, tn), lambda i, j: (layer, 0, j)),
                  pl.BlockSpec((None, d, tn), lambda i, j: (layer, 0, j + n // tn)),
                  pl.BlockSpec((None, tn, d), lambda i, j: (layer, j, 0))],
        out_specs=row,
        out_shape=jax.ShapeDtypeStruct((m, d), F32),
        scratch_shapes=[pltpu.VMEM((tm, d), BF16)],
        compiler_params=_cparams("parallel", "arbitrary"),
        name="ffn",
    )(x, g.reshape(1, d), w_gate_up, w_gate_up, w_down)


def _ple_kernel(x_ref, g_ref, p_ref, wg_ref, wp_ref, gf_ref, o_ref, *, final_norm):
    x = x_ref[...]
    gate = _sigmoid(_dot(_rms(x, g_ref[...], NORM_EPS).astype(BF16), wg_ref[...]))
    x_new = x + gate * _dot(p_ref[...].astype(BF16), wp_ref[...])
    o_ref[...] = _rms(x_new, gf_ref[...], NORM_EPS) if final_norm else x_new


def _ple(x, g, p, wg, wp, layer, g_final, final_norm, tm):
    m, d = x.shape
    row = pl.BlockSpec((tm, d), lambda i: (i, 0))
    vec = pl.BlockSpec((1, d), lambda i: (0, 0))
    full = lambda a: pl.BlockSpec((None,) + a.shape[1:], lambda i: (layer, 0, 0), pipeline_mode=pl.Buffered(1))
    return pl.pallas_call(
        functools.partial(_ple_kernel, final_norm=final_norm),
        grid=(m // tm,),
        in_specs=[row, vec, pl.BlockSpec((None, tm, p.shape[2]), lambda i: (layer, i, 0)), full(wg), full(wp), vec],
        out_specs=row,
        out_shape=jax.ShapeDtypeStruct((m, d), F32),
        compiler_params=_cparams("parallel"),
        name="ple_gate",
    )(x, g.reshape(1, d), p, wg, wp, g_final.reshape(1, d))


def _rope_tables(positions, dim, period, pass_through):
    inv_freq = ROPE_THETA ** (-jnp.arange(0, dim, 2, dtype=F32) / dim)
    ang = inv_freq[:, None] * positions.astype(F32).reshape(1, -1)
    cos, sin = jnp.cos(ang), jnp.sin(ang)
    half, m = ang.shape
    pad = period - dim
    rest, zero = jnp.full((pad, m), pass_through, F32), jnp.zeros((pad, m), F32)
    zh = jnp.zeros((half, m), F32)
    reps = LANES // period
    c = jnp.tile(jnp.concatenate([cos, cos, rest], axis=0), (reps, 1))
    s_lo = jnp.tile(jnp.concatenate([-sin, zh, zero], axis=0), (reps, 1))
    s_hi = jnp.tile(jnp.concatenate([zh, sin, zero], axis=0), (reps, 1))
    return c.T, s_lo.T, s_hi.T


def _tile(n, pref):
    t = min(n, pref)
    assert n % t == 0, (n, t)
    return t


def kernel(x, p, positions, g_mix, w_in, lambda_q1, lambda_k1, lambda_q2, lambda_k2, g_subln, g_q_latent, w_q_up, g_kv_latent, w_kv_up, w_branch_diff, w_branch_mla, w_out, g_ffn, w_gate_up, w_down, w_ple_in, g_ple, w_ple_gate, g_final):
    b, s, d = x.shape
    depth = w_in.shape[0]
    m = b * s
    tm = _tile(m, 1024)
    tm_small = _tile(m, 512)
    tm_mix = _tile(m, 256)
    tq = _tile(s, 512)
    tn = 512
    assert s % KEY_CHUNK == 0

    tabs_p = _rope_tables(positions, PARTIAL_ROPE_DIM, DIFF_HEAD_DIM, 1.0)
    tabs_m = _rope_tables(positions, MLA_ROPE_DIM, LANES, 0.0)

    xf = x.reshape(m, d)
    lat0 = 3 * DIFF_WIDTH
    gates0 = lat0 + MLA_Q_RANK + MLA_KV_RANK + MLA_ROPE_DIM
    assert gates0 - lat0 <= LATENT_TILE
    qk_dim = MLA_NOPE_DIM + MLA_ROPE_DIM
    w_main = w_in[:, :, :lat0 + LATENT_TILE].astype(BF16)
    w_gates = lax.reduce_precision(w_in[:, :, gates0:], exponent_bits=8, mantissa_bits=7).astype(BF16)
    w_bd, w_bm, w_o = w_branch_diff.astype(BF16), w_branch_mla.astype(BF16), w_out.astype(BF16)
    w_gu, w_dn = w_gate_up.astype(BF16), w_down.astype(BF16)
    w_pg, w_pi = w_ple_gate.astype(BF16), w_ple_in.astype(BF16)
    pf = p.reshape(depth, m, -1)
    for i in range(depth):
        lam_init = 0.8 - 0.6 * math.exp(-0.3 * i)
        w_q = jnp.pad(w_q_up[i].reshape(MLA_Q_RANK, MLA_HEADS, qk_dim),
                      ((0, 0), (0, 0), (0, MLA_QK_PAD - qk_dim))).reshape(MLA_Q_RANK, MLA_HEADS * MLA_QK_PAD).astype(BF16)
        w_kv = w_kv_up[i].reshape(MLA_KV_RANK, MLA_HEADS, MLA_NOPE_DIM + MLA_V_DIM)
        w_k = w_kv[:, :, :MLA_NOPE_DIM].reshape(MLA_KV_RANK, MLA_HEADS * MLA_NOPE_DIM).astype(BF16)
        w_v = w_kv[:, :, MLA_NOPE_DIM:].reshape(MLA_KV_RANK, MLA_HEADS * MLA_V_DIM).astype(BF16)
        lam_params = jnp.stack([lambda_q1[i], lambda_k1[i], lambda_q2[i], lambda_k2[i]]).astype(F32)

        qkv, qm, km, vm = _in_proj(xf, g_mix[i], w_main, i, g_q_latent[i].reshape(1, -1), w_q,
                                   g_kv_latent[i].reshape(1, -1), w_k, w_v, tabs_p, tabs_m, tm_small)
        od = _diff_attention(qkv.reshape(b, s, 3 * DIFF_WIDTH), lam_params, g_subln[i], lam_init, tq)
        om = _mla_attention(qm.reshape(b, s, -1), km.reshape(b, s, -1), vm.reshape(b, s, -1), tq)
        xf = _mix_out(xf, g_mix[i], od.reshape(m, -1), om.reshape(m, -1), w_gates, w_bd, w_bm, w_o, i, tm_mix, tn)

        xf = _ffn(xf, g_ffn[i], w_gu, w_dn, i, tm, tn)

        xf = _ple(xf, g_ple[i], pf, w_pg, w_pi, i, g_final, i == depth - 1, tm_small)
    return xf.reshape(b, s, d)
```

```python
import functools
import math

import jax
import jax.numpy as jnp
from jax import lax
from jax.experimental import pallas as pl
from jax.experimental.pallas import tpu as pltpu

F32 = jnp.float32
BF16 = jnp.bfloat16

DIFF_HEADS = 8
DIFF_HEAD_DIM = 64
DIFF_WIDTH = DIFF_HEADS * 2 * DIFF_HEAD_DIM
PARTIAL_ROPE_DIM = DIFF_HEAD_DIM // 4
MLA_HEADS = 8
MLA_Q_RANK = 512
MLA_KV_RANK = 256
MLA_NOPE_DIM = 128
MLA_ROPE_DIM = 64
MLA_V_DIM = 128
MLA_QK_PAD = 256
MLA_HEADS_PER_STEP = 2
ROPE_THETA = 500000.0
NORM_EPS = 1e-6
SUBLN_EPS = 1e-5
LOG2E = math.log2(math.e)
LANES = 128
VMEM_LIMIT_BYTES = 56 * 1024 * 1024
KEY_CHUNK = 256
QKV_SUBTILE = 256

def _cparams(*sem):
    return pltpu.CompilerParams(dimension_semantics=sem, vmem_limit_bytes=VMEM_LIMIT_BYTES)


def _dot(a, b):
    return jnp.dot(a, b, preferred_element_type=F32)


def _sigmoid(x):
    return 1.0 / (1.0 + jnp.exp(-x))


def _rms(x, g, eps):
    return x * lax.rsqrt(jnp.mean(x * x, axis=-1, keepdims=True) + eps) * g


def _transpose_bf16(a):
    return a.astype(F32).T.astype(BF16)


def _rope_chunk(z, c, s_lo, s_hi, half):
    return z * c + pltpu.roll(z, LANES - half, 1) * s_lo + pltpu.roll(z, half, 1) * s_hi


LATENT_TILE = MLA_Q_RANK + MLA_KV_RANK + 2 * LANES


def _in_proj_kernel(x_ref, g_ref, w_ref, gq_ref, wq_ref, gkv_ref, wk_ref, wv_ref, cp_ref, slop_ref, ship_ref,
                    cm_ref, slom_ref, shim_ref, qkv_ref, q_ref, k_ref, v_ref, *, q_scale, mla_scale):
    h = _rms(x_ref[...], g_ref[...], NORM_EPS).astype(BF16)
    plain = (cp_ref[...], slop_ref[...], ship_ref[...])
    scaled = tuple(t * q_scale for t in plain)
    for u in range(3 * DIFF_WIDTH // QKV_SUBTILE):
        z = _dot(h, w_ref[:, u * QKV_SUBTILE:(u + 1) * QKV_SUBTILE])
        part = u * QKV_SUBTILE // DIFF_WIDTH
        for t in range(QKV_SUBTILE // LANES):
            sl = slice(u * QKV_SUBTILE + t * LANES, u * QKV_SUBTILE + (t + 1) * LANES)
            zc = z[:, t * LANES:(t + 1) * LANES]
            if part < 2:
                zc = _rope_chunk(zc, *(scaled if part == 0 else plain), PARTIAL_ROPE_DIM // 2)
            qkv_ref[:, sl] = zc.astype(qkv_ref.dtype)

    lat0 = 3 * DIFF_WIDTH
    z = _dot(h, w_ref[:, lat0:lat0 + LATENT_TILE])
    cq = _rms(z[:, :MLA_Q_RANK], gq_ref[...], NORM_EPS).astype(BF16)
    ckv = _rms(z[:, MLA_Q_RANK:MLA_Q_RANK + MLA_KV_RANK], gkv_ref[...], NORM_EPS).astype(BF16)
    kr_off = MLA_Q_RANK + MLA_KV_RANK
    c, s_lo, s_hi = cm_ref[...], slom_ref[...], shim_ref[...]
    half = MLA_ROPE_DIM // 2
    kr = _rope_chunk(z[:, kr_off:kr_off + LANES], c, s_lo, s_hi, half).astype(BF16)
    qf = _dot(cq, wq_ref[...]) * mla_scale
    kn = _dot(ckv, wk_ref[...])
    for hd in range(MLA_HEADS):
        o = hd * MLA_QK_PAD
        q_ref[:, o:o + LANES] = qf[:, o:o + LANES].astype(BF16)
        q_ref[:, o + LANES:o + 2 * LANES] = _rope_chunk(qf[:, o + LANES:o + 2 * LANES], c, s_lo, s_hi, half).astype(BF16)
        k_ref[:, o:o + LANES] = kn[:, hd * LANES:(hd + 1) * LANES].astype(BF16)
        k_ref[:, o + LANES:o + 2 * LANES] = kr
    v_ref[...] = _dot(ckv, wv_ref[...]).astype(BF16)


def _in_proj(x, g, w_main, layer, gq, wq, gkv, wk, wv, tabs_p, tabs_m, tm):
    m, k = x.shape
    g = g.reshape(1, k)
    once = lambda a: pl.BlockSpec(a.shape, lambda i: (0,) * a.ndim, pipeline_mode=pl.Buffered(1))
    tab_spec = pl.BlockSpec((tm, LANES), lambda i: (i, 0))
    nqkv, nq, nv = 3 * DIFF_WIDTH, MLA_HEADS * MLA_QK_PAD, MLA_HEADS * MLA_V_DIM
    row = lambda n: pl.BlockSpec((tm, n), lambda i: (i, 0))
    return pl.pallas_call(
        functools.partial(_in_proj_kernel, q_scale=DIFF_HEAD_DIM ** -0.5 * LOG2E,
                          mla_scale=(MLA_NOPE_DIM + MLA_ROPE_DIM) ** -0.5 * LOG2E),
        grid=(m // tm,),
        in_specs=[row(k), once(g),
                  pl.BlockSpec((None, k, 3 * DIFF_WIDTH + LATENT_TILE), lambda i: (layer, 0, 0),
                               pipeline_mode=pl.Buffered(1)),
                  once(gq), once(wq), once(gkv), once(wk), once(wv)] + [tab_spec] * 6,
        out_specs=[row(nqkv), row(nq), row(nq), row(nv)],
        out_shape=[jax.ShapeDtypeStruct((m, nqkv), BF16), jax.ShapeDtypeStruct((m, nq), BF16),
                   jax.ShapeDtypeStruct((m, nq), BF16), jax.ShapeDtypeStruct((m, nv), BF16)],
        compiler_params=_cparams("parallel"),
        name="in_proj",
    )(x, g, w_main, gq, wq, gkv, wk, wv, *tabs_p, *tabs_m)


def _attend_pipelined(a_chains, b_chains, s_ref, m_ref, n_keys):
    n = len(a_chains)
    m_prev = [m_ref[u] for u in range(n)]
    acc, l, m_new = [None] * n, [None] * n, [None] * n
    for c in range(n_keys // KEY_CHUNK):
        rows = slice(c * KEY_CHUNK, (c + 1) * KEY_CHUNK)
        for u in range(n):
            p = jnp.exp2(s_ref[u, rows, :] - m_prev[u])
            lc = jnp.sum(p, axis=0, keepdims=True)
            pv = _dot(b_chains[u](rows), p.astype(BF16))
            acc[u] = pv if c == 0 else acc[u] + pv
            l[u] = lc if c == 0 else l[u] + lc
        for u, (load_k, q_t) in enumerate(a_chains):
            sc = _dot(load_k(rows), q_t)
            s_ref[u, rows, :] = sc
            mc = jnp.max(sc, axis=0, keepdims=True)
            m_new[u] = mc if c == 0 else jnp.maximum(m_new[u], mc)
    for u in range(n):
        m_ref[u] = m_new[u]
    return [acc[u] / l[u] for u in range(n)]


def _attn_scratch(n_chains, n_keys, tq, dv):
    return [pltpu.VMEM((dv, n_keys), BF16), pltpu.VMEM((n_chains, n_keys, tq), F32),
            pltpu.VMEM((n_chains, 1, tq), F32)]


def _init_attn_scratch(t, tiles_per_head, v_ref, vt_ref, s_ref, m_ref):
    @pl.when(t == 0)
    def _():
        s_ref[...] = jnp.zeros_like(s_ref)
        m_ref[...] = jnp.zeros_like(m_ref)

    @pl.when(_divmod_nonneg(jnp.maximum(t - 1, 0), tiles_per_head)[1] == 0)
    def _():
        vt_ref[...] = _transpose_bf16(v_ref[...])


def _divmod_nonneg(u, n):
    if n & (n - 1) == 0:
        return lax.shift_right_logical(u, n.bit_length() - 1), u & (n - 1)
    return u // n, u % n


def _flat_tile(u, n_heads, tiles_per_head):
    head_flat, qi = _divmod_nonneg(u, tiles_per_head)
    bi, h = _divmod_nonneg(head_flat, n_heads)
    return bi, h, qi


def _diff_attn_kernel(lam_ref, g_ref, q_ref, k_ref, v_ref, o_ref, vt_ref, s_ref, m_ref, *, lam_init, tiles_per_head):
    _init_attn_scratch(pl.program_id(0), tiles_per_head, v_ref, vt_ref, s_ref, m_ref)
    lp = lam_ref[...]
    lam = (jnp.exp(jnp.sum(lp[0:1] * lp[1:2], axis=-1, keepdims=True))
           - jnp.exp(jnp.sum(lp[2:3] * lp[3:4], axis=-1, keepdims=True)) + lam_init)
    q_t = _transpose_bf16(q_ref[...])
    row = lax.broadcasted_iota(jnp.int32, q_t.shape, 0)
    zero = jnp.zeros_like(q_t)
    load_k = lambda rows: k_ref[rows, :]
    load_vt = lambda cols: vt_ref[:, cols]
    o1, o2 = _attend_pipelined([(load_k, jnp.where(row < DIFF_HEAD_DIM, q_t, zero)),
                                (load_k, jnp.where(row >= DIFF_HEAD_DIM, q_t, zero))],
                               [load_vt, load_vt], s_ref, m_ref, k_ref.shape[0])
    o = (o1 - lam * o2).T
    o_ref[...] = (_rms(o, g_ref[...], SUBLN_EPS) * (1.0 - lam_init)).astype(o_ref.dtype)


def _diff_attention(qkv, lam_params, g_subln, lam_init, tq):
    b, s, _ = qkv.shape
    dv = 2 * DIFF_HEAD_DIM
    nh, nq = DIFF_HEADS, s // tq
    n_tiles = b * nh * nq
    tile_a = lambda t: _flat_tile(jnp.minimum(t, n_tiles - 1), nh, nq)
    tile_b = lambda t: _flat_tile(jnp.maximum(t - 1, 0), nh, nq)

    def q_map(t):
        bi, h, qi = tile_a(t)
        return bi, qi, h

    def k_map(t):
        bi, h, _ = tile_a(t)
        return bi, 0, nh + h

    def v_map(t):
        bi, h, _ = tile_b(t)
        return bi, 0, 2 * nh + h

    def o_map(t):
        bi, h, qi = tile_b(t)
        return bi, qi, h

    return pl.pallas_call(
        functools.partial(_diff_attn_kernel, lam_init=lam_init, tiles_per_head=nq),
        grid=(n_tiles + 1,),
        in_specs=[pl.BlockSpec(lam_params.shape, lambda t: (0, 0)),
                  pl.BlockSpec((1, dv), lambda t: (0, 0)),
                  pl.BlockSpec((None, tq, dv), q_map),
                  pl.BlockSpec((None, s, dv), k_map),
                  pl.BlockSpec((None, s, dv), v_map)],
        out_specs=pl.BlockSpec((None, tq, dv), o_map),
        out_shape=jax.ShapeDtypeStruct((b, s, nh * dv), BF16),
        scratch_shapes=_attn_scratch(2, s, tq, dv),
        compiler_params=_cparams("arbitrary"),
        name="diff_attention",
    )(lam_params, g_subln.reshape(1, dv), qkv, qkv, qkv)


def _mla_attn_kernel(q_ref, k_ref, v_ref, o_ref, vt_ref, s_ref, m_ref, *, tiles_per_head):
    _init_attn_scratch(pl.program_id(0), tiles_per_head, v_ref, vt_ref, s_ref, m_ref)
    a_chains, b_chains = [], []
    for h in range(MLA_HEADS_PER_STEP):
        kc = slice(h * MLA_QK_PAD, (h + 1) * MLA_QK_PAD)
        vr = slice(h * MLA_V_DIM, (h + 1) * MLA_V_DIM)
        a_chains.append((lambda rows, kc=kc: k_ref[rows, kc], _transpose_bf16(q_ref[:, kc])))
        b_chains.append(lambda cols, vr=vr: vt_ref[vr, cols])
    outs = _attend_pipelined(a_chains, b_chains, s_ref, m_ref, k_ref.shape[0])
    for h, o in enumerate(outs):
        o_ref[:, h * MLA_V_DIM:(h + 1) * MLA_V_DIM] = o.T.astype(o_ref.dtype)


def _mla_attention(q, k, v, tq):
    b, s, _ = q.shape
    hp = MLA_HEADS_PER_STEP
    nh, dq, dv = MLA_HEADS // hp, hp * MLA_QK_PAD, hp * MLA_V_DIM
    nq = s // tq
    n_tiles = b * nh * nq
    tile_a = lambda t: _flat_tile(jnp.minimum(t, n_tiles - 1), nh, nq)
    tile_b = lambda t: _flat_tile(jnp.maximum(t - 1, 0), nh, nq)

    def q_map(t):
        bi, h, qi = tile_a(t)
        return bi, qi, h

    def k_map(t):
        bi, h, _ = tile_a(t)
        return bi, 0, h

    def v_map(t):
        bi, h, _ = tile_b(t)
        return bi, 0, h

    def o_map(t):
        bi, h, qi = tile_b(t)
        return bi, qi, h

    return pl.pallas_call(
        functools.partial(_mla_attn_kernel, tiles_per_head=nq),
        grid=(n_tiles + 1,),
        in_specs=[pl.BlockSpec((None, tq, dq), q_map), pl.BlockSpec((None, s, dq), k_map),
                  pl.BlockSpec((None, s, dv), v_map)],
        out_specs=pl.BlockSpec((None, tq, dv), o_map),
        out_shape=jax.ShapeDtypeStruct((b, s, nh * dv), BF16),
        scratch_shapes=_attn_scratch(hp, s, tq, dv),
        compiler_params=_cparams("arbitrary"),
        name="mla_attention",
    )(q, k, v)


def _mix_out_kernel(x_ref, g_ref, od_ref, om_ref, wg_ref, wbd_ref, wbm_ref, wo_ref, xo_ref, mg_ref, *, tn):
    od, om = od_ref[...], om_ref[...]
    h = _rms(x_ref[...], g_ref[...], NORM_EPS).astype(BF16)
    n = mg_ref.shape[1]
    for t in range(n // tn):
        cols = slice(t * tn, (t + 1) * tn)
        ga = _sigmoid(_dot(h, wg_ref[:, cols]))
        gb = _sigmoid(_dot(h, wg_ref[:, n + t * tn:n + (t + 1) * tn]))
        mg_ref[:, cols] = (ga * _dot(od, wbd_ref[:, cols]) + gb * _dot(om, wbm_ref[:, cols])).astype(BF16)
    xo_ref[...] = x_ref[...] + _dot(mg_ref[...], wo_ref[...])


def _mix_out(x, g, od, om, w_gates, wbd, wbm, wo, layer, tm, tn):
    m, d = x.shape
    n = wbd.shape[2]
    row = lambda a: pl.BlockSpec((tm, a.shape[1]), lambda i: (i, 0))
    resident = lambda a: pl.BlockSpec((None,) + a.shape[1:], lambda i: (layer, 0, 0), pipeline_mode=pl.Buffered(1))
    return pl.pallas_call(
        functools.partial(_mix_out_kernel, tn=tn),
        grid=(m // tm,),
        in_specs=[row(x), pl.BlockSpec((1, d), lambda i: (0, 0)), row(od), row(om), resident(w_gates), resident(wbd),
                  resident(wbm), resident(wo)],
        out_specs=row(x),
        out_shape=jax.ShapeDtypeStruct((m, d), F32),
        scratch_shapes=[pltpu.VMEM((tm, n), BF16)],
        compiler_params=_cparams("parallel"),
        name="mix_out",
    )(x, g.reshape(1, d), od, om, w_gates, wbd, wbm, wo)


def _ffn_kernel(x_ref, g_ref, wg_ref, wu_ref, wd_ref, xo_ref, h_ref):
    def down_proj(h):
        gate = _dot(h, wg_ref[...])
        act = (gate * _sigmoid(gate) * _dot(h, wu_ref[...])).astype(BF16)
        return _dot(act, wd_ref[...])

    @pl.when(pl.program_id(1) == 0)
    def _():
        x = x_ref[...]
        h = _rms(x, g_ref[...], NORM_EPS).astype(h_ref.dtype)
        h_ref[...] = h
        xo_ref[...] = x + down_proj(h)

    @pl.when(pl.program_id(1) > 0)
    def _():
        xo_ref[...] += down_proj(h_ref[...])


def _ffn(x, g, w_gate_up, w_down, layer, tm, tn):
    m, d = x.shape
    n = w_down.shape[1]
    row = pl.BlockSpec((tm, d), lambda i, j: (i, 0))
    return pl.pallas_call(
        _ffn_kernel,
        grid=(m // tm, n // tn),
        in_specs=[row, pl.BlockSpec((1, d), lambda i, j: (0, 0)),
                  pl.BlockSpec((None, d, tn), lambda i, j: (layer, 0, j)),
                  pl.BlockSpec((None, d, tn), lambda i, j: (layer, 0, j + n // tn)),
                  pl.BlockSpec((None, tn, d), lambda i, j: (layer, j, 0))],
        out_specs=row,
        out_shape=jax.ShapeDtypeStruct((m, d), F32),
        scratch_shapes=[pltpu.VMEM((tm, d), BF16)],
        compiler_params=_cparams("parallel", "arbitrary"),
        name="ffn",
    )(x, g.reshape(1, d), w_gate_up, w_gate_up, w_down)


def _ple_kernel(x_ref, g_ref, p_ref, wg_ref, wp_ref, gf_ref, o_ref, *, final_norm):
    x = x_ref[...]
    gate = _sigmoid(_dot(_rms(x, g_ref[...], NORM_EPS).astype(BF16), wg_ref[...]))
    x_new = x + gate * _dot(p_ref[...].astype(BF16), wp_ref[...])
    o_ref[...] = _rms(x_new, gf_ref[...], NORM_EPS) if final_norm else x_new


def _ple(x, g, p, wg, wp, layer, g_final, final_norm, tm):
    m, d = x.shape
    row = pl.BlockSpec((tm, d), lambda i: (i, 0))
    vec = pl.BlockSpec((1, d), lambda i: (0, 0))
    full = lambda a: pl.BlockSpec((None,) + a.shape[1:], lambda i: (layer, 0, 0), pipeline_mode=pl.Buffered(1))
    return pl.pallas_call(
        functools.partial(_ple_kernel, final_norm=final_norm),
        grid=(m // tm,),
        in_specs=[row, vec, pl.BlockSpec((None, tm, p.shape[2]), lambda i: (layer, i, 0)), full(wg), full(wp), vec],
        out_specs=row,
        out_shape=jax.ShapeDtypeStruct((m, d), F32),
        compiler_params=_cparams("parallel"),
        name="ple_gate",
    )(x, g.reshape(1, d), p, wg, wp, g_final.reshape(1, d))


def _rope_tables(positions, dim, period, pass_through):
    inv_freq = ROPE_THETA ** (-jnp.arange(0, dim, 2, dtype=F32) / dim)
    ang = inv_freq[:, None] * positions.astype(F32).reshape(1, -1)
    cos, sin = jnp.cos(ang), jnp.sin(ang)
    half, m = ang.shape
    pad = period - dim
    rest, zero = jnp.full((pad, m), pass_through, F32), jnp.zeros((pad, m), F32)
    zh = jnp.zeros((half, m), F32)
    reps = LANES // period
    c = jnp.tile(jnp.concatenate([cos, cos, rest], axis=0), (reps, 1))
    s_lo = jnp.tile(jnp.concatenate([-sin, zh, zero], axis=0), (reps, 1))
    s_hi = jnp.tile(jnp.concatenate([zh, sin, zero], axis=0), (reps, 1))
    return c.T, s_lo.T, s_hi.T


def _tile(n, pref):
    t = min(n, pref)
    assert n % t == 0, (n, t)
    return t


def kernel(x, p, positions, g_mix, w_in, lambda_q1, lambda_k1, lambda_q2, lambda_k2, g_subln, g_q_latent, w_q_up, g_kv_latent, w_kv_up, w_branch_diff, w_branch_mla, w_out, g_ffn, w_gate_up, w_down, w_ple_in, g_ple, w_ple_gate, g_final):
    b, s, d = x.shape
    depth = w_in.shape[0]
    m = b * s
    tm = _tile(m, 1024)
    tm_small = _tile(m, 512)
    tm_mix = _tile(m, 256)
    tq = _tile(s, 512)
    tn = 512
    assert s % KEY_CHUNK == 0

    tabs_p = _rope_tables(positions, PARTIAL_ROPE_DIM, DIFF_HEAD_DIM, 1.0)
    tabs_m = _rope_tables(positions, MLA_ROPE_DIM, LANES, 0.0)

    xf = x.reshape(m, d)
    lat0 = 3 * DIFF_WIDTH
    gates0 = lat0 + MLA_Q_RANK + MLA_KV_RANK + MLA_ROPE_DIM
    assert gates0 - lat0 <= LATENT_TILE
    qk_dim = MLA_NOPE_DIM + MLA_ROPE_DIM
    w_main = w_in.astype(BF16)
    w_gates = w_main[:, :, gates0:]
    w_bd, w_bm, w_o = w_branch_diff.astype(BF16), w_branch_mla.astype(BF16), w_out.astype(BF16)
    w_gu, w_dn = w_gate_up.astype(BF16), w_down.astype(BF16)
    w_pg, w_pi = w_ple_gate.astype(BF16), w_ple_in.astype(BF16)
    pf = p.reshape(depth, m, -1)
    for i in range(depth):
        lam_init = 0.8 - 0.6 * math.exp(-0.3 * i)
        w_q = jnp.pad(w_q_up[i].reshape(MLA_Q_RANK, MLA_HEADS, qk_dim),
                      ((0, 0), (0, 0), (0, MLA_QK_PAD - qk_dim))).reshape(MLA_Q_RANK, MLA_HEADS * MLA_QK_PAD).astype(BF16)
        w_kv = w_kv_up[i].reshape(MLA_KV_RANK, MLA_HEADS, MLA_NOPE_DIM + MLA_V_DIM)
        w_k = w_kv[:, :, :MLA_NOPE_DIM].reshape(MLA_KV_RANK, MLA_HEADS * MLA_NOPE_DIM).astype(BF16)
        w_v = w_kv[:, :, MLA_NOPE_DIM:].reshape(MLA_KV_RANK, MLA_HEADS * MLA_V_DIM).astype(BF16)
        lam_params = jnp.stack([lambda_q1[i], lambda_k1[i], lambda_q2[i], lambda_k2[i]]).astype(F32)

        qkv, qm, km, vm = _in_proj(xf, g_mix[i], w_main, i, g_q_latent[i].reshape(1, -1), w_q,
                                   g_kv_latent[i].reshape(1, -1), w_k, w_v, tabs_p, tabs_m, tm_small)
        od = _diff_attention(qkv.reshape(b, s, 3 * DIFF_WIDTH), lam_params, g_subln[i], lam_init, tq)
        om = _mla_attention(qm.reshape(b, s, -1), km.reshape(b, s, -1), vm.reshape(b, s, -1), tq)
        xf = _mix_out(xf, g_mix[i], od.reshape(m, -1), om.reshape(m, -1), w_gates, w_bd, w_bm, w_o, i, tm_mix, tn)

        xf = _ffn(xf, g_ffn[i], w_gu, w_dn, i, tm, tn)

        xf = _ple(xf, g_ple[i], pf, w_pg, w_pi, i, g_final, i == depth - 1, tm_small)
    return xf.reshape(b, s, d)
```

```python
import functools
import math

import jax
import jax.numpy as jnp
from jax import lax
from jax.experimental import pallas as pl
from jax.experimental.pallas import tpu as pltpu

F32 = jnp.float32
BF16 = jnp.bfloat16

DIFF_HEADS = 8
DIFF_HEAD_DIM = 64
DIFF_WIDTH = DIFF_HEADS * 2 * DIFF_HEAD_DIM
PARTIAL_ROPE_DIM = DIFF_HEAD_DIM // 4
MLA_HEADS = 8
MLA_Q_RANK = 512
MLA_KV_RANK = 256
MLA_NOPE_DIM = 128
MLA_ROPE_DIM = 64
MLA_V_DIM = 128
MLA_QK_PAD = 256
MLA_HEADS_PER_STEP = 2
ROPE_THETA = 500000.0
NORM_EPS = 1e-6
SUBLN_EPS = 1e-5
LOG2E = math.log2(math.e)
LANES = 128
SUBLANES = 8
VMEM_LIMIT_BYTES = 56 * 1024 * 1024
KEY_CHUNK = 256
QKV_SUBTILE = 256

def _cparams(*sem):
    return pltpu.CompilerParams(dimension_semantics=sem, vmem_limit_bytes=VMEM_LIMIT_BYTES)


def _dot(a, b):
    return jnp.dot(a, b, preferred_element_type=F32)


def _sigmoid(x):
    return 1.0 / (1.0 + jnp.exp(-x))


def _rms(x, g, eps):
    return x * lax.rsqrt(jnp.mean(x * x, axis=-1, keepdims=True) + eps) * g


def _transpose_bf16(a):
    return a.astype(F32).T.astype(BF16)


def _rope_chunk(z, c, s_lo, s_hi, half):
    return z * c + pltpu.roll(z, LANES - half, 1) * s_lo + pltpu.roll(z, half, 1) * s_hi


LATENT_TILE = MLA_Q_RANK + MLA_KV_RANK + 2 * LANES


def _in_proj_kernel(x_ref, g_ref, w_ref, gq_ref, wq_ref, gkv_ref, wk_ref, wv_ref, cp_ref, slop_ref, ship_ref,
                    cm_ref, slom_ref, shim_ref, qkv_ref, q_ref, k_ref, v_ref, *, q_scale, mla_scale):
    h = _rms(x_ref[...], g_ref[...], NORM_EPS).astype(BF16)
    plain = (cp_ref[...], slop_ref[...], ship_ref[...])
    scaled = tuple(t * q_scale for t in plain)
    for u in range(3 * DIFF_WIDTH // QKV_SUBTILE):
        z = _dot(h, w_ref[:, u * QKV_SUBTILE:(u + 1) * QKV_SUBTILE])
        part = u * QKV_SUBTILE // DIFF_WIDTH
        for t in range(QKV_SUBTILE // LANES):
            sl = slice(u * QKV_SUBTILE + t * LANES, u * QKV_SUBTILE + (t + 1) * LANES)
            zc = z[:, t * LANES:(t + 1) * LANES]
            if part < 2:
                zc = _rope_chunk(zc, *(scaled if part == 0 else plain), PARTIAL_ROPE_DIM // 2)
            qkv_ref[:, sl] = zc.astype(qkv_ref.dtype)

    lat0 = 3 * DIFF_WIDTH
    z = _dot(h, w_ref[:, lat0:lat0 + LATENT_TILE])
    cq = _rms(z[:, :MLA_Q_RANK], gq_ref[...], NORM_EPS).astype(BF16)
    ckv = _rms(z[:, MLA_Q_RANK:MLA_Q_RANK + MLA_KV_RANK], gkv_ref[...], NORM_EPS).astype(BF16)
    kr_off = MLA_Q_RANK + MLA_KV_RANK
    c, s_lo, s_hi = cm_ref[...], slom_ref[...], shim_ref[...]
    half = MLA_ROPE_DIM // 2
    kr = _rope_chunk(z[:, kr_off:kr_off + LANES], c, s_lo, s_hi, half).astype(BF16)
    qf = _dot(cq, wq_ref[...]) * mla_scale
    kn = _dot(ckv, wk_ref[...])
    for hd in range(MLA_HEADS):
        o = hd * MLA_QK_PAD
        q_ref[:, o:o + LANES] = qf[:, o:o + LANES].astype(BF16)
        q_ref[:, o + LANES:o + 2 * LANES] = _rope_chunk(qf[:, o + LANES:o + 2 * LANES], c, s_lo, s_hi, half).astype(BF16)
        k_ref[:, o:o + LANES] = kn[:, hd * LANES:(hd + 1) * LANES].astype(BF16)
        k_ref[:, o + LANES:o + 2 * LANES] = kr
    v_ref[...] = _dot(ckv, wv_ref[...]).astype(BF16)


def _in_proj(x, g, w_main, layer, gq, wq, gkv, wk, wv, tabs_p, tabs_m, tm):
    m, k = x.shape
    g = g.reshape(1, k)
    once = lambda a: pl.BlockSpec(a.shape, lambda i: (0,) * a.ndim, pipeline_mode=pl.Buffered(1))
    tab_spec = pl.BlockSpec((tm, LANES), lambda i: (i, 0))
    nqkv, nq, nv = 3 * DIFF_WIDTH, MLA_HEADS * MLA_QK_PAD, MLA_HEADS * MLA_V_DIM
    row = lambda n: pl.BlockSpec((tm, n), lambda i: (i, 0))
    return pl.pallas_call(
        functools.partial(_in_proj_kernel, q_scale=DIFF_HEAD_DIM ** -0.5 * LOG2E,
                          mla_scale=(MLA_NOPE_DIM + MLA_ROPE_DIM) ** -0.5 * LOG2E),
        grid=(m // tm,),
        in_specs=[row(k), once(g),
                  pl.BlockSpec((None, k, 3 * DIFF_WIDTH + LATENT_TILE), lambda i: (layer, 0, 0),
                               pipeline_mode=pl.Buffered(1)),
                  once(gq), once(wq), once(gkv), once(wk), once(wv)] + [tab_spec] * 6,
        out_specs=[row(nqkv), row(nq), row(nq), row(nv)],
        out_shape=[jax.ShapeDtypeStruct((m, nqkv), BF16), jax.ShapeDtypeStruct((m, nq), BF16),
                   jax.ShapeDtypeStruct((m, nq), BF16), jax.ShapeDtypeStruct((m, nv), BF16)],
        compiler_params=_cparams("parallel"),
        name="in_proj",
    )(x, g, w_main, gq, wq, gkv, wk, wv, *tabs_p, *tabs_m)


def _attend_pipelined(a_chains, b_chains, s_ref, m_ref, n_keys):
    n = len(a_chains)
    m_prev = [m_ref[u] for u in range(n)]
    acc, l, m_new = [None] * n, [None] * n, [None] * n
    for c in range(n_keys // KEY_CHUNK):
        rows = slice(c * KEY_CHUNK, (c + 1) * KEY_CHUNK)
        for u in range(n):
            p = jnp.exp2(s_ref[u, rows, :] - m_prev[u])
            lc = jnp.sum(p.reshape(-1, SUBLANES, p.shape[1]), axis=0)
            pv = _dot(b_chains[u](rows), p.astype(BF16))
            acc[u] = pv if c == 0 else acc[u] + pv
            l[u] = lc if c == 0 else l[u] + lc
        for u, (load_k, q_t) in enumerate(a_chains):
            sc = _dot(load_k(rows), q_t)
            s_ref[u, rows, :] = sc
            mc = jnp.max(sc.reshape(-1, SUBLANES, sc.shape[1]), axis=0)
            m_new[u] = mc if c == 0 else jnp.maximum(m_new[u], mc)
    for u in range(n):
        m_ref[u] = jnp.max(m_new[u], axis=0, keepdims=True)
    return [acc[u] / jnp.sum(l[u], axis=0, keepdims=True) for u in range(n)]


def _attn_scratch(n_chains, n_keys, tq, dv):
    return [pltpu.VMEM((dv, n_keys), BF16), pltpu.VMEM((n_chains, n_keys, tq), F32),
            pltpu.VMEM((n_chains, 1, tq), F32)]


def _init_attn_scratch(t, tiles_per_head, v_ref, vt_ref, s_ref, m_ref):
    @pl.when(t == 0)
    def _():
        s_ref[...] = jnp.zeros_like(s_ref)
        m_ref[...] = jnp.zeros_like(m_ref)

    @pl.when(_divmod_nonneg(jnp.maximum(t - 1, 0), tiles_per_head)[1] == 0)
    def _():
        vt_ref[...] = _transpose_bf16(v_ref[...])


def _divmod_nonneg(u, n):
    if n & (n - 1) == 0:
        return lax.shift_right_logical(u, n.bit_length() - 1), u & (n - 1)
    return u // n, u % n


def _flat_tile(u, n_heads, tiles_per_head):
    head_flat, qi = _divmod_nonneg(u, tiles_per_head)
    bi, h = _divmod_nonneg(head_flat, n_heads)
    return bi, h, qi


def _diff_attn_kernel(lam_ref, g_ref, q_ref, k_ref, v_ref, o_ref, vt_ref, s_ref, m_ref, *, lam_init, tiles_per_head):
    _init_attn_scratch(pl.program_id(0), tiles_per_head, v_ref, vt_ref, s_ref, m_ref)
    lp = lam_ref[...]
    lam = (jnp.exp(jnp.sum(lp[0:1] * lp[1:2], axis=-1, keepdims=True))
           - jnp.exp(jnp.sum(lp[2:3] * lp[3:4], axis=-1, keepdims=True)) + lam_init)
    q_t = _transpose_bf16(q_ref[...])
    row = lax.broadcasted_iota(jnp.int32, q_t.shape, 0)
    zero = jnp.zeros_like(q_t)
    load_k = lambda rows: k_ref[rows, :]
    load_vt = lambda cols: vt_ref[:, cols]
    o1, o2 = _attend_pipelined([(load_k, jnp.where(row < DIFF_HEAD_DIM, q_t, zero)),
                                (load_k, jnp.where(row >= DIFF_HEAD_DIM, q_t, zero))],
                               [load_vt, load_vt], s_ref, m_ref, k_ref.shape[0])
    o = (o1 - lam * o2).T
    o_ref[...] = (_rms(o, g_ref[...], SUBLN_EPS) * (1.0 - lam_init)).astype(o_ref.dtype)


def _diff_attention(qkv, lam_params, g_subln, lam_init, tq):
    b, s, _ = qkv.shape
    dv = 2 * DIFF_HEAD_DIM
    nh, nq = DIFF_HEADS, s // tq
    n_tiles = b * nh * nq
    tile_a = lambda t: _flat_tile(jnp.minimum(t, n_tiles - 1), nh, nq)
    tile_b = lambda t: _flat_tile(jnp.maximum(t - 1, 0), nh, nq)

    def q_map(t):
        bi, h, qi = tile_a(t)
        return bi, qi, h

    def k_map(t):
        bi, h, _ = tile_a(t)
        return bi, 0, nh + h

    def v_map(t):
        bi, h, _ = tile_b(t)
        return bi, 0, 2 * nh + h

    def o_map(t):
        bi, h, qi = tile_b(t)
        return bi, qi, h

    return pl.pallas_call(
        functools.partial(_diff_attn_kernel, lam_init=lam_init, tiles_per_head=nq),
        grid=(n_tiles + 1,),
        in_specs=[pl.BlockSpec(lam_params.shape, lambda t: (0, 0)),
                  pl.BlockSpec((1, dv), lambda t: (0, 0)),
                  pl.BlockSpec((None, tq, dv), q_map),
                  pl.BlockSpec((None, s, dv), k_map),
                  pl.BlockSpec((None, s, dv), v_map)],
        out_specs=pl.BlockSpec((None, tq, dv), o_map),
        out_shape=jax.ShapeDtypeStruct((b, s, nh * dv), BF16),
        scratch_shapes=_attn_scratch(2, s, tq, dv),
        compiler_params=_cparams("arbitrary"),
        name="diff_attention",
    )(lam_params, g_subln.reshape(1, dv), qkv, qkv, qkv)


def _mla_attn_kernel(q_ref, k_ref, v_ref, o_ref, vt_ref, s_ref, m_ref, *, tiles_per_head):
    _init_attn_scratch(pl.program_id(0), tiles_per_head, v_ref, vt_ref, s_ref, m_ref)
    a_chains, b_chains = [], []
    for h in range(MLA_HEADS_PER_STEP):
        kc = slice(h * MLA_QK_PAD, (h + 1) * MLA_QK_PAD)
        vr = slice(h * MLA_V_DIM, (h + 1) * MLA_V_DIM)
        a_chains.append((lambda rows, kc=kc: k_ref[rows, kc], _transpose_bf16(q_ref[:, kc])))
        b_chains.append(lambda cols, vr=vr: vt_ref[vr, cols])
    outs = _attend_pipelined(a_chains, b_chains, s_ref, m_ref, k_ref.shape[0])
    for h, o in enumerate(outs):
        o_ref[:, h * MLA_V_DIM:(h + 1) * MLA_V_DIM] = o.T.astype(o_ref.dtype)


def _mla_attention(q, k, v, tq):
    b, s, _ = q.shape
    hp = MLA_HEADS_PER_STEP
    nh, dq, dv = MLA_HEADS // hp, hp * MLA_QK_PAD, hp * MLA_V_DIM
    nq = s // tq
    n_tiles = b * nh * nq
    tile_a = lambda t: _flat_tile(jnp.minimum(t, n_tiles - 1), nh, nq)
    tile_b = lambda t: _flat_tile(jnp.maximum(t - 1, 0), nh, nq)

    def q_map(t):
        bi, h, qi = tile_a(t)
        return bi, qi, h

    def k_map(t):
        bi, h, _ = tile_a(t)
        return bi, 0, h

    def v_map(t):
        bi, h, _ = tile_b(t)
        return bi, 0, h

    def o_map(t):
        bi, h, qi = tile_b(t)
        return bi, qi, h

    return pl.pallas_call(
        functools.partial(_mla_attn_kernel, tiles_per_head=nq),
        grid=(n_tiles + 1,),
        in_specs=[pl.BlockSpec((None, tq, dq), q_map), pl.BlockSpec((None, s, dq), k_map),
                  pl.BlockSpec((None, s, dv), v_map)],
        out_specs=pl.BlockSpec((None, tq, dv), o_map),
        out_shape=jax.ShapeDtypeStruct((b, s, nh * dv), BF16),
        scratch_shapes=_attn_scratch(hp, s, tq, dv),
        compiler_params=_cparams("arbitrary"),
        name="mla_attention",
    )(q, k, v)


def _mix_out_kernel(x_ref, g_ref, od_ref, om_ref, wg_ref, wbd_ref, wbm_ref, wo_ref, xo_ref, mg_ref, *, tn):
    od, om = od_ref[...], om_ref[...]
    h = _rms(x_ref[...], g_ref[...], NORM_EPS).astype(BF16)
    n = mg_ref.shape[1]
    for t in range(n // tn):
        cols = slice(t * tn, (t + 1) * tn)
        ga = _sigmoid(_dot(h, wg_ref[:, cols]))
        gb = _sigmoid(_dot(h, wg_ref[:, n + t * tn:n + (t + 1) * tn]))
        mg_ref[:, cols] = (ga * _dot(od, wbd_ref[:, cols]) + gb * _dot(om, wbm_ref[:, cols])).astype(BF16)
    xo_ref[...] = x_ref[...] + _dot(mg_ref[...], wo_ref[...])


def _mix_out(x, g, od, om, w_gates, wbd, wbm, wo, layer, tm, tn):
    m, d = x.shape
    n = wbd.shape[2]
    row = lambda a: pl.BlockSpec((tm, a.shape[1]), lambda i: (i, 0))
    resident = lambda a: pl.BlockSpec((None,) + a.shape[1:], lambda i: (layer, 0, 0), pipeline_mode=pl.Buffered(1))
    return pl.pallas_call(
        functools.partial(_mix_out_kernel, tn=tn),
        grid=(m // tm,),
        in_specs=[row(x), pl.BlockSpec((1, d), lambda i: (0, 0)), row(od), row(om), resident(w_gates), resident(wbd),
                  resident(wbm), resident(wo)],
        out_specs=row(x),
        out_shape=jax.ShapeDtypeStruct((m, d), F32),
        scratch_shapes=[pltpu.VMEM((tm, n), BF16)],
        compiler_params=_cparams("parallel"),
        name="mix_out",
    )(x, g.reshape(1, d), od, om, w_gates, wbd, wbm, wo)


def _ffn_kernel(x_ref, g_ref, wg_ref, wu_ref, wd_ref, xo_ref, h_ref):
    def down_proj(h):
        gate = _dot(h, wg_ref[...])
        act = (gate * _sigmoid(gate) * _dot(h, wu_ref[...])).astype(BF16)
        return _dot(act, wd_ref[...])

    @pl.when(pl.program_id(1) == 0)
    def _():
        x = x_ref[...]
        h = _rms(x, g_ref[...], NORM_EPS).astype(h_ref.dtype)
        h_ref[...] = h
        xo_ref[...] = x + down_proj(h)

    @pl.when(pl.program_id(1) > 0)
    def _():
        xo_ref[...] += down_proj(h_ref[...])


def _ffn(x, g, w_gate_up, w_down, layer, tm, tn):
    m, d = x.shape
    n = w_down.shape[1]
    row = pl.BlockSpec((tm, d), lambda i, j: (i, 0))
    return pl.pallas_call(
        _ffn_kernel,
        grid=(m // tm, n // tn),
        in_specs=[row, pl.BlockSpec((1, d), lambda i, j: (0, 0)),
                  pl.BlockSpec((None, d, tn), lambda i, j: (layer, 0, j)),
                  pl.BlockSpec((None, d, tn), lambda i, j: (layer, 0, j + n // tn)),
                  pl.BlockSpec((None, tn, d), lambda i, j: (layer, j, 0))],
        out_specs=row,
        out_shape=jax.ShapeDtypeStruct((m, d), F32),
        scratch_shapes=[pltpu.VMEM((tm, d), BF16)],
        compiler_params=_cparams("parallel", "arbitrary"),
        name="ffn",
    )(x, g.reshape(1, d), w_gate_up, w_gate_up, w_down)


def _ple_kernel(x_ref, g_ref, p_ref, wg_ref, wp_ref, gf_ref, o_ref, *, final_norm):
    x = x_ref[...]
    gate = _sigmoid(_dot(_rms(x, g_ref[...], NORM_EPS).astype(BF16), wg_ref[...]))
    x_new = x + gate * _dot(p_ref[...].astype(BF16), wp_ref[...])
    o_ref[...] = _rms(x_new, gf_ref[...], NORM_EPS) if final_norm else x_new


def _ple(x, g, p, wg, wp, layer, g_final, final_norm, tm):
    m, d = x.shape
    row = pl.BlockSpec((tm, d), lambda i: (i, 0))
    vec = pl.BlockSpec((1, d), lambda i: (0, 0))
    full = lambda a: pl.BlockSpec((None,) + a.shape[1:], lambda i: (layer, 0, 0), pipeline_mode=pl.Buffered(1))
    return pl.pallas_call(
        functools.partial(_ple_kernel, final_norm=final_norm),
        grid=(m // tm,),
        in_specs=[row, vec, pl.BlockSpec((None, tm, p.shape[2]), lambda i: (layer, i, 0)), full(wg), full(wp), vec],
        out_specs=row,
        out_shape=jax.ShapeDtypeStruct((m, d), F32),
        compiler_params=_cparams("parallel"),
        name="ple_gate",
    )(x, g.reshape(1, d), p, wg, wp, g_final.reshape(1, d))


def _rope_tables(positions, dim, period, pass_through):
    inv_freq = ROPE_THETA ** (-jnp.arange(0, dim, 2, dtype=F32) / dim)
    ang = inv_freq[:, None] * positions.astype(F32).reshape(1, -1)
    cos, sin = jnp.cos(ang), jnp.sin(ang)
    half, m = ang.shape
    pad = period - dim
    rest, zero = jnp.full((pad, m), pass_through, F32), jnp.zeros((pad, m), F32)
    zh = jnp.zeros((half, m), F32)
    reps = LANES // period
    c = jnp.tile(jnp.concatenate([cos, cos, rest], axis=0), (reps, 1))
    s_lo = jnp.tile(jnp.concatenate([-sin, zh, zero], axis=0), (reps, 1))
    s_hi = jnp.tile(jnp.concatenate([zh, sin, zero], axis=0), (reps, 1))
    return c.T, s_lo.T, s_hi.T


def _tile(n, pref):
    t = min(n, pref)
    assert n % t == 0, (n, t)
    return t


def kernel(x, p, positions, g_mix, w_in, lambda_q1, lambda_k1, lambda_q2, lambda_k2, g_subln, g_q_latent, w_q_up, g_kv_latent, w_kv_up, w_branch_diff, w_branch_mla, w_out, g_ffn, w_gate_up, w_down, w_ple_in, g_ple, w_ple_gate, g_final):
    b, s, d = x.shape
    depth = w_in.shape[0]
    m = b * s
    tm = _tile(m, 1024)
    tm_small = _tile(m, 512)
    tm_mix = _tile(m, 256)
    tq = _tile(s, 512)
    tn = 512
    assert s % KEY_CHUNK == 0

    tabs_p = _rope_tables(positions, PARTIAL_ROPE_DIM, DIFF_HEAD_DIM, 1.0)
    tabs_m = _rope_tables(positions, MLA_ROPE_DIM, LANES, 0.0)

    xf = x.reshape(m, d)
    lat0 = 3 * DIFF_WIDTH
    gates0 = lat0 + MLA_Q_RANK + MLA_KV_RANK + MLA_ROPE_DIM
    assert gates0 - lat0 <= LATENT_TILE
    qk_dim = MLA_NOPE_DIM + MLA_ROPE_DIM
    w_main = w_in.astype(BF16)
    w_gates = w_main[:, :, gates0:]
    w_bd, w_bm, w_o = w_branch_diff.astype(BF16), w_branch_mla.astype(BF16), w_out.astype(BF16)
    w_gu, w_dn = w_gate_up.astype(BF16), w_down.astype(BF16)
    w_pg, w_pi = w_ple_gate.astype(BF16), w_ple_in.astype(BF16)
    pf = p.reshape(depth, m, -1)
    for i in range(depth):
        lam_init = 0.8 - 0.6 * math.exp(-0.3 * i)
        w_q = jnp.pad(w_q_up[i].reshape(MLA_Q_RANK, MLA_HEADS, qk_dim),
                      ((0, 0), (0, 0), (0, MLA_QK_PAD - qk_dim))).reshape(MLA_Q_RANK, MLA_HEADS * MLA_QK_PAD).astype(BF16)
        w_kv = w_kv_up[i].reshape(MLA_KV_RANK, MLA_HEADS, MLA_NOPE_DIM + MLA_V_DIM)
        w_k = w_kv[:, :, :MLA_NOPE_DIM].reshape(MLA_KV_RANK, MLA_HEADS * MLA_NOPE_DIM).astype(BF16)
        w_v = w_kv[:, :, MLA_NOPE_DIM:].reshape(MLA_KV_RANK, MLA_HEADS * MLA_V_DIM).astype(BF16)
        lam_params = jnp.stack([lambda_q1[i], lambda_k1[i], lambda_q2[i], lambda_k2[i]]).astype(F32)

        qkv, qm, km, vm = _in_proj(xf, g_mix[i], w_main, i, g_q_latent[i].reshape(1, -1), w_q,
                                   g_kv_latent[i].reshape(1, -1), w_k, w_v, tabs_p, tabs_m, tm_small)
        od = _diff_attention(qkv.reshape(b, s, 3 * DIFF_WIDTH), lam_params, g_subln[i], lam_init, tq)
        om = _mla_attention(qm.reshape(b, s, -1), km.reshape(b, s, -1), vm.reshape(b, s, -1), tq)
        xf = _mix_out(xf, g_mix[i], od.reshape(m, -1), om.reshape(m, -1), w_gates, w_bd, w_bm, w_o, i, tm_mix, tn)

        xf = _ffn(xf, g_ffn[i], w_gu, w_dn, i, tm, tn)

        xf = _ple(xf, g_ple[i], pf, w_pg, w_pi, i, g_final, i == depth - 1, tm_small)
    return xf.reshape(b, s, d)
```

```python
import functools
import math

import jax
import jax.numpy as jnp
from jax import lax
from jax.experimental import pallas as pl
from jax.experimental.pallas import tpu as pltpu

F32 = jnp.float32
BF16 = jnp.bfloat16

DIFF_HEADS = 8
DIFF_HEAD_DIM = 64
DIFF_WIDTH = DIFF_HEADS * 2 * DIFF_HEAD_DIM
PARTIAL_ROPE_DIM = DIFF_HEAD_DIM // 4
MLA_HEADS = 8
MLA_Q_RANK = 512
MLA_KV_RANK = 256
MLA_NOPE_DIM = 128
MLA_ROPE_DIM = 64
MLA_V_DIM = 128
MLA_QK_PAD = 256
MLA_HEADS_PER_STEP = 2
ROPE_THETA = 500000.0
NORM_EPS = 1e-6
SUBLN_EPS = 1e-5
LOG2E = math.log2(math.e)
LANES = 128
VMEM_LIMIT_BYTES = 56 * 1024 * 1024
KEY_CHUNK = 256
QKV_SUBTILE = 256

def _cparams(*sem):
    return pltpu.CompilerParams(dimension_semantics=sem, vmem_limit_bytes=VMEM_LIMIT_BYTES)


def _dot(a, b):
    return jnp.dot(a, b, preferred_element_type=F32)


def _sigmoid(x):
    return 1.0 / (1.0 + jnp.exp(-x))


def _rms(x, g, eps):
    return x * lax.rsqrt(jnp.mean(x * x, axis=-1, keepdims=True) + eps) * g


def _transpose_bf16(a):
    return a.astype(F32).T.astype(BF16)


def _rope_chunk(z, c, s_lo, s_hi, half):
    return z * c + pltpu.roll(z, LANES - half, 1) * s_lo + pltpu.roll(z, half, 1) * s_hi


LATENT_TILE = MLA_Q_RANK + MLA_KV_RANK + 2 * LANES


def _in_proj_kernel(x_ref, g_ref, w_ref, gq_ref, wq_ref, gkv_ref, wk_ref, wv_ref, cp_ref, slop_ref, ship_ref,
                    cm_ref, slom_ref, shim_ref, qkv_ref, q_ref, k_ref, v_ref, *, q_scale, mla_scale):
    h = _rms(x_ref[...], g_ref[...], NORM_EPS).astype(BF16)
    plain = (cp_ref[...], slop_ref[...], ship_ref[...])
    scaled = tuple(t * q_scale for t in plain)
    for u in range(3 * DIFF_WIDTH // QKV_SUBTILE):
        z = _dot(h, w_ref[:, u * QKV_SUBTILE:(u + 1) * QKV_SUBTILE])
        part = u * QKV_SUBTILE // DIFF_WIDTH
        for t in range(QKV_SUBTILE // LANES):
            sl = slice(u * QKV_SUBTILE + t * LANES, u * QKV_SUBTILE + (t + 1) * LANES)
            zc = z[:, t * LANES:(t + 1) * LANES]
            if part < 2:
                zc = _rope_chunk(zc, *(scaled if part == 0 else plain), PARTIAL_ROPE_DIM // 2)
            qkv_ref[:, sl] = zc.astype(qkv_ref.dtype)

    lat0 = 3 * DIFF_WIDTH
    z = _dot(h, w_ref[:, lat0:lat0 + LATENT_TILE])
    cq = _rms(z[:, :MLA_Q_RANK], gq_ref[...], NORM_EPS).astype(BF16)
    ckv = _rms(z[:, MLA_Q_RANK:MLA_Q_RANK + MLA_KV_RANK], gkv_ref[...], NORM_EPS).astype(BF16)
    kr_off = MLA_Q_RANK + MLA_KV_RANK
    c, s_lo, s_hi = cm_ref[...], slom_ref[...], shim_ref[...]
    half = MLA_ROPE_DIM // 2
    kr = _rope_chunk(z[:, kr_off:kr_off + LANES], c, s_lo, s_hi, half).astype(BF16)
    qf = _dot(cq, wq_ref[...]) * mla_scale
    kn = _dot(ckv, wk_ref[...])
    for hd in range(MLA_HEADS):
        o = hd * MLA_QK_PAD
        q_ref[:, o:o + LANES] = qf[:, o:o + LANES].astype(BF16)
        q_ref[:, o + LANES:o + 2 * LANES] = _rope_chunk(qf[:, o + LANES:o + 2 * LANES], c, s_lo, s_hi, half).astype(BF16)
        k_ref[:, o:o + LANES] = kn[:, hd * LANES:(hd + 1) * LANES].astype(BF16)
        k_ref[:, o + LANES:o + 2 * LANES] = kr
    v_ref[...] = _dot(ckv, wv_ref[...]).astype(BF16)


def _in_proj(x, g, w_main, layer, gq, wq, gkv, wk, wv, tabs_p, tabs_m, tm):
    m, k = x.shape
    g = g.reshape(1, k)
    once = lambda a: pl.BlockSpec(a.shape, lambda i: (0,) * a.ndim, pipeline_mode=pl.Buffered(1))
    tab_spec = pl.BlockSpec((tm, LANES), lambda i: (i, 0))
    nqkv, nq, nv = 3 * DIFF_WIDTH, MLA_HEADS * MLA_QK_PAD, MLA_HEADS * MLA_V_DIM
    row = lambda n: pl.BlockSpec((tm, n), lambda i: (i, 0))
    return pl.pallas_call(
        functools.partial(_in_proj_kernel, q_scale=DIFF_HEAD_DIM ** -0.5 * LOG2E,
                          mla_scale=(MLA_NOPE_DIM + MLA_ROPE_DIM) ** -0.5 * LOG2E),
        grid=(m // tm,),
        in_specs=[row(k), once(g),
                  pl.BlockSpec((None, k, 3 * DIFF_WIDTH + LATENT_TILE), lambda i: (layer, 0, 0),
                               pipeline_mode=pl.Buffered(1)),
                  once(gq), once(wq), once(gkv), once(wk), once(wv)] + [tab_spec] * 6,
        out_specs=[row(nqkv), row(nq), row(nq), row(nv)],
        out_shape=[jax.ShapeDtypeStruct((m, nqkv), BF16), jax.ShapeDtypeStruct((m, nq), BF16),
                   jax.ShapeDtypeStruct((m, nq), BF16), jax.ShapeDtypeStruct((m, nv), BF16)],
        compiler_params=_cparams("parallel"),
        name="in_proj",
    )(x, g, w_main, gq, wq, gkv, wk, wv, *tabs_p, *tabs_m)


def _attend_pipelined(a_chains, b_chains, s_ref, m_ref, n_keys):
    n = len(a_chains)
    m_prev = [m_ref[u] for u in range(n)]
    acc, m_new = [None] * n, [None] * n
    for c in range(n_keys // KEY_CHUNK):
        rows = slice(c * KEY_CHUNK, (c + 1) * KEY_CHUNK)
        for u in range(n):
            p = jnp.exp2(s_ref[u, rows, :] - m_prev[u])
            pv = _dot(b_chains[u](rows), p.astype(BF16))
            acc[u] = pv if c == 0 else acc[u] + pv
        for u, (load_k, q_t) in enumerate(a_chains):
            sc = _dot(load_k(rows), q_t)
            s_ref[u, rows, :] = sc
            mc = jnp.max(sc, axis=0, keepdims=True)
            m_new[u] = mc if c == 0 else jnp.maximum(m_new[u], mc)
    for u in range(n):
        m_ref[u] = m_new[u]
    return [acc[u][:V_DIM] / acc[u][V_DIM:V_DIM + 1] for u in range(n)]


V_DIM = MLA_V_DIM
VT_ROWS = V_DIM + 16


def _attn_scratch(n_chains, n_keys, tq, heads):
    return [pltpu.VMEM((heads * VT_ROWS, n_keys), BF16), pltpu.VMEM((n_chains, n_keys, tq), F32),
            pltpu.VMEM((n_chains, 1, tq), F32)]


def _init_attn_scratch(t, tiles_per_head, v_ref, vt_ref, s_ref, m_ref):
    @pl.when(t == 0)
    def _():
        s_ref[...] = jnp.zeros_like(s_ref)
        m_ref[...] = jnp.zeros_like(m_ref)

    @pl.when(_divmod_nonneg(jnp.maximum(t - 1, 0), tiles_per_head)[1] == 0)
    def _():
        for h in range(vt_ref.shape[0] // VT_ROWS):
            vt_ref[h * VT_ROWS:h * VT_ROWS + V_DIM, :] = _transpose_bf16(v_ref[:, h * V_DIM:(h + 1) * V_DIM])
            vt_ref[h * VT_ROWS + V_DIM:(h + 1) * VT_ROWS, :] = jnp.ones((VT_ROWS - V_DIM, vt_ref.shape[1]), BF16)


def _divmod_nonneg(u, n):
    if n & (n - 1) == 0:
        return lax.shift_right_logical(u, n.bit_length() - 1), u & (n - 1)
    return u // n, u % n


def _flat_tile(u, n_heads, tiles_per_head):
    head_flat, qi = _divmod_nonneg(u, tiles_per_head)
    bi, h = _divmod_nonneg(head_flat, n_heads)
    return bi, h, qi


def _diff_attn_kernel(lam_ref, g_ref, q_ref, k_ref, v_ref, o_ref, vt_ref, s_ref, m_ref, *, lam_init, tiles_per_head):
    _init_attn_scratch(pl.program_id(0), tiles_per_head, v_ref, vt_ref, s_ref, m_ref)
    lp = lam_ref[...]
    lam = (jnp.exp(jnp.sum(lp[0:1] * lp[1:2], axis=-1, keepdims=True))
           - jnp.exp(jnp.sum(lp[2:3] * lp[3:4], axis=-1, keepdims=True)) + lam_init)
    q_t = _transpose_bf16(q_ref[...])
    row = lax.broadcasted_iota(jnp.int32, q_t.shape, 0)
    zero = jnp.zeros_like(q_t)
    load_k = lambda rows: k_ref[rows, :]
    load_vt = lambda cols: vt_ref[:, cols]
    o1, o2 = _attend_pipelined([(load_k, jnp.where(row < DIFF_HEAD_DIM, q_t, zero)),
                                (load_k, jnp.where(row >= DIFF_HEAD_DIM, q_t, zero))],
                               [load_vt, load_vt], s_ref, m_ref, k_ref.shape[0])
    o = (o1 - lam * o2).T
    o_ref[...] = (_rms(o, g_ref[...], SUBLN_EPS) * (1.0 - lam_init)).astype(o_ref.dtype)


def _diff_attention(qkv, lam_params, g_subln, lam_init, tq):
    b, s, _ = qkv.shape
    dv = 2 * DIFF_HEAD_DIM
    nh, nq = DIFF_HEADS, s // tq
    n_tiles = b * nh * nq
    tile_a = lambda t: _flat_tile(jnp.minimum(t, n_tiles - 1), nh, nq)
    tile_b = lambda t: _flat_tile(jnp.maximum(t - 1, 0), nh, nq)

    def q_map(t):
        bi, h, qi = tile_a(t)
        return bi, qi, h

    def k_map(t):
        bi, h, _ = tile_a(t)
        return bi, 0, nh + h

    def v_map(t):
        bi, h, _ = tile_b(t)
        return bi, 0, 2 * nh + h

    def o_map(t):
        bi, h, qi = tile_b(t)
        return bi, qi, h

    return pl.pallas_call(
        functools.partial(_diff_attn_kernel, lam_init=lam_init, tiles_per_head=nq),
        grid=(n_tiles + 1,),
        in_specs=[pl.BlockSpec(lam_params.shape, lambda t: (0, 0)),
                  pl.BlockSpec((1, dv), lambda t: (0, 0)),
                  pl.BlockSpec((None, tq, dv), q_map),
                  pl.BlockSpec((None, s, dv), k_map),
                  pl.BlockSpec((None, s, dv), v_map)],
        out_specs=pl.BlockSpec((None, tq, dv), o_map),
        out_shape=jax.ShapeDtypeStruct((b, s, nh * dv), BF16),
        scratch_shapes=_attn_scratch(2, s, tq, 1),
        compiler_params=_cparams("arbitrary"),
        name="diff_attention",
    )(lam_params, g_subln.reshape(1, dv), qkv, qkv, qkv)


def _mla_attn_kernel(q_ref, k_ref, v_ref, o_ref, vt_ref, s_ref, m_ref, *, tiles_per_head):
    _init_attn_scratch(pl.program_id(0), tiles_per_head, v_ref, vt_ref, s_ref, m_ref)
    a_chains, b_chains = [], []
    for h in range(MLA_HEADS_PER_STEP):
        kc = slice(h * MLA_QK_PAD, (h + 1) * MLA_QK_PAD)
        vr = slice(h * VT_ROWS, (h + 1) * VT_ROWS)
        a_chains.append((lambda rows, kc=kc: k_ref[rows, kc], _transpose_bf16(q_ref[:, kc])))
        b_chains.append(lambda cols, vr=vr: vt_ref[vr, cols])
    outs = _attend_pipelined(a_chains, b_chains, s_ref, m_ref, k_ref.shape[0])
    for h, o in enumerate(outs):
        o_ref[:, h * MLA_V_DIM:(h + 1) * MLA_V_DIM] = o.T.astype(o_ref.dtype)


def _mla_attention(q, k, v, tq):
    b, s, _ = q.shape
    hp = MLA_HEADS_PER_STEP
    nh, dq, dv = MLA_HEADS // hp, hp * MLA_QK_PAD, hp * MLA_V_DIM
    nq = s // tq
    n_tiles = b * nh * nq
    tile_a = lambda t: _flat_tile(jnp.minimum(t, n_tiles - 1), nh, nq)
    tile_b = lambda t: _flat_tile(jnp.maximum(t - 1, 0), nh, nq)

    def q_map(t):
        bi, h, qi = tile_a(t)
        return bi, qi, h

    def k_map(t):
        bi, h, _ = tile_a(t)
        return bi, 0, h

    def v_map(t):
        bi, h, _ = tile_b(t)
        return bi, 0, h

    def o_map(t):
        bi, h, qi = tile_b(t)
        return bi, qi, h

    return pl.pallas_call(
        functools.partial(_mla_attn_kernel, tiles_per_head=nq),
        grid=(n_tiles + 1,),
        in_specs=[pl.BlockSpec((None, tq, dq), q_map), pl.BlockSpec((None, s, dq), k_map),
                  pl.BlockSpec((None, s, dv), v_map)],
        out_specs=pl.BlockSpec((None, tq, dv), o_map),
        out_shape=jax.ShapeDtypeStruct((b, s, nh * dv), BF16),
        scratch_shapes=_attn_scratch(hp, s, tq, hp),
        compiler_params=_cparams("arbitrary"),
        name="mla_attention",
    )(q, k, v)


def _mix_out_kernel(x_ref, g_ref, od_ref, om_ref, wg_ref, wbd_ref, wbm_ref, wo_ref, xo_ref, mg_ref, *, tn):
    od, om = od_ref[...], om_ref[...]
    h = _rms(x_ref[...], g_ref[...], NORM_EPS).astype(BF16)
    n = mg_ref.shape[1]
    for t in range(n // tn):
        cols = slice(t * tn, (t + 1) * tn)
        ga = _sigmoid(_dot(h, wg_ref[:, cols]))
        gb = _sigmoid(_dot(h, wg_ref[:, n + t * tn:n + (t + 1) * tn]))
        mg_ref[:, cols] = (ga * _dot(od, wbd_ref[:, cols]) + gb * _dot(om, wbm_ref[:, cols])).astype(BF16)
    xo_ref[...] = x_ref[...] + _dot(mg_ref[...], wo_ref[...])


def _mix_out(x, g, od, om, w_gates, wbd, wbm, wo, layer, tm, tn):
    m, d = x.shape
    n = wbd.shape[2]
    row = lambda a: pl.BlockSpec((tm, a.shape[1]), lambda i: (i, 0))
    resident = lambda a: pl.BlockSpec((None,) + a.shape[1:], lambda i: (layer, 0, 0), pipeline_mode=pl.Buffered(1))
    return pl.pallas_call(
        functools.partial(_mix_out_kernel, tn=tn),
        grid=(m // tm,),
        in_specs=[row(x), pl.BlockSpec((1, d), lambda i: (0, 0)), row(od), row(om), resident(w_gates), resident(wbd),
                  resident(wbm), resident(wo)],
        out_specs=row(x),
        out_shape=jax.ShapeDtypeStruct((m, d), F32),
        scratch_shapes=[pltpu.VMEM((tm, n), BF16)],
        compiler_params=_cparams("parallel"),
        name="mix_out",
    )(x, g.reshape(1, d), od, om, w_gates, wbd, wbm, wo)


def _ffn_kernel(x_ref, g_ref, wg_ref, wu_ref, wd_ref, xo_ref, h_ref):
    def down_proj(h):
        gate = _dot(h, wg_ref[...])
        act = (gate * _sigmoid(gate) * _dot(h, wu_ref[...])).astype(BF16)
        return _dot(act, wd_ref[...])

    @pl.when(pl.program_id(1) == 0)
    def _():
        x = x_ref[...]
        h = _rms(x, g_ref[...], NORM_EPS).astype(h_ref.dtype)
        h_ref[...] = h
        xo_ref[...] = x + down_proj(h)

    @pl.when(pl.program_id(1) > 0)
    def _():
        xo_ref[...] += down_proj(h_ref[...])


def _ffn(x, g, w_gate_up, w_down, layer, tm, tn):
    m, d = x.shape
    n = w_down.shape[1]
    row = pl.BlockSpec((tm, d), lambda i, j: (i, 0))
    return pl.pallas_call(
        _ffn_kernel,
        grid=(m // tm, n // tn),
        in_specs=[row, pl.BlockSpec((1, d), lambda i, j: (0, 0)),
                  pl.BlockSpec((None, d, tn), lambda i, j: (layer, 0, j)),
                  pl.BlockSpec((None, d, tn), lambda i, j: (layer, 0, j + n // tn)),
                  pl.BlockSpec((None, tn, d), lambda i, j: (layer, j, 0))],
        out_specs=row,
        out_shape=jax.ShapeDtypeStruct((m, d), F32),
        scratch_shapes=[pltpu.VMEM((tm, d), BF16)],
        compiler_params=_cparams("parallel", "arbitrary"),
        name="ffn",
    )(x, g.reshape(1, d), w_gate_up, w_gate_up, w_down)


def _ple_kernel(x_ref, g_ref, p_ref, wg_ref, wp_ref, gf_ref, o_ref, *, final_norm):
    x = x_ref[...]
    gate = _sigmoid(_dot(_rms(x, g_ref[...], NORM_EPS).astype(BF16), wg_ref[...]))
    x_new = x + gate * _dot(p_ref[...].astype(BF16), wp_ref[...])
    o_ref[...] = _rms(x_new, gf_ref[...], NORM_EPS) if final_norm else x_new


def _ple(x, g, p, wg, wp, layer, g_final, final_norm, tm):
    m, d = x.shape
    row = pl.BlockSpec((tm, d), lambda i: (i, 0))
    vec = pl.BlockSpec((1, d), lambda i: (0, 0))
    full = lambda a: pl.BlockSpec((None,) + a.shape[1:], lambda i: (layer, 0, 0), pipeline_mode=pl.Buffered(1))
    return pl.pallas_call(
        functools.partial(_ple_kernel, final_norm=final_norm),
        grid=(m // tm,),
        in_specs=[row, vec, pl.BlockSpec((None, tm, p.shape[2]), lambda i: (layer, i, 0)), full(wg), full(wp), vec],
        out_specs=row,
        out_shape=jax.ShapeDtypeStruct((m, d), F32),
        compiler_params=_cparams("parallel"),
        name="ple_gate",
    )(x, g.reshape(1, d), p, wg, wp, g_final.reshape(1, d))


def _rope_tables(positions, dim, period, pass_through):
    inv_freq = ROPE_THETA ** (-jnp.arange(0, dim, 2, dtype=F32) / dim)
    ang = inv_freq[:, None] * positions.astype(F32).reshape(1, -1)
    cos, sin = jnp.cos(ang), jnp.sin(ang)
    half, m = ang.shape
    pad = period - dim
    rest, zero = jnp.full((pad, m), pass_through, F32), jnp.zeros((pad, m), F32)
    zh = jnp.zeros((half, m), F32)
    reps = LANES // period
    c = jnp.tile(jnp.concatenate([cos, cos, rest], axis=0), (reps, 1))
    s_lo = jnp.tile(jnp.concatenate([-sin, zh, zero], axis=0), (reps, 1))
    s_hi = jnp.tile(jnp.concatenate([zh, sin, zero], axis=0), (reps, 1))
    return c.T, s_lo.T, s_hi.T


def _tile(n, pref):
    t = min(n, pref)
    assert n % t == 0, (n, t)
    return t


def kernel(x, p, positions, g_mix, w_in, lambda_q1, lambda_k1, lambda_q2, lambda_k2, g_subln, g_q_latent, w_q_up, g_kv_latent, w_kv_up, w_branch_diff, w_branch_mla, w_out, g_ffn, w_gate_up, w_down, w_ple_in, g_ple, w_ple_gate, g_final):
    b, s, d = x.shape
    depth = w_in.shape[0]
    m = b * s
    tm = _tile(m, 1024)
    tm_small = _tile(m, 512)
    tm_mix = _tile(m, 256)
    tq = _tile(s, 512)
    tn = 512
    assert s % KEY_CHUNK == 0

    tabs_p = _rope_tables(positions, PARTIAL_ROPE_DIM, DIFF_HEAD_DIM, 1.0)
    tabs_m = _rope_tables(positions, MLA_ROPE_DIM, LANES, 0.0)

    xf = x.reshape(m, d)
    lat0 = 3 * DIFF_WIDTH
    gates0 = lat0 + MLA_Q_RANK + MLA_KV_RANK + MLA_ROPE_DIM
    assert gates0 - lat0 <= LATENT_TILE
    qk_dim = MLA_NOPE_DIM + MLA_ROPE_DIM
    w_main = w_in.astype(BF16)
    w_gates = w_main[:, :, gates0:]
    w_bd, w_bm, w_o = w_branch_diff.astype(BF16), w_branch_mla.astype(BF16), w_out.astype(BF16)
    w_gu, w_dn = w_gate_up.astype(BF16), w_down.astype(BF16)
    w_pg, w_pi = w_ple_gate.astype(BF16), w_ple_in.astype(BF16)
    pf = p.reshape(depth, m, -1)
    for i in range(depth):
        lam_init = 0.8 - 0.6 * math.exp(-0.3 * i)
        w_q = jnp.pad(w_q_up[i].reshape(MLA_Q_RANK, MLA_HEADS, qk_dim),
                      ((0, 0), (0, 0), (0, MLA_QK_PAD - qk_dim))).reshape(MLA_Q_RANK, MLA_HEADS * MLA_QK_PAD).astype(BF16)
        w_kv = w_kv_up[i].reshape(MLA_KV_RANK, MLA_HEADS, MLA_NOPE_DIM + MLA_V_DIM)
        w_k = w_kv[:, :, :MLA_NOPE_DIM].reshape(MLA_KV_RANK, MLA_HEADS * MLA_NOPE_DIM).astype(BF16)
        w_v = w_kv[:, :, MLA_NOPE_DIM:].reshape(MLA_KV_RANK, MLA_HEADS * MLA_V_DIM).astype(BF16)
        lam_params = jnp.stack([lambda_q1[i], lambda_k1[i], lambda_q2[i], lambda_k2[i]]).astype(F32)

        qkv, qm, km, vm = _in_proj(xf, g_mix[i], w_main, i, g_q_latent[i].reshape(1, -1), w_q,
                                   g_kv_latent[i].reshape(1, -1), w_k, w_v, tabs_p, tabs_m, tm_small)
        od = _diff_attention(qkv.reshape(b, s, 3 * DIFF_WIDTH), lam_params, g_subln[i], lam_init, tq)
        om = _mla_attention(qm.reshape(b, s, -1), km.reshape(b, s, -1), vm.reshape(b, s, -1), tq)
        xf = _mix_out(xf, g_mix[i], od.reshape(m, -1), om.reshape(m, -1), w_gates, w_bd, w_bm, w_o, i, tm_mix, tn)

        xf = _ffn(xf, g_ffn[i], w_gu, w_dn, i, tm, tn)

        xf = _ple(xf, g_ple[i], pf, w_pg, w_pi, i, g_final, i == depth - 1, tm_small)
    return xf.reshape(b, s, d)
```
